```python
import jax, jax.numpy as jnp
from jax import lax
import numpy as np

D_MODEL = 4096
BATCH = 4
SEQ = 2048
DEPTH = 1
DEC_BATCH = 128
DEC_SEQ = 8
PAST_LEN = 16384
PAGE_SIZE = 128

EPS = 1e-6
A_WIDTH = D_MODEL // 2
A_CONV = 3
SSD_INNER = D_MODEL
SSD_HEAD_DIM = 64
SSD_HEADS = SSD_INNER // SSD_HEAD_DIM
SSD_GROUPS = 8
SSD_STATE = 128
SSD_CONV = 4
SSD_CHUNK = 128
SSD_XBC = SSD_INNER + 2 * SSD_GROUPS * SSD_STATE
N_MEM = 256
XATT_HEADS = 4
XATT_HEAD_DIM = D_MODEL // 8
XATT_WIDTH = XATT_HEADS * XATT_HEAD_DIM
N_BRANCH = 3
PEER_HEADS = 8
PEER_KEYS = 128
PEER_TOPK = 16
PEER_QDIM = 256
PEER_HALF = PEER_QDIM // 2
N_EXPERTS = PEER_KEYS * PEER_KEYS
PEER_TOKEN_BLOCK = 64
IN_SPLITS = (A_WIDTH, A_WIDTH, A_WIDTH, SSD_INNER, SSD_XBC, SSD_HEADS, XATT_WIDTH, N_BRANCH * D_MODEL)
IN_WIDTH = sum(IN_SPLITS)

kernel_name = 'hybrid_conv_ssd_xattn_peer_step'


def rmsnorm(x, g):
    x32 = x.astype(jnp.float32)
    inv = lax.rsqrt(jnp.mean(x32 * x32, axis=-1, keepdims=True) + EPS)
    return (x32 * inv).astype(x.dtype) * g


def causal_dwconv(u, prev, w):
    width = w.shape[0]
    length = u.shape[1]
    full = jnp.concatenate([prev.astype(u.dtype), u], axis=1)
    out = sum(full[:, k:k + length] * w[k] for k in range(width))
    return out, full[:, length:]


def ssd_chunked(x, dt, a, bmat, cmat, h0, chunk):
    b_, length = x.shape[:2]
    c = length // chunk
    r = SSD_HEADS // SSD_GROUPS
    xs = (x * dt[..., None]).reshape(b_, c, chunk, SSD_GROUPS, r, SSD_HEAD_DIM)
    adt = (dt * a).reshape(b_, c, chunk, SSD_GROUPS, r)
    bm = bmat.reshape(b_, c, chunk, SSD_GROUPS, SSD_STATE)
    cm = cmat.reshape(b_, c, chunk, SSD_GROUPS, SSD_STATE)
    acum = jnp.cumsum(adt, axis=2)
    causal = jnp.tril(jnp.ones((chunk, chunk), dtype=bool))[:, :, None, None]
    seg = acum[:, :, :, None] - acum[:, :, None, :]
    decay = jnp.exp(jnp.where(causal, seg, -jnp.inf))
    cb = jnp.einsum('bcign,bcjgn->bcijg', cm, bm)
    y_diag = jnp.einsum('bcijgr,bcjgrp->bcigrp', cb[..., None] * decay, xs)
    decay_states = jnp.exp(acum[:, :, -1:] - acum)
    chunk_states = jnp.einsum('bcjgn,bcjgr,bcjgrp->bcgrpn', bm, decay_states, xs)
    chunk_decay = jnp.exp(acum[:, :, -1])

    def step(h, inp):
        s, d = inp
        return h * d[..., None, None] + s, h

    h_last, h_prev = lax.scan(
        step, h0.reshape(b_, SSD_GROUPS, r, SSD_HEAD_DIM, SSD_STATE),
        (jnp.moveaxis(chunk_states, 1, 0), jnp.moveaxis(chunk_decay, 1, 0)))
    h_prev = jnp.moveaxis(h_prev, 0, 1)
    y_off = jnp.einsum('bcign,bcgrpn,bcigr->bcigrp', cm, h_prev, jnp.exp(acum))
    y = (y_diag + y_off).reshape(b_, length, SSD_HEADS, SSD_HEAD_DIM)
    return y, h_last.reshape(b_, SSD_HEADS, SSD_HEAD_DIM, SSD_STATE)


def memory_kv(mem, norm_mem, w_mem_k, w_mem_v):
    b_, m = mem.shape[:2]
    mn = rmsnorm(mem, norm_mem)
    k = (mn @ w_mem_k).reshape(b_, m, XATT_HEADS, XATT_HEAD_DIM)
    v = (mn @ w_mem_v).reshape(b_, m, XATT_HEADS, XATT_HEAD_DIM)
    return k, v


def hybrid_mixer(xn, mem_k, mem_v, conv_a_prev, ssd_conv_prev, ssd_h_prev, chunk,
                 w_in, a_conv_w, a_out, ssd_conv_w, ssd_conv_b, ssd_dt_bias, ssd_a_log,
                 ssd_d, ssd_norm, ssd_out, xatt_out, w_o):
    f32 = jnp.float32
    b_, length, _ = xn.shape
    proj = xn @ w_in
    cuts = [int(i) for i in np.cumsum(IN_SPLITS)[:-1]]
    a_in, a_bg, a_cg, z, xbc, dt_raw, q, gates = jnp.split(proj, cuts, axis=-1)

    conv_a_out, conv_a_new = causal_dwconv(a_cg * a_in, conv_a_prev, a_conv_w)
    h_a = (a_bg * conv_a_out) @ a_out

    xbc_conv, ssd_conv_new = causal_dwconv(xbc, ssd_conv_prev, ssd_conv_w)
    xbc_conv = jax.nn.silu(xbc_conv + ssd_conv_b)
    xs, bm, cm = jnp.split(xbc_conv, [SSD_INNER, SSD_INNER + SSD_GROUPS * SSD_STATE], axis=-1)
    xs = xs.reshape(b_, length, SSD_HEADS, SSD_HEAD_DIM).astype(f32)
    dt = jax.nn.softplus(dt_raw.astype(f32) + ssd_dt_bias.astype(f32))
    a = -jnp.exp(ssd_a_log.astype(f32))
    y, ssd_h_new = ssd_chunked(
        xs, dt, a,
        bm.reshape(b_, length, SSD_GROUPS, SSD_STATE).astype(f32),
        cm.reshape(b_, length, SSD_GROUPS, SSD_STATE).astype(f32),
        ssd_h_prev.astype(f32), chunk)
    y = y + ssd_d.astype(f32)[:, None] * xs
    y = y.reshape(b_, length, SSD_INNER).astype(xn.dtype) * jax.nn.silu(z)
    y = rmsnorm(y.reshape(b_, length, SSD_GROUPS, SSD_INNER // SSD_GROUPS),
                ssd_norm.reshape(SSD_GROUPS, SSD_INNER // SSD_GROUPS)).reshape(b_, length, SSD_INNER)
    h_b = y @ ssd_out

    qh = q.reshape(b_, length, XATT_HEADS, XATT_HEAD_DIM)
    s = jnp.einsum('blhd,bmhd->bhlm', qh, mem_k).astype(f32) * (XATT_HEAD_DIM ** -0.5)
    p = jax.nn.softmax(s, axis=-1).astype(xn.dtype)
    o = jnp.einsum('bhlm,bmhd->blhd', p, mem_v).reshape(b_, length, XATT_WIDTH)
    h_c = o @ xatt_out

    g_a, g_b, g_c = jnp.split(jax.nn.sigmoid(gates), N_BRANCH, axis=-1)
    out = (g_a * h_a + g_b * h_b + g_c * h_c) @ w_o
    return out, conv_a_new, ssd_conv_new, ssd_h_new


def peer_ffn(xn, peer_wq, peer_subkeys, peer_u, peer_v):
    shp = xn.shape
    x2 = xn.reshape(-1, D_MODEL)
    t = x2.shape[0]
    q = (x2 @ peer_wq).reshape(t, PEER_HEADS, 2, PEER_HALF)
    s = jnp.einsum('thpd,hpkd->thpk', q, peer_subkeys).astype(jnp.float32)
    s1, i1 = lax.top_k(s[:, :, 0], PEER_TOPK)
    s2, i2 = lax.top_k(s[:, :, 1], PEER_TOPK)
    cand = (s1[..., :, None] + s2[..., None, :]).reshape(t, PEER_HEADS, PEER_TOPK * PEER_TOPK)
    cidx = (i1[..., :, None] * PEER_KEYS + i2[..., None, :]).reshape(t, PEER_HEADS, PEER_TOPK * PEER_TOPK)
    top, pos = lax.top_k(cand, PEER_TOPK)
    eidx = jnp.take_along_axis(cidx, pos, axis=-1).reshape(t, PEER_HEADS * PEER_TOPK)
    gate = jax.nn.softmax(top, axis=-1).reshape(t, PEER_HEADS * PEER_TOPK).astype(xn.dtype)
    nb = -(-t // PEER_TOKEN_BLOCK)
    pad = nb * PEER_TOKEN_BLOCK - t
    xb = jnp.pad(x2, ((0, pad), (0, 0))).reshape(nb, PEER_TOKEN_BLOCK, D_MODEL)
    ib = jnp.pad(eidx, ((0, pad), (0, 0))).reshape(nb, PEER_TOKEN_BLOCK, PEER_HEADS * PEER_TOPK)
    gb = jnp.pad(gate, ((0, pad), (0, 0))).reshape(nb, PEER_TOKEN_BLOCK, PEER_HEADS * PEER_TOPK)

    def block(args):
        xt, it, gt = args
        u = jnp.take(peer_u, it, axis=0)
        hid = jax.nn.gelu(jnp.einsum('td,tkd->tk', xt, u), approximate=False)
        v = jnp.take(peer_v, it, axis=0)
        return jnp.einsum('tk,tkd->td', hid * gt, v)

    out = lax.map(block, (xb, ib, gb)).reshape(nb * PEER_TOKEN_BLOCK, D_MODEL)[:t]
    return out.reshape(shp)


def setup_inputs(seed: int = 0) -> dict:
    key = jax.random.key(seed)
    ks = jax.random.split(key, 32)
    f32 = jnp.float32
    nrm = lambda k, shape, scale: jax.random.normal(k, shape, f32) * scale
    dt0 = jnp.exp(jax.random.uniform(ks[13], (DEPTH, SSD_HEADS), f32, np.log(1e-3), np.log(1e-1)))
    return {
        'x_prompt': nrm(ks[0], (BATCH, SEQ, D_MODEL), 1.0),
        'x_sample': nrm(ks[1], (DEC_BATCH, DEC_SEQ, D_MODEL), 1.0),
        'mem_prompt': nrm(ks[2], (BATCH, N_MEM, D_MODEL), 1.0),
        'cache_mem_k': nrm(ks[3], (DEPTH, DEC_BATCH, N_MEM, XATT_HEADS, XATT_HEAD_DIM), 1.0),
        'cache_mem_v': nrm(ks[4], (DEPTH, DEC_BATCH, N_MEM, XATT_HEADS, XATT_HEAD_DIM), 1.0),
        'state_conv_a': nrm(ks[5], (DEPTH, DEC_BATCH, A_CONV - 1, A_WIDTH), 1.0),
        'state_ssd_conv': nrm(ks[6], (DEPTH, DEC_BATCH, SSD_CONV - 1, SSD_XBC), 1.0),
        'state_ssd': nrm(ks[7], (DEPTH, DEC_BATCH, SSD_HEADS, SSD_HEAD_DIM, SSD_STATE), 0.5),
        'norm_mix': 1.0 + nrm(ks[8], (DEPTH, D_MODEL), 0.02),
        'norm_mem': 1.0 + nrm(ks[9], (DEPTH, D_MODEL), 0.02),
        'norm_ffn': 1.0 + nrm(ks[10], (DEPTH, D_MODEL), 0.02),
        'norm_final': 1.0 + nrm(ks[11], (D_MODEL,), 0.02),
        'w_in': nrm(ks[12], (DEPTH, D_MODEL, IN_WIDTH), D_MODEL ** -0.5),
        'a_conv_w': nrm(ks[14], (DEPTH, A_CONV, A_WIDTH), A_CONV ** -0.5),
        'a_out': nrm(ks[15], (DEPTH, A_WIDTH, D_MODEL), A_WIDTH ** -0.5),
        'ssd_conv_w': nrm(ks[16], (DEPTH, SSD_CONV, SSD_XBC), SSD_CONV ** -0.5),
        'ssd_conv_b': nrm(ks[17], (DEPTH, SSD_XBC), 0.02),
        'ssd_dt_bias': dt0 + jnp.log(-jnp.expm1(-dt0)),
        'ssd_a_log': jnp.log(jax.random.uniform(ks[18], (DEPTH, SSD_HEADS), f32, 1.0, 16.0)),
        'ssd_d': 1.0 + nrm(ks[19], (DEPTH, SSD_HEADS), 0.02),
        'ssd_norm': 1.0 + nrm(ks[20], (DEPTH, SSD_INNER), 0.02),
        'ssd_out': nrm(ks[21], (DEPTH, SSD_INNER, D_MODEL), SSD_INNER ** -0.5),
        'w_mem_k': nrm(ks[22], (DEPTH, D_MODEL, XATT_WIDTH), D_MODEL ** -0.5),
        'w_mem_v': nrm(ks[23], (DEPTH, D_MODEL, XATT_WIDTH), D_MODEL ** -0.5),
        'xatt_out': nrm(ks[24], (DEPTH, XATT_WIDTH, D_MODEL), XATT_WIDTH ** -0.5),
        'w_o': nrm(ks[25], (DEPTH, D_MODEL, D_MODEL), D_MODEL ** -0.5),
        'peer_wq': nrm(ks[26], (DEPTH, D_MODEL, PEER_HEADS * PEER_QDIM), D_MODEL ** -0.5),
        'peer_subkeys': nrm(ks[27], (DEPTH, PEER_HEADS, 2, PEER_KEYS, PEER_HALF), PEER_HALF ** -0.5),
        'peer_u': nrm(ks[28], (DEPTH, N_EXPERTS, D_MODEL), D_MODEL ** -0.5),
        'peer_v': nrm(ks[29], (DEPTH, N_EXPERTS, D_MODEL), (PEER_HEADS * PEER_TOPK) ** -0.5),
    }


def reference(x_prompt, x_sample, mem_prompt, cache_mem_k, cache_mem_v, state_conv_a,
              state_ssd_conv, state_ssd, norm_mix, norm_mem, norm_ffn, norm_final, w_in,
              a_conv_w, a_out, ssd_conv_w, ssd_conv_b, ssd_dt_bias, ssd_a_log, ssd_d, ssd_norm,
              ssd_out, w_mem_k, w_mem_v, xatt_out, w_o, peer_wq, peer_subkeys, peer_u, peer_v):
    f32 = jnp.float32
    xp, xs = x_prompt, x_sample
    bp, lp = x_prompt.shape[:2]
    ls = x_sample.shape[1]
    chunk_p = SSD_CHUNK if lp % SSD_CHUNK == 0 else lp
    mk_p_l, mv_p_l, ca_p_l, sc_p_l, st_p_l, ca_s_l, sc_s_l, st_s_l = [], [], [], [], [], [], [], []
    for l in range(DEPTH):
        mix_w = (w_in[l], a_conv_w[l], a_out[l], ssd_conv_w[l], ssd_conv_b[l], ssd_dt_bias[l],
                 ssd_a_log[l], ssd_d[l], ssd_norm[l], ssd_out[l], xatt_out[l], w_o[l])
        ffn_w = (peer_wq[l], peer_subkeys[l], peer_u[l], peer_v[l])
        mem_k_p, mem_v_p = memory_kv(mem_prompt, norm_mem[l], w_mem_k[l], w_mem_v[l])
        h, ca_p, sc_p, st_p = hybrid_mixer(
            rmsnorm(xp, norm_mix[l]), mem_k_p, mem_v_p,
            jnp.zeros((bp, A_CONV - 1, A_WIDTH), xp.dtype),
            jnp.zeros((bp, SSD_CONV - 1, SSD_XBC), xp.dtype),
            jnp.zeros((bp, SSD_HEADS, SSD_HEAD_DIM, SSD_STATE), f32),
            chunk_p, *mix_w)
        xp = xp + h
        xp = xp + peer_ffn(rmsnorm(xp, norm_ffn[l]), *ffn_w)
        h, ca_s, sc_s, st_s = hybrid_mixer(
            rmsnorm(xs, norm_mix[l]), cache_mem_k[l], cache_mem_v[l],
            state_conv_a[l], state_ssd_conv[l], state_ssd[l], ls, *mix_w)
        xs = xs + h
        xs = xs + peer_ffn(rmsnorm(xs, norm_ffn[l]), *ffn_w)
        mk_p_l.append(mem_k_p)
        mv_p_l.append(mem_v_p)
        ca_p_l.append(ca_p)
        sc_p_l.append(sc_p)
        st_p_l.append(st_p)
        ca_s_l.append(ca_s)
        sc_s_l.append(sc_s)
        st_s_l.append(st_s)
    y_prompt = rmsnorm(xp, norm_final)
    y_sample = rmsnorm(xs, norm_final)
    return (y_prompt, y_sample,
            jnp.stack(mk_p_l), jnp.stack(mv_p_l), jnp.stack(ca_p_l), jnp.stack(sc_p_l), jnp.stack(st_p_l),
            jnp.stack(ca_s_l), jnp.stack(sc_s_l), jnp.stack(st_s_l))
```

```python
import dataclasses
import functools

import jax
import jax.numpy as jnp
import numpy as np
from jax import lax
from jax.experimental import pallas as pl
from jax.experimental.pallas import tpu as pltpu

F32 = jnp.float32
BF16 = jnp.bfloat16
EPS = 1e-6
HIGHEST = lax.Precision.HIGHEST
NEG_INF = float("-inf")

LANES = 128
SUBLANES = 8
VMEM_LIMIT_BYTES = 56 * 1024 * 1024


@dataclasses.dataclass(frozen=True)
class Dims:
    d_model: int = 4096
    batch: int = 4
    seq: int = 2048
    dec_batch: int = 128
    dec_seq: int = 8
    a_width: int = 2048
    a_conv: int = 3
    ssd_inner: int = 4096
    ssd_head_dim: int = 64
    ssd_groups: int = 8
    ssd_state: int = 128
    ssd_conv: int = 4
    ssd_chunk: int = 128
    n_mem: int = 256
    xatt_heads: int = 4
    xatt_head_dim: int = 512
    peer_heads: int = 8
    peer_keys: int = 128
    peer_topk: int = 16
    peer_qdim: int = 256

    @property
    def ssd_heads(self):
        return self.ssd_inner // self.ssd_head_dim

    @property
    def heads_per_group(self):
        return self.ssd_heads // self.ssd_groups

    @property
    def group_width(self):
        return self.ssd_inner // self.ssd_groups

    @property
    def ssd_xbc(self):
        return self.ssd_inner + 2 * self.ssd_groups * self.ssd_state

    @property
    def xatt_width(self):
        return self.xatt_heads * self.xatt_head_dim

    @property
    def n_experts(self):
        return self.peer_keys * self.peer_keys

    @property
    def t_prompt(self):
        return self.batch * self.seq

    @property
    def t_sample(self):
        return self.dec_batch * self.dec_seq

    @property
    def tokens(self):
        return self.t_prompt + self.t_sample

    @property
    def off_ain(self):
        return 0

    @property
    def off_abg(self):
        return self.a_width

    @property
    def off_acg(self):
        return 2 * self.a_width

    @property
    def off_z(self):
        return 3 * self.a_width

    @property
    def off_xbc(self):
        return self.off_z + self.ssd_inner

    @property
    def off_q(self):
        return self.off_xbc + self.ssd_xbc

    @property
    def off_gates(self):
        return self.off_q + self.xatt_width

    @property
    def proj_width(self):
        return self.off_gates + 3 * self.d_model


FULL = Dims()


def _params(*sem):
    return pltpu.CompilerParams(dimension_semantics=sem, vmem_limit_bytes=VMEM_LIMIT_BYTES)


def _pick(n, pref):
    t = min(n, pref)
    while n % t:
        t //= 2
    return t


def _rmsnorm_kernel(x_ref, g_ref, o_ref):
    x = x_ref[...]
    inv = lax.rsqrt(jnp.mean(x * x, axis=-1, keepdims=True) + EPS)
    o_ref[...] = ((x * inv) * g_ref[...]).astype(o_ref.dtype)


def _add_rmsnorm_kernel(x_ref, r_ref, g_ref, o_ref):
    x = x_ref[...] + r_ref[...]
    inv = lax.rsqrt(jnp.mean(x * x, axis=-1, keepdims=True) + EPS)
    o_ref[...] = ((x * inv) * g_ref[...]).astype(o_ref.dtype)


def rmsnorm(x, g, out_dtype, residual=None, tm=256):
    m, d = x.shape
    tm = _pick(m, tm)
    row = pl.BlockSpec((tm, d), lambda i: (i, 0))
    gspec = pl.BlockSpec((1, d), lambda i: (0, 0))
    args = (x,) if residual is None else (x, residual)
    return pl.pallas_call(
        _rmsnorm_kernel if residual is None else _add_rmsnorm_kernel,
        grid=(m // tm,),
        in_specs=[row] * len(args) + [gspec],
        out_specs=row,
        out_shape=jax.ShapeDtypeStruct((m, d), out_dtype),
        compiler_params=_params("parallel"),
        name="rmsnorm" if residual is None else "add_rmsnorm",
    )(*args, g.reshape(1, d))


def _mm_kernel(a_ref, b_ref, o_ref):
    o_ref[...] = jnp.dot(a_ref[...], b_ref[...], preferred_element_type=F32).astype(o_ref.dtype)


def _mm_res_kernel(a_ref, b_ref, r_ref, o_ref):
    acc = jnp.dot(a_ref[...], b_ref[...], preferred_element_type=F32)
    o_ref[...] = (r_ref[...] + acc).astype(o_ref.dtype)


def matmul(a, b, out_dtype, residual=None, tm=512, tn=1024, name="matmul"):
    m, k = a.shape
    n = b.shape[1]
    tm, tn = _pick(m, tm), _pick(n, tn)
    in_specs = [pl.BlockSpec((tm, k), lambda j, i: (i, 0)), pl.BlockSpec((k, tn), lambda j, i: (0, j))]
    args = [a, b]
    if residual is not None:
        in_specs.append(pl.BlockSpec((tm, tn), lambda j, i: (i, j)))
        args.append(residual)
    return pl.pallas_call(
        _mm_kernel if residual is None else _mm_res_kernel,
        grid=(n // tn, m // tm),
        in_specs=in_specs,
        out_specs=pl.BlockSpec((tm, tn), lambda j, i: (i, j)),
        out_shape=jax.ShapeDtypeStruct((m, n), out_dtype),
        compiler_params=_params("parallel", "parallel"),
        name=name,
    )(*args)


def _conv_kernel(*refs, width, tl, gated):
    if gated:
        in_ref, bg_ref, cg_ref, prev_ref, w_ref, o_ref, st_ref, scr = refs
    else:
        in_ref, prev_ref, w_ref, bias_ref, o_ref, scr = refs
    lt = pl.program_id(2)

    @pl.when(lt == 0)
    def _():
        scr[0:SUBLANES, :] = prev_ref[0]

    u = cg_ref[...] * in_ref[...] if gated else in_ref[...]
    scr[SUBLANES:SUBLANES + tl, :] = u
    acc = w_ref[width - 1:width, :] * u
    for k in range(width - 1):
        lo = SUBLANES - (width - 1 - k)
        acc = acc + w_ref[k:k + 1, :] * scr[lo:lo + tl, :]
    tail = scr[tl:tl + SUBLANES, :]
    scr[0:SUBLANES, :] = tail
    if gated:
        o_ref[...] = (bg_ref[...] * acc).astype(o_ref.dtype)
        st_ref[0] = tail
    else:
        y = acc + bias_ref[...]
        o_ref[...] = (y * jax.nn.sigmoid(y)).astype(o_ref.dtype)


def causal_conv(proj, prev8, w, *, row0, nseq, length, cols, col_offs, gated, bias=None, out_dtype=F32,
                tl=512, tc=512):
    width = w.shape[0]
    tl, tc = _pick(length, tl), _pick(cols, tc)
    while any(o % tc for o in col_offs):
        tc //= 2
    nl = length // tl
    rb0 = row0 // tl
    assert row0 % tl == 0 and all(o % tc == 0 for o in col_offs)

    def blk(off):
        return pl.BlockSpec((tl, tc), lambda s, c, l, off=off: (rb0 + s * nl + l, off // tc + c))

    prev_spec = pl.BlockSpec((1, SUBLANES, tc), lambda s, c, l: (s, 0, c))
    w_spec = pl.BlockSpec((width, tc), lambda s, c, l: (0, c))
    out_spec = pl.BlockSpec((tl, tc), lambda s, c, l: (s * nl + l, c))
    out_shape = jax.ShapeDtypeStruct((nseq * length, cols), out_dtype)
    if gated:
        in_specs = [blk(col_offs[0]), blk(col_offs[1]), blk(col_offs[2]), prev_spec, w_spec]
        args = (proj, proj, proj, prev8, w)
        out_specs = [out_spec, pl.BlockSpec((1, SUBLANES, tc), lambda s, c, l: (s, 0, c))]
        out_shape = [out_shape, jax.ShapeDtypeStruct((nseq, SUBLANES, cols), F32)]
    else:
        in_specs = [blk(col_offs[0]), prev_spec, w_spec, pl.BlockSpec((1, tc), lambda s, c, l: (0, c))]
        args = (proj, prev8, w, bias.reshape(1, cols))
        out_specs = out_spec
    return pl.pallas_call(
        functools.partial(_conv_kernel, width=width, tl=tl, gated=gated),
        grid=(nseq, cols // tc, nl),
        in_specs=in_specs,
        out_specs=out_specs,
        out_shape=out_shape,
        scratch_shapes=[pltpu.VMEM((tl + SUBLANES, tc), F32)],
        compiler_params=_params("parallel", "parallel", "arbitrary"),
        name="gated_conv" if gated else "ssd_conv",
    )(*args)


def _softplus(x):
    return jnp.maximum(x, 0.0) + jnp.log1p(jnp.exp(-jnp.abs(x)))


def _ssd_kernel(*refs, nseq, dims, chained):
    if chained:
        (xs_ref, b_ref, c_ref, z_ref, dt_ref, bias_ref, alog_ref, d_ref, nw_ref, sel_ref,
         y_ref, hout_ref, h_scr) = refs
    else:
        (xs_ref, b_ref, c_ref, z_ref, dt_ref, bias_ref, alog_ref, d_ref, nw_ref, sel_ref, h0_ref,
         y_ref, hout_ref) = refs
    q = dims.ssd_chunk
    r_heads = dims.heads_per_group
    p = dims.ssd_head_dim
    gw = dims.group_width
    seg = q // nseq

    if chained:
        @pl.when(pl.program_id(2) == 0)
        def _():
            h_scr[...] = jnp.zeros_like(h_scr)

    row = lax.broadcasted_iota(jnp.int32, (q, q), 0)
    col = lax.broadcasted_iota(jnp.int32, (q, q), 1)
    same = (row // seg) == (col // seg)
    dt = _softplus(dt_ref[0] + bias_ref[0])
    adt = dt * (-jnp.exp(alog_ref[0]))
    cum_mask = jnp.where(same & (row <= col), 1.0, 0.0).astype(F32)
    acum_t = jnp.dot(adt, cum_mask, precision=HIGHEST, preferred_element_type=F32)
    if nseq == 1:
        atot_t = jnp.broadcast_to(acum_t[:, q - 1:q], (r_heads, q))
    else:
        atot_t = jnp.dot(adt, jnp.where(same, 1.0, 0.0).astype(F32), precision=HIGHEST,
                         preferred_element_type=F32)
    stack = jnp.concatenate(
        [acum_t, dt, dt * jnp.exp(atot_t - acum_t), jnp.exp(acum_t),
         jnp.zeros((q - 4 * r_heads, q), F32)], axis=0)
    cols_form = stack.T
    expand = jnp.dot(cols_form, sel_ref[...], precision=HIGHEST, preferred_element_type=F32)
    e_dt, e_st, e_ac = expand[:, :gw], expand[:, gw:2 * gw], expand[:, 2 * gw:]

    x = xs_ref[...]
    bb = b_ref[...].astype(BF16)
    cb_ = c_ref[...].astype(BF16)
    cb = lax.dot_general(cb_, bb, (((1,), (1,)), ((), ())), preferred_element_type=F32)
    x_dt = x * e_dt
    x_dt_b = x_dt.astype(BF16)
    causal = same & (row >= col)
    lane = lax.broadcasted_iota(jnp.int32, (q, LANES), 1)
    heads_per_tile = LANES // p

    y_tiles = []
    for tile in range(gw // LANES):
        xt = x_dt_b[:, tile * LANES:(tile + 1) * LANES]
        acc = None
        for k in range(heads_per_tile):
            r = tile * heads_per_tile + k
            segm = cols_form[:, r:r + 1] - acum_t[r:r + 1, :]
            decay = jnp.exp(jnp.where(causal, segm, NEG_INF))
            m_r = (cb * decay).astype(BF16)
            x_r = jnp.where((lane >= k * p) & (lane < (k + 1) * p), xt, jnp.zeros_like(xt))
            part = jnp.dot(m_r, x_r, preferred_element_type=F32)
            acc = part if acc is None else acc + part
        y_tiles.append(acc)
    y = jnp.concatenate(y_tiles, axis=1)

    x_st = x * e_st
    eac_t = jnp.exp(atot_t)

    if chained:
        h = h_scr[...]
        y_off = lax.dot_general(cb_, h.astype(BF16), (((1,), (1,)), ((), ())),
                                preferred_element_type=F32)
        y = y + y_off * e_ac
        s_new = lax.dot_general(x_st.astype(BF16), bb, (((0,), (0,)), ((), ())),
                                preferred_element_type=F32)
        scale = jnp.broadcast_to(eac_t[:, q - 1:q], (r_heads, LANES))
        for r in range(r_heads):
            rows = slice(r * p, (r + 1) * p)
            h_scr[rows, :] = h[rows, :] * jnp.broadcast_to(scale[r:r + 1, :], (p, LANES)) + s_new[rows, :]
        for r in range(r_heads):
            hout_ref[0, r] = h_scr[r * p:(r + 1) * p, :]
    else:
        x_st_t = x_st.T
        rows_q = lax.broadcasted_iota(jnp.int32, (q, gw), 0)
        lanes_q = lax.broadcasted_iota(jnp.int32, (gw, q), 1)
        y_off = jnp.zeros((q, gw), F32)
        for s in range(nseq):
            h_s = h0_ref[s].reshape(r_heads * p, dims.ssd_state)
            y_s = lax.dot_general(cb_, h_s.astype(BF16), (((1,), (1,)), ((), ())),
                                  preferred_element_type=F32)
            y_off = jnp.where(rows_q // seg == s, y_s, y_off)
            xs_s = jnp.where(lanes_q // seg == s, x_st_t, 0.0).astype(BF16)
            s_new = jnp.dot(xs_s, bb, preferred_element_type=F32)
            scale = jnp.broadcast_to(eac_t[:, s * seg:s * seg + 1], (r_heads, LANES))
            for r in range(r_heads):
                rows = slice(r * p, (r + 1) * p)
                hout_ref[s, r] = (h_s[rows, :] * jnp.broadcast_to(scale[r:r + 1, :], (p, LANES))
                                  + s_new[rows, :])
        y = y + y_off * e_ac

    y = y + d_ref[...] * x
    z = z_ref[...]
    y = y * (z * jax.nn.sigmoid(z))
    inv = lax.rsqrt(jnp.mean(y * y, axis=-1, keepdims=True) + EPS)
    y_ref[...] = ((y * inv) * nw_ref[...]).astype(y_ref.dtype)


def _ssd_selector(dims):
    r_heads, p, gw = dims.heads_per_group, dims.ssd_head_dim, dims.group_width
    sel = np.zeros((dims.ssd_chunk, 3 * gw), np.float32)
    for part in range(3):
        for r in range(r_heads):
            sel[(part + 1) * r_heads + r, part * gw + r * p:part * gw + (r + 1) * p] = 1.0
    return jnp.asarray(sel)


def ssd_block(dims, xbc, proj, dt_t, bias_g, alog_g, d_exp, norm_w, *, row0, nrows, chained, h0=None):
    q = dims.ssd_chunk
    g = dims.ssd_groups
    gw = dims.group_width
    n = dims.ssd_state
    r_heads, p = dims.heads_per_group, dims.ssd_head_dim
    assert LANES % p == 0 and q == LANES and 4 * r_heads <= q
    sel = _ssd_selector(dims)
    rb0 = row0 // q
    zoff = dims.off_z // gw
    boff = dims.ssd_inner // n
    coff = boff + g

    if chained:
        nb, nc = dims.batch, dims.seq // q
        grid = (nb, g, nc)
        rowblk = lambda b, gi, c: b * nc + c
        sems = ("parallel", "parallel", "arbitrary")
        nseq = 1
    else:
        nseq = q // dims.dec_seq
        nb = dims.dec_batch // nseq
        grid = (nb, g)
        rowblk = lambda b, gi: b
        sems = ("parallel", "parallel")

    def spec(shape, fn):
        return pl.BlockSpec(shape, fn)

    if chained:
        ix = lambda f: (lambda b, gi, c: f(rowblk(b, gi, c), gi))
    else:
        ix = lambda f: (lambda b, gi: f(rowblk(b, gi), gi))

    in_specs = [
        spec((q, gw), ix(lambda rb, gi: (rb, gi))),
        spec((q, n), ix(lambda rb, gi: (rb, boff + gi))),
        spec((q, n), ix(lambda rb, gi: (rb, coff + gi))),
        spec((q, gw), ix(lambda rb, gi: (rb0 + rb, zoff + gi))),
        spec((1, r_heads, q), ix(lambda rb, gi: (gi, 0, rb0 + rb))),
        spec((1, r_heads, 1), ix(lambda rb, gi: (gi, 0, 0))),
        spec((1, r_heads, 1), ix(lambda rb, gi: (gi, 0, 0))),
        spec((1, gw), ix(lambda rb, gi: (0, gi))),
        spec((1, gw), ix(lambda rb, gi: (0, gi))),
        spec((q, 3 * gw), ix(lambda rb, gi: (0, 0))),
    ]
    args = [xbc, xbc, xbc, proj, dt_t, bias_g, alog_g, d_exp, norm_w, sel]
    y_spec = spec((q, gw), ix(lambda rb, gi: (rb, gi)))
    if chained:
        h_spec = pl.BlockSpec((1, r_heads, p, n), lambda b, gi, c: (b, gi, 0, 0))
        h_shape = jax.ShapeDtypeStruct((nb, dims.ssd_heads, p, n), F32)
        scratch = [pltpu.VMEM((r_heads * p, n), F32)]
    else:
        in_specs.append(pl.BlockSpec((nseq, r_heads, p, n), lambda b, gi: (b, gi, 0, 0)))
        args.append(h0)
        h_spec = pl.BlockSpec((nseq, r_heads, p, n), lambda b, gi: (b, gi, 0, 0))
        h_shape = jax.ShapeDtypeStruct((dims.dec_batch, dims.ssd_heads, p, n), F32)
        scratch = []
    return pl.pallas_call(
        functools.partial(_ssd_kernel, nseq=nseq, dims=dims, chained=chained),
        grid=grid,
        in_specs=in_specs,
        out_specs=[y_spec, h_spec],
        out_shape=[jax.ShapeDtypeStruct((nrows, dims.ssd_inner), BF16), h_shape],
        scratch_shapes=scratch,
        compiler_params=_params(*sems),
        name="ssd_prompt" if chained else "ssd_sample",
    )(*args)


def _xattn_kernel(q_ref, k_ref, v_ref, o_ref, *, heads, head_dim):
    scale = head_dim ** -0.5
    for h in range(heads):
        cs = slice(h * head_dim, (h + 1) * head_dim)
        qh = q_ref[:, cs].astype(BF16)
        kh = k_ref[0, :, cs].astype(BF16)
        vh = v_ref[0, :, cs].astype(BF16)
        s = lax.dot_general(qh, kh, (((1,), (1,)), ((), ())), preferred_element_type=F32) * scale
        s = s - jnp.max(s, axis=-1, keepdims=True)
        e = jnp.exp(s)
        pr = (e / jnp.sum(e, axis=-1, keepdims=True)).astype(BF16)
        o_ref[:, cs] = jnp.dot(pr, vh, preferred_element_type=F32).astype(o_ref.dtype)


def cross_attention(dims, proj, k, v, *, row0, nseq, length, k_col, v_col, tq=512):
    w = dims.xatt_width
    tq = _pick(length, tq)
    nq = length // tq
    rb0 = row0 // tq
    assert row0 % tq == 0 and dims.off_q % w == 0
    qoff = dims.off_q // w
    return pl.pallas_call(
        functools.partial(_xattn_kernel, heads=dims.xatt_heads, head_dim=dims.xatt_head_dim),
        grid=(nseq, nq),
        in_specs=[
            pl.BlockSpec((tq, w), lambda b, i: (rb0 + b * nq + i, qoff)),
            pl.BlockSpec((1, dims.n_mem, w), lambda b, i: (b, 0, k_col)),
            pl.BlockSpec((1, dims.n_mem, w), lambda b, i: (b, 0, v_col)),
        ],
        out_specs=pl.BlockSpec((tq, w), lambda b, i: (b * nq + i, 0)),
        out_shape=jax.ShapeDtypeStruct((nseq * length, w), BF16),
        compiler_params=_params("parallel", "parallel"),
        name="cross_attention",
    )(proj, k, v)


def _merge_kernel(va_ref, yb_ref, oc_ref, wa_ref, wb_ref, wc_ref, ga_ref, gb_ref, gc_ref, o_ref):
    ha = jnp.dot(va_ref[...], wa_ref[...], preferred_element_type=F32)
    hb = jnp.dot(yb_ref[...], wb_ref[...], preferred_element_type=F32)
    hc = jnp.dot(oc_ref[...], wc_ref[...], preferred_element_type=F32)
    mix = (jax.nn.sigmoid(ga_ref[...]) * ha + jax.nn.sigmoid(gb_ref[...]) * hb
           + jax.nn.sigmoid(gc_ref[...]) * hc)
    o_ref[...] = mix.astype(o_ref.dtype)


def merge_branches(dims, va, yb, oc, wa, wb, wc, proj, tm=512, tn=512):
    m = va.shape[0]
    d = dims.d_model
    tm, tn = _pick(m, tm), _pick(d, tn)
    g0 = dims.off_gates // tn
    gd = d // tn
    assert dims.off_gates % tn == 0

    def lhs(width):
        return pl.BlockSpec((tm, width), lambda j, i: (i, 0))

    def rhs(width):
        return pl.BlockSpec((width, tn), lambda j, i: (0, j))

    def gate(k):
        return pl.BlockSpec((tm, tn), lambda j, i, k=k: (i, g0 + k * gd + j))

    return pl.pallas_call(
        _merge_kernel,
        grid=(d // tn, m // tm),
        in_specs=[lhs(va.shape[1]), lhs(yb.shape[1]), lhs(oc.shape[1]),
                  rhs(wa.shape[0]), rhs(wb.shape[0]), rhs(wc.shape[0]), gate(0), gate(1), gate(2)],
        out_specs=pl.BlockSpec((tm, tn), lambda j, i: (i, j)),
        out_shape=jax.ShapeDtypeStruct((m, d), BF16),
        compiler_params=_params("parallel", "parallel"),
        name="merge_branches",
    )(va, yb, oc, wa, wb, wc, proj, proj, proj)


def _peer_select_kernel(q_ref, keys_ref, s1_ref, e1_ref, s2_ref, e2_ref, tau_ref, *, dims):
    heads, nk, topk = dims.peer_heads, dims.peer_keys, dims.peer_topk
    half = dims.peer_qdim // 2
    tb = q_ref.shape[0]
    rank = lax.broadcasted_iota(jnp.int32, (topk, tb), 0)

    def top_sorted(s):
        def body(r, carry):
            s, vals = carry
            m = jnp.max(s, axis=0, keepdims=True)
            vals = jnp.where(rank == r, m, vals)
            return jnp.where(s == m, NEG_INF, s), vals
        return lax.fori_loop(0, topk, body, (s, jnp.full((topk, tb), NEG_INF, F32)))[1]

    for h in range(heads):
        scores = []
        for part in range(2):
            qh = q_ref[:, (2 * h + part) * half:(2 * h + part + 1) * half].astype(BF16)
            kh = keys_ref[2 * h + part].astype(BF16)
            scores.append(lax.dot_general(kh, qh, (((1,), (1,)), ((), ())), preferred_element_type=F32))
        s1, s2 = scores
        a = top_sorted(s1)
        b = top_sorted(s2)
        cands = [a[0:1, :] + b]
        for pi in range(1, topk):
            cands.append(a[pi:pi + 1, :] + b[0:SUBLANES, :])
        cand = jnp.concatenate(cands, axis=0)

        def body(r, carry):
            c, _ = carry
            m = jnp.max(c, axis=0, keepdims=True)
            return jnp.where(c == m, NEG_INF, c), m
        _, tau = lax.fori_loop(0, topk, body, (cand, jnp.zeros((1, tb), F32)))
        top = a[0:1, :] + b[0:1, :]
        z = jnp.sum(jnp.where(cand >= tau, jnp.exp(cand - top), 0.0), axis=0, keepdims=True)
        s1_ref[h] = s1
        s2_ref[h] = s2
        e1_ref[h] = jnp.exp(s1 - a[0:1, :]) / z
        e2_ref[h] = jnp.exp(s2 - b[0:1, :])
        tau_ref[h:h + 1, :] = tau


def peer_select(dims, q, subkeys, tb=128):
    t = q.shape[0]
    heads, nk = dims.peer_heads, dims.peer_keys
    half = dims.peer_qdim // 2
    assert dims.peer_topk >= SUBLANES and heads == SUBLANES
    tab = jax.ShapeDtypeStruct((heads, nk, t), F32)
    tab_spec = pl.BlockSpec((heads, nk, tb), lambda i: (0, 0, i))
    return pl.pallas_call(
        functools.partial(_peer_select_kernel, dims=dims),
        grid=(t // tb,),
        in_specs=[pl.BlockSpec((tb, q.shape[1]), lambda i: (i, 0)),
                  pl.BlockSpec((2 * heads, nk, half), lambda i: (0, 0, 0))],
        out_specs=[tab_spec, tab_spec, tab_spec, tab_spec, pl.BlockSpec((heads, tb), lambda i: (0, i))],
        out_shape=[tab, tab, tab, tab, jax.ShapeDtypeStruct((heads, t), F32)],
        compiler_params=_params("parallel"),
        name="peer_select",
    )(q, subkeys.reshape(2 * heads, nk, half))


def _peer_mix_kernel(x_ref, u_ref, v_ref, s1_ref, e1_ref, s2_ref, e2_ref, tau_ref, o_ref, gw_scr, *, dims):
    heads, nk = dims.peer_heads, dims.peer_keys
    eb = pl.program_id(1)
    te = u_ref.shape[0]

    @pl.when(eb == 0)
    def _():
        o_ref[...] = jnp.zeros_like(o_ref)

    ht = lax.dot_general(u_ref[...], x_ref[...], (((1,), (1,)), ((), ())), preferred_element_type=F32)
    for ii in range(te // nk):
        rows = slice(ii * nk, (ii + 1) * nk)
        w = None
        for h in range(heads):
            s = s1_ref[h, ii] + s2_ref[h]
            wh = jnp.where(s >= tau_ref[h:h + 1, :], e1_ref[h, ii] * e2_ref[h], 0.0)
            w = wh if w is None else w + wh
        hh = ht[rows, :]
        gelu = 0.5 * hh * (1.0 + lax.erf(hh * (2.0 ** -0.5)))
        gw_scr[rows, :] = (gelu * w).astype(BF16)
    o_ref[...] += lax.dot_general(gw_scr[...], v_ref[...], (((0,), (0,)), ((), ())),
                                  preferred_element_type=F32)


def peer_mix(dims, xn, u, v, s1, e1, s2, e2, tau, tb=512, te=512):
    t, d = xn.shape
    heads, nk = dims.peer_heads, dims.peer_keys
    tb, te = _pick(t, tb), _pick(dims.n_experts, te)
    assert te % nk == 0
    ni = te // nk
    row_tab = lambda a: a.reshape(heads, nk, 1, t)
    row_spec = pl.BlockSpec((heads, ni, 1, tb), lambda i, e: (0, e, 0, i))
    col_spec = pl.BlockSpec((heads, nk, tb), lambda i, e: (0, 0, i))
    return pl.pallas_call(
        functools.partial(_peer_mix_kernel, dims=dims),
        grid=(t // tb, dims.n_experts // te),
        in_specs=[pl.BlockSpec((tb, d), lambda i, e: (i, 0)),
                  pl.BlockSpec((te, d), lambda i, e: (e, 0)),
                  pl.BlockSpec((te, d), lambda i, e: (e, 0)),
                  row_spec, row_spec, col_spec, col_spec,
                  pl.BlockSpec((heads, tb), lambda i, e: (0, i))],
        out_specs=pl.BlockSpec((tb, d), lambda i, e: (i, 0)),
        out_shape=jax.ShapeDtypeStruct((t, d), F32),
        scratch_shapes=[pltpu.VMEM((te, tb), BF16)],
        compiler_params=_params("parallel", "arbitrary"),
        name="peer_mix",
    )(xn, u, v, row_tab(s1), row_tab(e1), s2, e2, tau)


def _pad_prev(state, width):
    return jnp.pad(state, ((0, 0), (SUBLANES - (width - 1), 0), (0, 0)))


def forward(dims, x_prompt, x_sample, mem_prompt, cache_mem_k, cache_mem_v, state_conv_a,
            state_ssd_conv, state_ssd, norm_mix, norm_mem, norm_ffn, norm_final, w_in,
            a_conv_w, a_out, ssd_conv_w, ssd_conv_b, ssd_dt_bias, ssd_a_log, ssd_d, ssd_norm,
            ssd_out, w_mem_k, w_mem_v, xatt_out, w_o, peer_wq, peer_subkeys, peer_u, peer_v):
    d = dims.d_model
    tp, ts, t = dims.t_prompt, dims.t_sample, dims.tokens
    g, rh, p, n = dims.ssd_groups, dims.heads_per_group, dims.ssd_head_dim, dims.ssd_state
    bf = lambda a: a.astype(BF16)

    x = jnp.concatenate([x_prompt.reshape(tp, d), x_sample.reshape(ts, d)], axis=0)

    w = w_in[0]
    dt0 = 3 * dims.a_width + dims.ssd_inner + dims.ssd_xbc
    w_main = bf(jnp.concatenate([w[:, :dt0], w[:, dt0 + dims.ssd_heads:]], axis=1))
    w_dt = bf(jnp.pad(w[:, dt0:dt0 + dims.ssd_heads], ((0, 0), (0, LANES - dims.ssd_heads))))
    w_kv = bf(jnp.concatenate([w_mem_k[0], w_mem_v[0]], axis=1))

    mn = rmsnorm(mem_prompt.reshape(dims.batch * dims.n_mem, d), norm_mem[0], BF16)
    kv = matmul(mn, w_kv, F32, name="mem_kv")
    xw = dims.xatt_width
    kv3 = kv.reshape(dims.batch, dims.n_mem, 2 * xw)
    mem_k_p = kv3[:, :, :xw].reshape(1, dims.batch, dims.n_mem, dims.xatt_heads, dims.xatt_head_dim)
    mem_v_p = kv3[:, :, xw:].reshape(1, dims.batch, dims.n_mem, dims.xatt_heads, dims.xatt_head_dim)

    xn = rmsnorm(x, norm_mix[0], BF16)
    proj = matmul(xn, w_main, F32, name="in_proj")
    dt_raw = matmul(xn, w_dt, F32, tn=LANES, tm=1024, name="dt_proj")[:, :dims.ssd_heads]
    dt_t = dt_raw.T.reshape(g, rh, t)

    a_cols = (dims.off_ain, dims.off_abg, dims.off_acg)
    zeros_a = jnp.zeros((dims.batch, SUBLANES, dims.a_width), F32)
    va_p, st_a_p = causal_conv(proj, zeros_a, a_conv_w[0], row0=0, nseq=dims.batch, length=dims.seq,
                               cols=dims.a_width, col_offs=a_cols, gated=True, out_dtype=BF16)
    va_s, st_a_s = causal_conv(proj, _pad_prev(state_conv_a[0], dims.a_conv), a_conv_w[0], row0=tp,
                               nseq=dims.dec_batch, length=dims.dec_seq, cols=dims.a_width,
                               col_offs=a_cols, gated=True, out_dtype=BF16, tc=dims.a_width)
    va = jnp.concatenate([va_p, va_s], axis=0)
    na = dims.a_conv - 1
    conv_a_p = st_a_p[None, :, SUBLANES - na:, :]
    conv_a_s = st_a_s[None, :, SUBLANES - na:, :]

    zeros_b = jnp.zeros((dims.batch, SUBLANES, dims.ssd_xbc), F32)
    xbc_p = causal_conv(proj, zeros_b, ssd_conv_w[0], row0=0, nseq=dims.batch, length=dims.seq,
                        cols=dims.ssd_xbc, col_offs=(dims.off_xbc,), gated=False, bias=ssd_conv_b[0])
    xbc_s = causal_conv(proj, _pad_prev(state_ssd_conv[0], dims.ssd_conv), ssd_conv_w[0], row0=tp,
                        nseq=dims.dec_batch, length=dims.dec_seq, cols=dims.ssd_xbc,
                        col_offs=(dims.off_xbc,), gated=False, bias=ssd_conv_b[0], tc=2048)
    nb = dims.ssd_conv - 1
    xcols = slice(dims.off_xbc, dims.off_xbc + dims.ssd_xbc)
    ssd_conv_p = proj[:tp, xcols].reshape(dims.batch, dims.seq, dims.ssd_xbc)[None, :, dims.seq - nb:, :]
    ssd_conv_s = proj[tp:, xcols].reshape(dims.dec_batch, dims.dec_seq, dims.ssd_xbc)[None, :, dims.dec_seq - nb:, :]

    bias_g = ssd_dt_bias[0].reshape(g, rh, 1)
    alog_g = ssd_a_log[0].reshape(g, rh, 1)
    d_exp = jnp.repeat(ssd_d[0], p).reshape(1, dims.ssd_inner)
    norm_w = ssd_norm[0].reshape(1, dims.ssd_inner)
    yb_p, h_p = ssd_block(dims, xbc_p, proj, dt_t, bias_g, alog_g, d_exp, norm_w, row0=0, nrows=tp, chained=True)
    yb_s, h_s = ssd_block(dims, xbc_s, proj, dt_t, bias_g, alog_g, d_exp, norm_w, row0=tp, nrows=ts,
                          chained=False, h0=state_ssd[0])
    yb = jnp.concatenate([yb_p, yb_s], axis=0)

    oc_p = cross_attention(dims, proj, kv3, kv3, row0=0, nseq=dims.batch, length=dims.seq, k_col=0, v_col=1)
    ck = cache_mem_k[0].reshape(dims.dec_batch, dims.n_mem, xw)
    cv = cache_mem_v[0].reshape(dims.dec_batch, dims.n_mem, xw)
    oc_s = cross_attention(dims, proj, ck, cv, row0=tp, nseq=dims.dec_batch, length=dims.dec_seq,
                           k_col=0, v_col=0)
    oc = jnp.concatenate([oc_p, oc_s], axis=0)

    mix = merge_branches(dims, va, yb, oc, bf(a_out[0]), bf(ssd_out[0]), bf(xatt_out[0]), proj)
    x1 = matmul(mix, bf(w_o[0]), F32, residual=x, name="out_proj")

    xn2 = rmsnorm(x1, norm_ffn[0], BF16)
    qp = matmul(xn2, bf(peer_wq[0]), F32, name="peer_query")
    s1, e1, s2, e2, tau = peer_select(dims, qp, peer_subkeys[0])
    ffn = peer_mix(dims, xn2, bf(peer_u[0]), bf(peer_v[0]), s1, e1, s2, e2, tau)
    y = rmsnorm(x1, norm_final, F32, residual=ffn)

    y_prompt = y[:tp].reshape(dims.batch, dims.seq, d)
    y_sample = y[tp:].reshape(dims.dec_batch, dims.dec_seq, d)
    return (y_prompt, y_sample, mem_k_p, mem_v_p, conv_a_p, ssd_conv_p, h_p[None],
            conv_a_s, ssd_conv_s, h_s[None])


def kernel(x_prompt, x_sample, mem_prompt, cache_mem_k, cache_mem_v, state_conv_a, state_ssd_conv, state_ssd, norm_mix, norm_mem, norm_ffn, norm_final, w_in, a_conv_w, a_out, ssd_conv_w, ssd_conv_b, ssd_dt_bias, ssd_a_log, ssd_d, ssd_norm, ssd_out, w_mem_k, w_mem_v, xatt_out, w_o, peer_wq, peer_subkeys, peer_u, peer_v):
    return forward(FULL, x_prompt, x_sample, mem_prompt, cache_mem_k, cache_mem_v, state_conv_a,
                   state_ssd_conv, state_ssd, norm_mix, norm_mem, norm_ffn, norm_final, w_in,
                   a_conv_w, a_out, ssd_conv_w, ssd_conv_b, ssd_dt_bias, ssd_a_log, ssd_d, ssd_norm,
                   ssd_out, w_mem_k, w_mem_v, xatt_out, w_o, peer_wq, peer_subkeys, peer_u, peer_v)
```

```python
import dataclasses
import functools

import jax
import jax.numpy as jnp
import numpy as np
from jax import lax
from jax.experimental import pallas as pl
from jax.experimental.pallas import tpu as pltpu

F32 = jnp.float32
BF16 = jnp.bfloat16
EPS = 1e-6
HIGHEST = lax.Precision.HIGHEST
NEG_INF = float("-inf")

LANES = 128
SUBLANES = 8
VMEM_LIMIT_BYTES = 56 * 1024 * 1024


@dataclasses.dataclass(frozen=True)
class Dims:
    d_model: int = 4096
    batch: int = 4
    seq: int = 2048
    dec_batch: int = 128
    dec_seq: int = 8
    a_width: int = 2048
    a_conv: int = 3
    ssd_inner: int = 4096
    ssd_head_dim: int = 64
    ssd_groups: int = 8
    ssd_state: int = 128
    ssd_conv: int = 4
    ssd_chunk: int = 128
    n_mem: int = 256
    xatt_heads: int = 4
    xatt_head_dim: int = 512
    peer_heads: int = 8
    peer_keys: int = 128
    peer_topk: int = 16
    peer_qdim: int = 256

    @property
    def ssd_heads(self):
        return self.ssd_inner // self.ssd_head_dim

    @property
    def heads_per_group(self):
        return self.ssd_heads // self.ssd_groups

    @property
    def group_width(self):
        return self.ssd_inner // self.ssd_groups

    @property
    def ssd_xbc(self):
        return self.ssd_inner + 2 * self.ssd_groups * self.ssd_state

    @property
    def xatt_width(self):
        return self.xatt_heads * self.xatt_head_dim

    @property
    def n_experts(self):
        return self.peer_keys * self.peer_keys

    @property
    def t_prompt(self):
        return self.batch * self.seq

    @property
    def t_sample(self):
        return self.dec_batch * self.dec_seq

    @property
    def tokens(self):
        return self.t_prompt + self.t_sample

    @property
    def off_ain(self):
        return 0

    @property
    def off_abg(self):
        return self.a_width

    @property
    def off_acg(self):
        return 2 * self.a_width

    @property
    def off_z(self):
        return 3 * self.a_width

    @property
    def off_xbc(self):
        return self.off_z + self.ssd_inner

    @property
    def off_q(self):
        return self.off_xbc + self.ssd_xbc

    @property
    def off_gates(self):
        return self.off_q + self.xatt_width

    @property
    def proj_width(self):
        return self.off_gates + 3 * self.d_model


FULL = Dims()


def _params(*sem):
    return pltpu.CompilerParams(dimension_semantics=sem, vmem_limit_bytes=VMEM_LIMIT_BYTES)


def _pick(n, pref):
    t = min(n, pref)
    while n % t:
        t //= 2
    return t


def _rmsnorm_rows(x, g_ref, o_ref):
    inv = lax.rsqrt(jnp.mean(x * x, axis=-1, keepdims=True) + EPS)
    o_ref[...] = ((x * inv) * g_ref[...]).astype(o_ref.dtype)


def _rmsnorm_kernel(x_ref, g_ref, o_ref):
    _rmsnorm_rows(x_ref[...], g_ref, o_ref)


def _add_rmsnorm_kernel(x_ref, r_ref, g_ref, o_ref):
    _rmsnorm_rows(x_ref[...] + r_ref[...], g_ref, o_ref)


def _rmsnorm_parts_kernel(xp_ref, xs_ref, g_ref, o_ref, *, npb):
    i = pl.program_id(0)

    @pl.when(i < npb)
    def _():
        _rmsnorm_rows(xp_ref[...], g_ref, o_ref)

    @pl.when(i >= npb)
    def _():
        _rmsnorm_rows(xs_ref[...], g_ref, o_ref)


def rmsnorm(x, g, out_dtype, residual=None, row0=0, nrows=None, tm=256):
    m, d = x.shape
    nrows = m - row0 if nrows is None else nrows
    tm = _pick(nrows, tm)
    rb0 = row0 // tm
    assert row0 % tm == 0
    row = pl.BlockSpec((tm, d), lambda i: (rb0 + i, 0))
    gspec = pl.BlockSpec((1, d), lambda i: (0, 0))
    args = (x,) if residual is None else (x, residual)
    return pl.pallas_call(
        _rmsnorm_kernel if residual is None else _add_rmsnorm_kernel,
        grid=(nrows // tm,),
        in_specs=[row] * len(args) + [gspec],
        out_specs=pl.BlockSpec((tm, d), lambda i: (i, 0)),
        out_shape=jax.ShapeDtypeStruct((nrows, d), out_dtype),
        compiler_params=_params("parallel"),
        name="rmsnorm" if residual is None else "add_rmsnorm",
    )(*args, g.reshape(1, d))


def rmsnorm_parts(xp, xs, g, out_dtype, tm=256):
    d = xp.shape[1]
    tm = _pick(xs.shape[0], _pick(xp.shape[0], tm))
    npb, nsb = xp.shape[0] // tm, xs.shape[0] // tm
    return pl.pallas_call(
        functools.partial(_rmsnorm_parts_kernel, npb=npb),
        grid=(npb + nsb,),
        in_specs=[pl.BlockSpec((tm, d), lambda i: (jnp.minimum(i, npb - 1), 0)),
                  pl.BlockSpec((tm, d), lambda i: (jnp.maximum(i - npb, 0), 0)),
                  pl.BlockSpec((1, d), lambda i: (0, 0))],
        out_specs=pl.BlockSpec((tm, d), lambda i: (i, 0)),
        out_shape=jax.ShapeDtypeStruct((xp.shape[0] + xs.shape[0], d), out_dtype),
        compiler_params=_params("arbitrary"),
        name="rmsnorm_parts",
    )(xp, xs, g.reshape(1, d))


def _cast_kernel(x_ref, o_ref):
    o_ref[...] = x_ref[0].astype(o_ref.dtype)


def cast_layer(w, dtype=BF16, tr=512):
    _, r, c = w.shape
    tr = _pick(r, tr)
    return pl.pallas_call(
        _cast_kernel,
        grid=(r // tr,),
        in_specs=[pl.BlockSpec((1, tr, c), lambda i: (0, i, 0))],
        out_specs=pl.BlockSpec((tr, c), lambda i: (i, 0)),
        out_shape=jax.ShapeDtypeStruct((r, c), dtype),
        compiler_params=_params("parallel"),
        name="cast_layer",
    )(w)


def _mm_kernel(a_ref, b_ref, o_ref):
    o_ref[...] = jnp.dot(a_ref[...], b_ref[...], preferred_element_type=F32).astype(o_ref.dtype)


def _mm_res_parts_kernel(a_ref, b_ref, rp_ref, rs_ref, o_ref, *, npb):
    i = pl.program_id(1)
    acc = jnp.dot(a_ref[...], b_ref[...], preferred_element_type=F32)

    @pl.when(i < npb)
    def _():
        o_ref[...] = (rp_ref[...] + acc).astype(o_ref.dtype)

    @pl.when(i >= npb)
    def _():
        o_ref[...] = (rs_ref[...] + acc).astype(o_ref.dtype)


def matmul(a, b, out_dtype, residual_parts=None, tm=512, tn=1024, name="matmul"):
    m, k = a.shape
    n = b.shape[1]
    tm, tn = _pick(m, tm), _pick(n, tn)
    in_specs = [pl.BlockSpec((tm, k), lambda j, i: (i, 0)), pl.BlockSpec((k, tn), lambda j, i: (0, j))]
    args = [a, b]
    body = _mm_kernel
    sems = ("parallel", "parallel")
    if residual_parts is not None:
        rp, rs = residual_parts
        tm = _pick(rs.shape[0], _pick(rp.shape[0], tm))
        in_specs[0] = pl.BlockSpec((tm, k), lambda j, i: (i, 0))
        npb = rp.shape[0] // tm
        in_specs += [pl.BlockSpec((tm, tn), lambda j, i: (jnp.minimum(i, npb - 1), j)),
                     pl.BlockSpec((tm, tn), lambda j, i: (jnp.maximum(i - npb, 0), j))]
        args += [rp, rs]
        body = functools.partial(_mm_res_parts_kernel, npb=npb)
        sems = ("arbitrary", "arbitrary")
    return pl.pallas_call(
        body,
        grid=(n // tn, m // tm),
        in_specs=in_specs,
        out_specs=pl.BlockSpec((tm, tn), lambda j, i: (i, j)),
        out_shape=jax.ShapeDtypeStruct((m, n), out_dtype),
        compiler_params=_params(*sems),
        name=name,
    )(*args)


def _conv_kernel(*refs, width, tl, gated):
    if gated:
        in_ref, bg_ref, cg_ref, prev_ref, w_ref, o_ref, st_ref, scr = refs
    else:
        in_ref, prev_ref, w_ref, bias_ref, o_ref, scr = refs
    lt = pl.program_id(2)

    @pl.when(lt == 0)
    def _():
        scr[0:SUBLANES, :] = prev_ref[0]

    u = cg_ref[...] * in_ref[...] if gated else in_ref[...]
    scr[SUBLANES:SUBLANES + tl, :] = u
    acc = w_ref[width - 1:width, :] * u
    for k in range(width - 1):
        lo = SUBLANES - (width - 1 - k)
        acc = acc + w_ref[k:k + 1, :] * scr[lo:lo + tl, :]
    tail = scr[tl:tl + SUBLANES, :]
    scr[0:SUBLANES, :] = tail
    if gated:
        o_ref[...] = (bg_ref[...] * acc).astype(o_ref.dtype)
        st_ref[0] = tail
    else:
        y = acc + bias_ref[...]
        o_ref[...] = (y * jax.nn.sigmoid(y)).astype(o_ref.dtype)


def causal_conv(proj, prev8, w, *, row0, nseq, length, cols, col_offs, gated, bias=None, out_dtype=F32,
                tl=512, tc=512):
    width = w.shape[0]
    tl, tc = _pick(length, tl), _pick(cols, tc)
    while any(o % tc for o in col_offs):
        tc //= 2
    nl = length // tl
    rb0 = row0 // tl
    assert row0 % tl == 0 and all(o % tc == 0 for o in col_offs)

    def blk(off):
        return pl.BlockSpec((tl, tc), lambda s, c, l, off=off: (rb0 + s * nl + l, off // tc + c))

    prev_spec = pl.BlockSpec((1, SUBLANES, tc), lambda s, c, l: (s, 0, c))
    w_spec = pl.BlockSpec((width, tc), lambda s, c, l: (0, c))
    out_spec = pl.BlockSpec((tl, tc), lambda s, c, l: (s * nl + l, c))
    out_shape = jax.ShapeDtypeStruct((nseq * length, cols), out_dtype)
    if gated:
        in_specs = [blk(col_offs[0]), blk(col_offs[1]), blk(col_offs[2]), prev_spec, w_spec]
        args = (proj, proj, proj, prev8, w)
        out_specs = [out_spec, pl.BlockSpec((1, SUBLANES, tc), lambda s, c, l: (s, 0, c))]
        out_shape = [out_shape, jax.ShapeDtypeStruct((nseq, SUBLANES, cols), F32)]
    else:
        in_specs = [blk(col_offs[0]), prev_spec, w_spec, pl.BlockSpec((1, tc), lambda s, c, l: (0, c))]
        args = (proj, prev8, w, bias.reshape(1, cols))
        out_specs = out_spec
    return pl.pallas_call(
        functools.partial(_conv_kernel, width=width, tl=tl, gated=gated),
        grid=(nseq, cols // tc, nl),
        in_specs=in_specs,
        out_specs=out_specs,
        out_shape=out_shape,
        scratch_shapes=[pltpu.VMEM((tl + SUBLANES, tc), F32)],
        compiler_params=_params("parallel", "parallel", "arbitrary"),
        name="gated_conv" if gated else "ssd_conv",
    )(*args)


def _short_conv_kernel(*refs, width, seq, gated):
    if gated:
        in_ref, bg_ref, cg_ref, prev_ref, w_ref, o_ref, u_ref, scr_u, scr_p = refs
    else:
        in_ref, prev_ref, w_ref, bias_ref, o_ref, scr_u, scr_p = refs
    rows = in_ref.shape[0]
    u = cg_ref[...] * in_ref[...] if gated else in_ref[...]
    zeros = jnp.zeros((SUBLANES, u.shape[1]), F32)
    scr_u[0:SUBLANES, :] = zeros
    scr_u[SUBLANES:SUBLANES + rows, :] = u
    scr_p[0:rows, :] = prev_ref[...]
    scr_p[rows:rows + SUBLANES, :] = zeros
    pos = lax.broadcasted_iota(jnp.int32, u.shape, 0) % seq
    acc = w_ref[width - 1:width, :] * u
    for k in range(width - 1):
        shift = width - 1 - k
        lo = SUBLANES - shift
        operand = jnp.where(pos >= shift, scr_u[lo:lo + rows, :], scr_p[lo:lo + rows, :])
        acc = acc + w_ref[k:k + 1, :] * operand
    if gated:
        o_ref[...] = (bg_ref[...] * acc).astype(o_ref.dtype)
        u_ref[...] = u
    else:
        y = acc + bias_ref[...]
        o_ref[...] = (y * jax.nn.sigmoid(y)).astype(o_ref.dtype)


def short_seq_conv(proj, prev8, w, *, row0, nseq, seq, cols, col_offs, gated, bias=None, out_dtype=F32,
                   rows=128, tc=2048):
    width = w.shape[0]
    assert seq == SUBLANES and width - 1 <= seq
    total = nseq * seq
    rows, tc = _pick(total, rows), _pick(cols, tc)
    while any(o % tc for o in col_offs):
        tc //= 2
    rb0 = row0 // rows
    assert row0 % rows == 0

    def blk(off):
        return pl.BlockSpec((rows, tc), lambda r, c, off=off: (rb0 + r, off // tc + c))

    own = pl.BlockSpec((rows, tc), lambda r, c: (r, c))
    w_spec = pl.BlockSpec((width, tc), lambda r, c: (0, c))
    out_shape = jax.ShapeDtypeStruct((total, cols), out_dtype)
    if gated:
        in_specs = [blk(col_offs[0]), blk(col_offs[1]), blk(col_offs[2]), own, w_spec]
        args = (proj, proj, proj, prev8, w)
        out_specs = [own, own]
        out_shape = [out_shape, jax.ShapeDtypeStruct((total, cols), F32)]
    else:
        in_specs = [blk(col_offs[0]), own, w_spec, pl.BlockSpec((1, tc), lambda r, c: (0, c))]
        args = (proj, prev8, w, bias.reshape(1, cols))
        out_specs = own
    return pl.pallas_call(
        functools.partial(_short_conv_kernel, width=width, seq=seq, gated=gated),
        grid=(total // rows, cols // tc),
        in_specs=in_specs,
        out_specs=out_specs,
        out_shape=out_shape,
        scratch_shapes=[pltpu.VMEM((rows + SUBLANES, tc), F32), pltpu.VMEM((rows + SUBLANES, tc), F32)],
        compiler_params=_params("parallel", "parallel"),
        name="gated_conv_short" if gated else "ssd_conv_short",
    )(*args)


def _softplus(x):
    return jnp.maximum(x, 0.0) + jnp.log1p(jnp.exp(-jnp.abs(x)))


def _ssd_kernel(*refs, nseq, dims, chained):
    if chained:
        (xs_ref, b_ref, c_ref, z_ref, dt_ref, bias_ref, alog_ref, d_ref, nw_ref, sel_ref,
         y_ref, hout_ref, h_scr) = refs
    else:
        (xs_ref, b_ref, c_ref, z_ref, dt_ref, bias_ref, alog_ref, d_ref, nw_ref, sel_ref, h0_ref,
         y_ref, hout_ref) = refs
    q = dims.ssd_chunk
    r_heads = dims.heads_per_group
    p = dims.ssd_head_dim
    gw = dims.group_width
    seg = q // nseq

    if chained:
        @pl.when(pl.program_id(2) == 0)
        def _():
            h_scr[...] = jnp.zeros_like(h_scr)

    row = lax.broadcasted_iota(jnp.int32, (q, q), 0)
    col = lax.broadcasted_iota(jnp.int32, (q, q), 1)
    same = (row // seg) == (col // seg)
    dt = _softplus(dt_ref[0] + bias_ref[0])
    adt = dt * (-jnp.exp(alog_ref[0]))
    cum_mask = jnp.where(same & (row <= col), 1.0, 0.0).astype(F32)
    acum_t = jnp.dot(adt, cum_mask, precision=HIGHEST, preferred_element_type=F32)
    if nseq == 1:
        atot_t = jnp.broadcast_to(acum_t[:, q - 1:q], (r_heads, q))
    else:
        atot_t = jnp.dot(adt, jnp.where(same, 1.0, 0.0).astype(F32), precision=HIGHEST,
                         preferred_element_type=F32)
    stack = jnp.concatenate(
        [acum_t, dt, dt * jnp.exp(atot_t - acum_t), jnp.exp(acum_t),
         jnp.zeros((q - 4 * r_heads, q), F32)], axis=0)
    cols_form = stack.T
    sel = sel_ref[...]
    hi = cols_form.astype(BF16)
    rest = cols_form - hi.astype(F32)
    mid = rest.astype(BF16)
    low = (rest - mid.astype(F32)).astype(BF16)
    expand = (jnp.dot(hi, sel, preferred_element_type=F32) + jnp.dot(mid, sel, preferred_element_type=F32)
              + jnp.dot(low, sel, preferred_element_type=F32))
    e_dt, e_st, e_ac = expand[:, :gw], expand[:, gw:2 * gw], expand[:, 2 * gw:]

    x = xs_ref[...]
    bb = b_ref[...].astype(BF16)
    cb_ = c_ref[...].astype(BF16)
    cb = lax.dot_general(cb_, bb, (((1,), (1,)), ((), ())), preferred_element_type=F32)
    x_dt = x * e_dt
    x_dt_b = x_dt.astype(BF16)
    causal = same & (row >= col)
    lane = lax.broadcasted_iota(jnp.int32, (q, LANES), 1)
    heads_per_tile = LANES // p

    y_tiles = []
    for tile in range(gw // LANES):
        xt = x_dt_b[:, tile * LANES:(tile + 1) * LANES]
        acc = None
        for k in range(heads_per_tile):
            r = tile * heads_per_tile + k
            segm = cols_form[:, r:r + 1] - acum_t[r:r + 1, :]
            decay = jnp.exp(jnp.where(causal, segm, NEG_INF))
            m_r = (cb * decay).astype(BF16)
            x_r = jnp.where((lane >= k * p) & (lane < (k + 1) * p), xt, jnp.zeros_like(xt))
            part = jnp.dot(m_r, x_r, preferred_element_type=F32)
            acc = part if acc is None else acc + part
        y_tiles.append(acc)
    y = jnp.concatenate(y_tiles, axis=1)

    x_st = x * e_st
    eac_t = jnp.exp(atot_t)

    if chained:
        h = h_scr[...]
        y_off = lax.dot_general(cb_, h.astype(BF16), (((1,), (1,)), ((), ())),
                                preferred_element_type=F32)
        y = y + y_off * e_ac
        s_new = lax.dot_general(x_st.astype(BF16), bb, (((0,), (0,)), ((), ())),
                                preferred_element_type=F32)
        scale = jnp.broadcast_to(eac_t[:, q - 1:q], (r_heads, LANES))
        for r in range(r_heads):
            rows = slice(r * p, (r + 1) * p)
            h_scr[rows, :] = h[rows, :] * jnp.broadcast_to(scale[r:r + 1, :], (p, LANES)) + s_new[rows, :]
        for r in range(r_heads):
            hout_ref[0, r] = h_scr[r * p:(r + 1) * p, :]
    else:
        x_st_t = x_st.T
        rows_q = lax.broadcasted_iota(jnp.int32, (q, gw), 0)
        lanes_q = lax.broadcasted_iota(jnp.int32, (gw, q), 1)
        y_off = jnp.zeros((q, gw), F32)
        for s in range(nseq):
            h_s = h0_ref[s].reshape(r_heads * p, dims.ssd_state)
            y_s = lax.dot_general(cb_, h_s.astype(BF16), (((1,), (1,)), ((), ())),
                                  preferred_element_type=F32)
            y_off = jnp.where(rows_q // seg == s, y_s, y_off)
            xs_s = jnp.where(lanes_q // seg == s, x_st_t, 0.0).astype(BF16)
            s_new = jnp.dot(xs_s, bb, preferred_element_type=F32)
            scale = jnp.broadcast_to(eac_t[:, s * seg:s * seg + 1], (r_heads, LANES))
            for r in range(r_heads):
                rows = slice(r * p, (r + 1) * p)
                hout_ref[s, r] = (h_s[rows, :] * jnp.broadcast_to(scale[r:r + 1, :], (p, LANES))
                                  + s_new[rows, :])
        y = y + y_off * e_ac

    y = y + d_ref[...] * x
    z = z_ref[...]
    y = y * (z * jax.nn.sigmoid(z))
    inv = lax.rsqrt(jnp.mean(y * y, axis=-1, keepdims=True) + EPS)
    y_ref[...] = ((y * inv) * nw_ref[...]).astype(y_ref.dtype)


def _ssd_selector(dims):
    r_heads, p, gw = dims.heads_per_group, dims.ssd_head_dim, dims.group_width
    sel = np.zeros((dims.ssd_chunk, 3 * gw), np.float32)
    for part in range(3):
        for r in range(r_heads):
            sel[(part + 1) * r_heads + r, part * gw + r * p:part * gw + (r + 1) * p] = 1.0
    return jnp.asarray(sel, BF16)


def ssd_block(dims, xbc, proj, dt_t, bias_g, alog_g, d_exp, norm_w, *, row0, nrows, chained, h0=None):
    q = dims.ssd_chunk
    g = dims.ssd_groups
    gw = dims.group_width
    n = dims.ssd_state
    r_heads, p = dims.heads_per_group, dims.ssd_head_dim
    assert LANES % p == 0 and q == LANES and 4 * r_heads <= q
    sel = _ssd_selector(dims)
    rb0 = row0 // q
    zoff = dims.off_z // gw
    boff = dims.ssd_inner // n
    coff = boff + g

    if chained:
        nb, nc = dims.batch, dims.seq // q
        grid = (nb, g, nc)
        rowblk = lambda b, gi, c: b * nc + c
        sems = ("parallel", "parallel", "arbitrary")
        nseq = 1
    else:
        nseq = q // dims.dec_seq
        nb = dims.dec_batch // nseq
        grid = (nb, g)
        rowblk = lambda b, gi: b
        sems = ("parallel", "parallel")

    def spec(shape, fn):
        return pl.BlockSpec(shape, fn)

    if chained:
        ix = lambda f: (lambda b, gi, c: f(rowblk(b, gi, c), gi))
    else:
        ix = lambda f: (lambda b, gi: f(rowblk(b, gi), gi))

    in_specs = [
        spec((q, gw), ix(lambda rb, gi: (rb, gi))),
        spec((q, n), ix(lambda rb, gi: (rb, boff + gi))),
        spec((q, n), ix(lambda rb, gi: (rb, coff + gi))),
        spec((q, gw), ix(lambda rb, gi: (rb0 + rb, zoff + gi))),
        spec((1, r_heads, q), ix(lambda rb, gi: (gi, 0, rb0 + rb))),
        spec((1, r_heads, 1), ix(lambda rb, gi: (gi, 0, 0))),
        spec((1, r_heads, 1), ix(lambda rb, gi: (gi, 0, 0))),
        spec((1, gw), ix(lambda rb, gi: (0, gi))),
        spec((1, gw), ix(lambda rb, gi: (0, gi))),
        spec((q, 3 * gw), ix(lambda rb, gi: (0, 0))),
    ]
    args = [xbc, xbc, xbc, proj, dt_t, bias_g, alog_g, d_exp, norm_w, sel]
    y_spec = spec((q, gw), ix(lambda rb, gi: (rb, gi)))
    if chained:
        h_spec = pl.BlockSpec((1, r_heads, p, n), lambda b, gi, c: (b, gi, 0, 0))
        h_shape = jax.ShapeDtypeStruct((nb, dims.ssd_heads, p, n), F32)
        scratch = [pltpu.VMEM((r_heads * p, n), F32)]
    else:
        in_specs.append(pl.BlockSpec((nseq, r_heads, p, n), lambda b, gi: (b, gi, 0, 0)))
        args.append(h0)
        h_spec = pl.BlockSpec((nseq, r_heads, p, n), lambda b, gi: (b, gi, 0, 0))
        h_shape = jax.ShapeDtypeStruct((dims.dec_batch, dims.ssd_heads, p, n), F32)
        scratch = []
    return pl.pallas_call(
        functools.partial(_ssd_kernel, nseq=nseq, dims=dims, chained=chained),
        grid=grid,
        in_specs=in_specs,
        out_specs=[y_spec, h_spec],
        out_shape=[jax.ShapeDtypeStruct((nrows, dims.ssd_inner), BF16), h_shape],
        scratch_shapes=scratch,
        compiler_params=_params(*sems),
        name="ssd_prompt" if chained else "ssd_sample",
    )(*args)


def _xattn_kernel(q_ref, k_ref, v_ref, o_ref, *, heads, head_dim):
    scale = head_dim ** -0.5
    for h in range(heads):
        cs = slice(h * head_dim, (h + 1) * head_dim)
        qh = q_ref[:, cs].astype(BF16)
        kh = k_ref[0, :, cs].astype(BF16)
        vh = v_ref[0, :, cs].astype(BF16)
        s = lax.dot_general(qh, kh, (((1,), (1,)), ((), ())), preferred_element_type=F32) * scale
        s = s - jnp.max(s, axis=-1, keepdims=True)
        e = jnp.exp(s)
        pr = (e / jnp.sum(e, axis=-1, keepdims=True)).astype(BF16)
        o_ref[:, cs] = jnp.dot(pr, vh, preferred_element_type=F32).astype(o_ref.dtype)


def cross_attention(dims, proj, k, v, *, row0, nseq, length, k_col, v_col, tq=512):
    w = dims.xatt_width
    tq = _pick(length, tq)
    nq = length // tq
    rb0 = row0 // tq
    assert row0 % tq == 0 and dims.off_q % w == 0
    qoff = dims.off_q // w
    return pl.pallas_call(
        functools.partial(_xattn_kernel, heads=dims.xatt_heads, head_dim=dims.xatt_head_dim),
        grid=(nseq, nq),
        in_specs=[
            pl.BlockSpec((tq, w), lambda b, i: (rb0 + b * nq + i, qoff)),
            pl.BlockSpec((1, dims.n_mem, w), lambda b, i: (b, 0, k_col)),
            pl.BlockSpec((1, dims.n_mem, w), lambda b, i: (b, 0, v_col)),
        ],
        out_specs=pl.BlockSpec((tq, w), lambda b, i: (b * nq + i, 0)),
        out_shape=jax.ShapeDtypeStruct((nseq * length, w), BF16),
        compiler_params=_params("parallel", "parallel"),
        name="cross_attention",
    )(proj, k, v)


def _merge_kernel(vap_ref, ybp_ref, ocp_ref, vas_ref, ybs_ref, ocs_ref, wa_ref, wb_ref, wc_ref,
                  ga_ref, gb_ref, gc_ref, o_ref, *, npb):
    i = pl.program_id(1)

    def body(va_ref, yb_ref, oc_ref):
        ha = jnp.dot(va_ref[...], wa_ref[...], preferred_element_type=F32)
        hb = jnp.dot(yb_ref[...], wb_ref[...], preferred_element_type=F32)
        hc = jnp.dot(oc_ref[...], wc_ref[...], preferred_element_type=F32)
        mix = (jax.nn.sigmoid(ga_ref[...]) * ha + jax.nn.sigmoid(gb_ref[...]) * hb
               + jax.nn.sigmoid(gc_ref[...]) * hc)
        o_ref[...] = mix.astype(o_ref.dtype)

    @pl.when(i < npb)
    def _():
        body(vap_ref, ybp_ref, ocp_ref)

    @pl.when(i >= npb)
    def _():
        body(vas_ref, ybs_ref, ocs_ref)


def merge_branches(dims, prompt, sample, wa, wb, wc, proj, tm=256, tn=512):
    d = dims.d_model
    mp, ms = prompt[0].shape[0], sample[0].shape[0]
    tm, tn = _pick(ms, _pick(mp, tm)), _pick(d, tn)
    npb = mp // tm
    g0 = dims.off_gates // tn
    gd = d // tn
    assert dims.off_gates % tn == 0

    def lhs_p(a):
        return pl.BlockSpec((tm, a.shape[1]), lambda j, i: (jnp.minimum(i, npb - 1), 0))

    def lhs_s(a):
        return pl.BlockSpec((tm, a.shape[1]), lambda j, i: (jnp.maximum(i - npb, 0), 0))

    def rhs(width):
        return pl.BlockSpec((width, tn), lambda j, i: (0, j))

    def gate(k):
        return pl.BlockSpec((tm, tn), lambda j, i, k=k: (i, g0 + k * gd + j))

    return pl.pallas_call(
        functools.partial(_merge_kernel, npb=npb),
        grid=(d // tn, (mp + ms) // tm),
        in_specs=[lhs_p(a) for a in prompt] + [lhs_s(a) for a in sample]
        + [rhs(wa.shape[0]), rhs(wb.shape[0]), rhs(wc.shape[0]), gate(0), gate(1), gate(2)],
        out_specs=pl.BlockSpec((tm, tn), lambda j, i: (i, j)),
        out_shape=jax.ShapeDtypeStruct((mp + ms, d), BF16),
        compiler_params=_params("arbitrary", "arbitrary"),
        name="merge_branches",
    )(*prompt, *sample, wa, wb, wc, proj, proj, proj)


def _peer_select_kernel(q_ref, keys_ref, s1_ref, e1_ref, s2_ref, e2_ref, tau_ref, work, tops, cand, *, dims):
    heads, nk, topk = dims.peer_heads, dims.peer_keys, dims.peer_topk
    half = dims.peer_qdim // 2
    tb = q_ref.shape[0]
    rank = lax.broadcasted_iota(jnp.int32, (topk, tb), 0)

    for h in range(heads):
        for part, out in ((0, s1_ref), (1, s2_ref)):
            k = 2 * h + part
            qh = q_ref[:, k * half:(k + 1) * half].astype(BF16)
            s = lax.dot_general(keys_ref[k].astype(BF16), qh, (((1,), (1,)), ((), ())),
                                preferred_element_type=F32)
            out[h] = s
            work[k] = s
            tops[k] = jnp.full((topk, tb), NEG_INF, F32)

    def extract(r, carry):
        for k in range(2 * heads):
            s = work[k]
            m = jnp.max(s, axis=0, keepdims=True)
            tops[k] = jnp.where(rank == r, m, tops[k])
            work[k] = jnp.where(s == m, NEG_INF, s)
        return carry
    lax.fori_loop(0, topk, extract, 0)

    for h in range(heads):
        a, b = tops[2 * h], tops[2 * h + 1]
        cand[h, 0:topk, :] = a[0:1, :] + b
        for pi in range(1, topk):
            lo = topk + (pi - 1) * SUBLANES
            cand[h, lo:lo + SUBLANES, :] = a[pi:pi + 1, :] + b[0:SUBLANES, :]

    def threshold(r, taus):
        new = []
        for h in range(heads):
            c = cand[h]
            m = jnp.max(c, axis=0, keepdims=True)
            cand[h] = jnp.where(c == m, NEG_INF, c)
            new.append(m)
        return tuple(new)
    taus = lax.fori_loop(0, topk, threshold, tuple(jnp.zeros((1, tb), F32) for _ in range(heads)))

    for h in range(heads):
        a, b = tops[2 * h], tops[2 * h + 1]
        tau = taus[h]
        z = None
        for pi in range(topk):
            rows = topk if pi == 0 else SUBLANES
            c = a[pi:pi + 1, :] + b[0:rows, :]
            part = jnp.sum(jnp.where(c >= tau, jnp.exp(c - (a[0:1, :] + b[0:1, :])), 0.0), axis=0, keepdims=True)
            z = part if z is None else z + part
        e1_ref[h] = jnp.exp(s1_ref[h] - a[0:1, :]) / z
        e2_ref[h] = jnp.exp(s2_ref[h] - b[0:1, :])
        tau_ref[h:h + 1, :] = tau


def peer_select(dims, q, subkeys, tb=128):
    t = q.shape[0]
    heads, nk = dims.peer_heads, dims.peer_keys
    half = dims.peer_qdim // 2
    assert dims.peer_topk >= SUBLANES and heads == SUBLANES
    tab = jax.ShapeDtypeStruct((heads, nk, t), F32)
    tab_spec = pl.BlockSpec((heads, nk, tb), lambda i: (0, 0, i))
    return pl.pallas_call(
        functools.partial(_peer_select_kernel, dims=dims),
        grid=(t // tb,),
        in_specs=[pl.BlockSpec((tb, q.shape[1]), lambda i: (i, 0)),
                  pl.BlockSpec((2 * heads, nk, half), lambda i: (0, 0, 0))],
        out_specs=[tab_spec, tab_spec, tab_spec, tab_spec, pl.BlockSpec((heads, tb), lambda i: (0, i))],
        out_shape=[tab, tab, tab, tab, jax.ShapeDtypeStruct((heads, t), F32)],
        scratch_shapes=[pltpu.VMEM((2 * heads, nk, tb), F32),
                        pltpu.VMEM((2 * heads, dims.peer_topk, tb), F32),
                        pltpu.VMEM((heads, dims.peer_topk + (dims.peer_topk - 1) * SUBLANES, tb), F32)],
        compiler_params=_params("parallel"),
        name="peer_select",
    )(q, subkeys.reshape(2 * heads, nk, half))


def _peer_mix_kernel(x_ref, u_ref, v_ref, s1_ref, e1_ref, s2_ref, e2_ref, tau_ref, o_ref, gw_scr, w_scr, *,
                     dims, sub):
    heads, nk = dims.peer_heads, dims.peer_keys
    eb = pl.program_id(1)
    te = u_ref.shape[0]

    @pl.when(eb == 0)
    def _():
        o_ref[...] = jnp.zeros_like(o_ref)

    tb = x_ref.shape[0]
    tr, tl = 64, LANES
    for ii in range(te // nk):
        for r0 in range(0, nk, tr):
            for c0 in range(0, tb, tl):
                cs = slice(c0, c0 + tl)
                w = None
                for h in range(heads):
                    s = s1_ref[h, ii, :, cs] + s2_ref[h, r0:r0 + tr, cs]
                    wh = jnp.where(s >= tau_ref[h:h + 1, cs],
                                   e1_ref[h, ii, :, cs] * e2_ref[h, r0:r0 + tr, cs], 0.0)
                    w = wh if w is None else w + wh
                w_scr[ii * nk + r0:ii * nk + r0 + tr, cs] = w
    x = x_ref[...]
    for sb in range(te // sub):
        rows = slice(sb * sub, (sb + 1) * sub)
        hh = lax.dot_general(u_ref[rows, :], x, (((1,), (1,)), ((), ())), preferred_element_type=F32)
        gelu = 0.5 * hh * (1.0 + lax.erf(hh * (2.0 ** -0.5)))
        gw_scr[rows, :] = (gelu * w_scr[rows, :]).astype(BF16)
        o_ref[...] += lax.dot_general(gw_scr[rows, :], v_ref[rows, :], (((0,), (0,)), ((), ())),
                                      preferred_element_type=F32)


def peer_mix(dims, xn, u, v, s1, e1, s2, e2, tau, tb=512, te=512, sub=256):
    t, d = xn.shape
    heads, nk = dims.peer_heads, dims.peer_keys
    tb, te = _pick(t, tb), _pick(dims.n_experts, te)
    sub = min(sub, te)
    assert te % sub == 0 and sub % nk == 0
    ni = te // nk
    row_tab = lambda a: a.reshape(heads, nk, 1, t)
    row_spec = pl.BlockSpec((heads, ni, 1, tb), lambda i, e: (0, e, 0, i))
    col_spec = pl.BlockSpec((heads, nk, tb), lambda i, e: (0, 0, i))
    return pl.pallas_call(
        functools.partial(_peer_mix_kernel, dims=dims, sub=sub),
        grid=(t // tb, dims.n_experts // te),
        in_specs=[pl.BlockSpec((tb, d), lambda i, e: (i, 0)),
                  pl.BlockSpec((te, d), lambda i, e: (e, 0)),
                  pl.BlockSpec((te, d), lambda i, e: (e, 0)),
                  row_spec, row_spec, col_spec, col_spec,
                  pl.BlockSpec((heads, tb), lambda i, e: (0, i))],
        out_specs=pl.BlockSpec((tb, d), lambda i, e: (i, 0)),
        out_shape=jax.ShapeDtypeStruct((t, d), F32),
        scratch_shapes=[pltpu.VMEM((te, tb), BF16), pltpu.VMEM((te, tb), F32)],
        compiler_params=_params("parallel", "arbitrary"),
        name="peer_mix",
    )(xn, u, v, row_tab(s1), row_tab(e1), s2, e2, tau)


def _pad_prev(state, width):
    return jnp.pad(state, ((0, 0), (SUBLANES - (width - 1), 0), (0, 0)))


def forward(dims, x_prompt, x_sample, mem_prompt, cache_mem_k, cache_mem_v, state_conv_a,
            state_ssd_conv, state_ssd, norm_mix, norm_mem, norm_ffn, norm_final, w_in,
            a_conv_w, a_out, ssd_conv_w, ssd_conv_b, ssd_dt_bias, ssd_a_log, ssd_d, ssd_norm,
            ssd_out, w_mem_k, w_mem_v, xatt_out, w_o, peer_wq, peer_subkeys, peer_u, peer_v):
    d = dims.d_model
    tp, ts, t = dims.t_prompt, dims.t_sample, dims.tokens
    g, rh, p, n = dims.ssd_groups, dims.heads_per_group, dims.ssd_head_dim, dims.ssd_state
    bf = lambda a: a.astype(BF16)

    xp2, xs2 = x_prompt.reshape(tp, d), x_sample.reshape(ts, d)

    w = w_in[0]
    dt0 = 3 * dims.a_width + dims.ssd_inner + dims.ssd_xbc
    w_main = bf(jnp.concatenate([w[:, :dt0], w[:, dt0 + dims.ssd_heads:]], axis=1))
    w_dt = bf(jnp.pad(w[:, dt0:dt0 + dims.ssd_heads], ((0, 0), (0, LANES - dims.ssd_heads))))
    w_kv = bf(jnp.concatenate([w_mem_k[0], w_mem_v[0]], axis=1))

    mn = rmsnorm(mem_prompt.reshape(dims.batch * dims.n_mem, d), norm_mem[0], BF16)
    kv = matmul(mn, w_kv, F32, name="mem_kv")
    xw = dims.xatt_width
    kv3 = kv.reshape(dims.batch, dims.n_mem, 2 * xw)
    mem_k_p = kv3[:, :, :xw].reshape(1, dims.batch, dims.n_mem, dims.xatt_heads, dims.xatt_head_dim)
    mem_v_p = kv3[:, :, xw:].reshape(1, dims.batch, dims.n_mem, dims.xatt_heads, dims.xatt_head_dim)

    xn = rmsnorm_parts(xp2, xs2, norm_mix[0], BF16)
    proj = matmul(xn, w_main, F32, name="in_proj")
    dt_raw = matmul(xn, w_dt, F32, tn=LANES, tm=1024, name="dt_proj")[:, :dims.ssd_heads]
    dt_t = dt_raw.T.reshape(g, rh, t)

    a_cols = (dims.off_ain, dims.off_abg, dims.off_acg)
    zeros_a = jnp.zeros((dims.batch, SUBLANES, dims.a_width), F32)
    va_p, st_a_p = causal_conv(proj, zeros_a, a_conv_w[0], row0=0, nseq=dims.batch, length=dims.seq,
                               cols=dims.a_width, col_offs=a_cols, gated=True, out_dtype=BF16)
    prev_a = _pad_prev(state_conv_a[0], dims.a_conv).reshape(dims.dec_batch * SUBLANES, dims.a_width)
    va_s, u_s = short_seq_conv(proj, prev_a, a_conv_w[0], row0=tp, nseq=dims.dec_batch, seq=dims.dec_seq,
                               cols=dims.a_width, col_offs=a_cols, gated=True, out_dtype=BF16)
    na = dims.a_conv - 1
    conv_a_p = st_a_p[None, :, SUBLANES - na:, :]
    conv_a_s = u_s.reshape(dims.dec_batch, dims.dec_seq, dims.a_width)[None, :, dims.dec_seq - na:, :]

    zeros_b = jnp.zeros((dims.batch, SUBLANES, dims.ssd_xbc), F32)
    xbc_p = causal_conv(proj, zeros_b, ssd_conv_w[0], row0=0, nseq=dims.batch, length=dims.seq,
                        cols=dims.ssd_xbc, col_offs=(dims.off_xbc,), gated=False, bias=ssd_conv_b[0])
    prev_b = _pad_prev(state_ssd_conv[0], dims.ssd_conv).reshape(dims.dec_batch * SUBLANES, dims.ssd_xbc)
    xbc_s = short_seq_conv(proj, prev_b, ssd_conv_w[0], row0=tp, nseq=dims.dec_batch, seq=dims.dec_seq,
                           cols=dims.ssd_xbc, col_offs=(dims.off_xbc,), gated=False, bias=ssd_conv_b[0])
    nb = dims.ssd_conv - 1
    x0, x1c = dims.off_xbc, dims.off_xbc + dims.ssd_xbc
    ssd_conv_p = jnp.stack([lax.slice(proj, ((b + 1) * dims.seq - nb, x0), ((b + 1) * dims.seq, x1c))
                            for b in range(dims.batch)])[None]
    ssd_conv_s = lax.slice(proj.reshape(t // dims.dec_seq, dims.dec_seq, dims.proj_width),
                           (tp // dims.dec_seq, dims.dec_seq - nb, x0),
                           (t // dims.dec_seq, dims.dec_seq, x1c))[None]

    bias_g = ssd_dt_bias[0].reshape(g, rh, 1)
    alog_g = ssd_a_log[0].reshape(g, rh, 1)
    d_exp = jnp.repeat(ssd_d[0], p).reshape(1, dims.ssd_inner)
    norm_w = ssd_norm[0].reshape(1, dims.ssd_inner)
    yb_p, h_p = ssd_block(dims, xbc_p, proj, dt_t, bias_g, alog_g, d_exp, norm_w, row0=0, nrows=tp, chained=True)
    yb_s, h_s = ssd_block(dims, xbc_s, proj, dt_t, bias_g, alog_g, d_exp, norm_w, row0=tp, nrows=ts,
                          chained=False, h0=state_ssd[0])

    oc_p = cross_attention(dims, proj, kv3, kv3, row0=0, nseq=dims.batch, length=dims.seq, k_col=0, v_col=1)
    ck = cache_mem_k[0].reshape(dims.dec_batch, dims.n_mem, xw)
    cv = cache_mem_v[0].reshape(dims.dec_batch, dims.n_mem, xw)
    oc_s = cross_attention(dims, proj, ck, cv, row0=tp, nseq=dims.dec_batch, length=dims.dec_seq,
                           k_col=0, v_col=0)

    mix = merge_branches(dims, (va_p, yb_p, oc_p), (va_s, yb_s, oc_s), bf(a_out[0]), bf(ssd_out[0]),
                         bf(xatt_out[0]), proj)
    x1 = matmul(mix, bf(w_o[0]), F32, residual_parts=(xp2, xs2), name="out_proj")

    xn2 = rmsnorm(x1, norm_ffn[0], BF16)
    qp = matmul(xn2, bf(peer_wq[0]), F32, name="peer_query")
    s1, e1, s2, e2, tau = peer_select(dims, qp, peer_subkeys[0])
    ffn = peer_mix(dims, xn2, cast_layer(peer_u), cast_layer(peer_v), s1, e1, s2, e2, tau)
    y_prompt = rmsnorm(x1, norm_final, F32, residual=ffn, row0=0, nrows=tp).reshape(dims.batch, dims.seq, d)
    y_sample = rmsnorm(x1, norm_final, F32, residual=ffn, row0=tp, nrows=ts).reshape(
        dims.dec_batch, dims.dec_seq, d)
    return (y_prompt, y_sample, mem_k_p, mem_v_p, conv_a_p, ssd_conv_p, h_p[None],
            conv_a_s, ssd_conv_s, h_s[None])


def kernel(x_prompt, x_sample, mem_prompt, cache_mem_k, cache_mem_v, state_conv_a, state_ssd_conv, state_ssd, norm_mix, norm_mem, norm_ffn, norm_final, w_in, a_conv_w, a_out, ssd_conv_w, ssd_conv_b, ssd_dt_bias, ssd_a_log, ssd_d, ssd_norm, ssd_out, w_mem_k, w_mem_v, xatt_out, w_o, peer_wq, peer_subkeys, peer_u, peer_v):
    return forward(FULL, x_prompt, x_sample, mem_prompt, cache_mem_k, cache_mem_v, state_conv_a,
                   state_ssd_conv, state_ssd, norm_mix, norm_mem, norm_ffn, norm_final, w_in,
                   a_conv_w, a_out, ssd_conv_w, ssd_conv_b, ssd_dt_bias, ssd_a_log, ssd_d, ssd_norm,
                   ssd_out, w_mem_k, w_mem_v, xatt_out, w_o, peer_wq, peer_subkeys, peer_u, peer_v)
```

```python
import dataclasses
import functools

import jax
import jax.numpy as jnp
import numpy as np
from jax import lax
from jax.experimental import pallas as pl
from jax.experimental.pallas import tpu as pltpu

F32 = jnp.float32
BF16 = jnp.bfloat16
EPS = 1e-6
HIGHEST = lax.Precision.HIGHEST
NEG_INF = float("-inf")

LANES = 128
SUBLANES = 8
VMEM_LIMIT_BYTES = 56 * 1024 * 1024


@dataclasses.dataclass(frozen=True)
class Dims:
    d_model: int = 4096
    batch: int = 4
    seq: int = 2048
    dec_batch: int = 128
    dec_seq: int = 8
    a_width: int = 2048
    a_conv: int = 3
    ssd_inner: int = 4096
    ssd_head_dim: int = 64
    ssd_groups: int = 8
    ssd_state: int = 128
    ssd_conv: int = 4
    ssd_chunk: int = 128
    n_mem: int = 256
    xatt_heads: int = 4
    xatt_head_dim: int = 512
    peer_heads: int = 8
    peer_keys: int = 128
    peer_topk: int = 16
    peer_qdim: int = 256

    @property
    def ssd_heads(self):
        return self.ssd_inner // self.ssd_head_dim

    @property
    def heads_per_group(self):
        return self.ssd_heads // self.ssd_groups

    @property
    def group_width(self):
        return self.ssd_inner // self.ssd_groups

    @property
    def ssd_xbc(self):
        return self.ssd_inner + 2 * self.ssd_groups * self.ssd_state

    @property
    def xatt_width(self):
        return self.xatt_heads * self.xatt_head_dim

    @property
    def n_experts(self):
        return self.peer_keys * self.peer_keys

    @property
    def t_prompt(self):
        return self.batch * self.seq

    @property
    def t_sample(self):
        return self.dec_batch * self.dec_seq

    @property
    def tokens(self):
        return self.t_prompt + self.t_sample

    @property
    def off_ain(self):
        return 0

    @property
    def off_abg(self):
        return self.a_width

    @property
    def off_acg(self):
        return 2 * self.a_width

    @property
    def off_z(self):
        return 3 * self.a_width

    @property
    def off_xbc(self):
        return self.off_z + self.ssd_inner

    @property
    def off_q(self):
        return self.off_xbc + self.ssd_xbc

    @property
    def off_gates(self):
        return self.off_q + self.xatt_width

    @property
    def proj_width(self):
        return self.off_gates + 3 * self.d_model


FULL = Dims()


def _params(*sem):
    return pltpu.CompilerParams(dimension_semantics=sem, vmem_limit_bytes=VMEM_LIMIT_BYTES)


def _pick(n, pref):
    t = min(n, pref)
    while n % t:
        t //= 2
    return t


def _rmsnorm_rows(x, g_ref, o_ref):
    inv = lax.rsqrt(jnp.mean(x * x, axis=-1, keepdims=True) + EPS)
    o_ref[...] = ((x * inv) * g_ref[...]).astype(o_ref.dtype)


def _rmsnorm_kernel(x_ref, g_ref, o_ref):
    _rmsnorm_rows(x_ref[...], g_ref, o_ref)


def _add_rmsnorm_kernel(x_ref, r_ref, g_ref, o_ref):
    _rmsnorm_rows(x_ref[...] + r_ref[...], g_ref, o_ref)


def _rmsnorm_parts_kernel(xp_ref, xs_ref, g_ref, o_ref, *, npb):
    i = pl.program_id(0)

    @pl.when(i < npb)
    def _():
        _rmsnorm_rows(xp_ref[...], g_ref, o_ref)

    @pl.when(i >= npb)
    def _():
        _rmsnorm_rows(xs_ref[...], g_ref, o_ref)


def rmsnorm(x, g, out_dtype, residual=None, row0=0, nrows=None, tm=256):
    m, d = x.shape
    nrows = m - row0 if nrows is None else nrows
    tm = _pick(nrows, tm)
    rb0 = row0 // tm
    assert row0 % tm == 0
    row = pl.BlockSpec((tm, d), lambda i: (rb0 + i, 0))
    gspec = pl.BlockSpec((1, d), lambda i: (0, 0))
    args = (x,) if residual is None else (x, residual)
    return pl.pallas_call(
        _rmsnorm_kernel if residual is None else _add_rmsnorm_kernel,
        grid=(nrows // tm,),
        in_specs=[row] * len(args) + [gspec],
        out_specs=pl.BlockSpec((tm, d), lambda i: (i, 0)),
        out_shape=jax.ShapeDtypeStruct((nrows, d), out_dtype),
        compiler_params=_params("parallel"),
        name="rmsnorm" if residual is None else "add_rmsnorm",
    )(*args, g.reshape(1, d))


def rmsnorm_parts(xp, xs, g, out_dtype, tm=256):
    d = xp.shape[1]
    tm = _pick(xs.shape[0], _pick(xp.shape[0], tm))
    npb, nsb = xp.shape[0] // tm, xs.shape[0] // tm
    return pl.pallas_call(
        functools.partial(_rmsnorm_parts_kernel, npb=npb),
        grid=(npb + nsb,),
        in_specs=[pl.BlockSpec((tm, d), lambda i: (jnp.minimum(i, npb - 1), 0)),
                  pl.BlockSpec((tm, d), lambda i: (jnp.maximum(i - npb, 0), 0)),
                  pl.BlockSpec((1, d), lambda i: (0, 0))],
        out_specs=pl.BlockSpec((tm, d), lambda i: (i, 0)),
        out_shape=jax.ShapeDtypeStruct((xp.shape[0] + xs.shape[0], d), out_dtype),
        compiler_params=_params("arbitrary"),
        name="rmsnorm_parts",
    )(xp, xs, g.reshape(1, d))


def _cast_kernel(x_ref, o_ref):
    o_ref[...] = x_ref[0].astype(o_ref.dtype)


def cast_layer(w, dtype=BF16, tr=512):
    _, r, c = w.shape
    tr = _pick(r, tr)
    return pl.pallas_call(
        _cast_kernel,
        grid=(r // tr,),
        in_specs=[pl.BlockSpec((1, tr, c), lambda i: (0, i, 0))],
        out_specs=pl.BlockSpec((tr, c), lambda i: (i, 0)),
        out_shape=jax.ShapeDtypeStruct((r, c), dtype),
        compiler_params=_params("parallel"),
        name="cast_layer",
    )(w)


def _mm_kernel(*refs, npb, layer_weight):
    a_ref, b_ref = refs[:2]
    refs = refs[2:]
    if npb is not None:
        rp_ref, rs_ref = refs[:2]
        refs = refs[2:]
    o_ref = refs[0]
    i = pl.program_id(1)
    if layer_weight:
        w_scr = refs[1]

        @pl.when(i == 0)
        def _():
            w_scr[...] = b_ref[0].astype(BF16)

        w = w_scr[...]
    else:
        w = b_ref[...]
    acc = jnp.dot(a_ref[...], w, preferred_element_type=F32)
    if npb is None:
        o_ref[...] = acc.astype(o_ref.dtype)
    else:
        @pl.when(i < npb)
        def _():
            o_ref[...] = (rp_ref[...] + acc).astype(o_ref.dtype)

        @pl.when(i >= npb)
        def _():
            o_ref[...] = (rs_ref[...] + acc).astype(o_ref.dtype)


def matmul(a, b, out_dtype, residual_parts=None, tm=512, tn=1024, name="matmul"):
    m, k = a.shape
    layer_weight = b.ndim == 3
    n = b.shape[-1]
    tm, tn = _pick(m, tm), _pick(n, tn)
    npb = None
    extra_specs, extra_args = [], []
    if residual_parts is not None:
        rp, rs = residual_parts
        tm = _pick(rs.shape[0], _pick(rp.shape[0], tm))
        npb = rp.shape[0] // tm
        extra_specs = [pl.BlockSpec((tm, tn), lambda j, i: (jnp.minimum(i, npb - 1), j)),
                       pl.BlockSpec((tm, tn), lambda j, i: (jnp.maximum(i - npb, 0), j))]
        extra_args = [rp, rs]
    if layer_weight:
        b_spec = pl.BlockSpec((1, k, tn), lambda j, i: (0, 0, j))
        scratch = [pltpu.VMEM((k, tn), BF16)]
    else:
        b_spec = pl.BlockSpec((k, tn), lambda j, i: (0, j))
        scratch = []
    sequential = layer_weight or npb is not None
    return pl.pallas_call(
        functools.partial(_mm_kernel, npb=npb, layer_weight=layer_weight),
        grid=(n // tn, m // tm),
        in_specs=[pl.BlockSpec((tm, k), lambda j, i: (i, 0)), b_spec] + extra_specs,
        out_specs=pl.BlockSpec((tm, tn), lambda j, i: (i, j)),
        out_shape=jax.ShapeDtypeStruct((m, n), out_dtype),
        scratch_shapes=scratch,
        compiler_params=_params(*(("arbitrary", "arbitrary") if sequential else ("parallel", "parallel"))),
        name=name,
    )(a, b, *extra_args)


def _mm_wt_kernel(a_ref, wt_ref, o_ref, w_scr):
    @pl.when(pl.program_id(1) == 0)
    def _():
        w_scr[...] = wt_ref[...].astype(BF16)

    o_ref[...] = lax.dot_general(a_ref[...], w_scr[...], (((1,), (1,)), ((), ())),
                                 preferred_element_type=F32).astype(o_ref.dtype)


def matmul_wt(a, wt, segments, out_dtype=F32, tm=1024, tn=512, name="matmul_wt"):
    m, k = a.shape
    tm = _pick(m, tm)
    for start, length in segments:
        tn = _pick(length, tn)
        assert start % SUBLANES == 0
    assert all(length % tn == 0 for _, length in segments)
    starts = np.concatenate([np.arange(s, s + l, tn) for s, l in segments]).astype(np.int32)
    nblk = len(starts)
    bounds = np.cumsum([l // tn for _, l in segments])[:-1]
    shifts = [segments[i + 1][0] - (segments[i][0] + segments[i][1]) for i in range(len(segments) - 1)]

    def row_start(j):
        r = segments[0][0] + j * tn
        for b, sh in zip(bounds, shifts):
            r = r + jnp.where(j >= b, sh, 0)
        return pl.multiple_of(r, SUBLANES)

    return pl.pallas_call(
        _mm_wt_kernel,
        grid=(nblk, m // tm),
        in_specs=[pl.BlockSpec((tm, k), lambda j, i: (i, 0)),
                  pl.BlockSpec((pl.Element(tn), pl.Element(k)), lambda j, i: (row_start(j), 0))],
        out_specs=pl.BlockSpec((tm, tn), lambda j, i: (i, j)),
        out_shape=jax.ShapeDtypeStruct((m, nblk * tn), out_dtype),
        scratch_shapes=[pltpu.VMEM((tn, k), BF16)],
        compiler_params=_params("arbitrary", "arbitrary"),
        name=name,
    )(a, wt)


def _dt_proj_kernel(a_ref, wt_ref, o_ref):
    o_ref[...] = lax.dot_general(wt_ref[...].astype(BF16), a_ref[...], (((1,), (1,)), ((), ())),
                                 preferred_element_type=F32)


def dt_proj_t(a, wt, start, nrows, tm=1024):
    m, k = a.shape
    tm = _pick(m, tm)
    assert start % SUBLANES == 0
    return pl.pallas_call(
        _dt_proj_kernel,
        grid=(m // tm,),
        in_specs=[pl.BlockSpec((tm, k), lambda i: (i, 0)),
                  pl.BlockSpec((pl.Element(nrows), pl.Element(k)), lambda i: (start, 0))],
        out_specs=pl.BlockSpec((nrows, tm), lambda i: (0, i)),
        out_shape=jax.ShapeDtypeStruct((nrows, m), F32),
        compiler_params=_params("parallel"),
        name="dt_proj",
    )(a, wt)


def _conv_kernel(*refs, width, tl, gated):
    if gated:
        in_ref, bg_ref, cg_ref, prev_ref, w_ref, o_ref, st_ref, scr = refs
    else:
        in_ref, prev_ref, w_ref, bias_ref, o_ref, scr = refs
    lt = pl.program_id(2)

    @pl.when(lt == 0)
    def _():
        scr[0:SUBLANES, :] = prev_ref[0]

    u = cg_ref[...] * in_ref[...] if gated else in_ref[...]
    scr[SUBLANES:SUBLANES + tl, :] = u
    acc = w_ref[width - 1:width, :] * u
    for k in range(width - 1):
        lo = SUBLANES - (width - 1 - k)
        acc = acc + w_ref[k:k + 1, :] * scr[lo:lo + tl, :]
    tail = scr[tl:tl + SUBLANES, :]
    scr[0:SUBLANES, :] = tail
    if gated:
        o_ref[...] = (bg_ref[...] * acc).astype(o_ref.dtype)
        st_ref[0] = tail
    else:
        y = acc + bias_ref[...]
        o_ref[...] = (y * jax.nn.sigmoid(y)).astype(o_ref.dtype)


def causal_conv(proj, prev8, w, *, row0, nseq, length, cols, col_offs, gated, bias=None, out_dtype=F32,
                tl=512, tc=512):
    width = w.shape[0]
    tl, tc = _pick(length, tl), _pick(cols, tc)
    while any(o % tc for o in col_offs):
        tc //= 2
    nl = length // tl
    rb0 = row0 // tl
    assert row0 % tl == 0 and all(o % tc == 0 for o in col_offs)

    def blk(off):
        return pl.BlockSpec((tl, tc), lambda s, c, l, off=off: (rb0 + s * nl + l, off // tc + c))

    prev_spec = pl.BlockSpec((1, SUBLANES, tc), lambda s, c, l: (s, 0, c))
    w_spec = pl.BlockSpec((width, tc), lambda s, c, l: (0, c))
    out_spec = pl.BlockSpec((tl, tc), lambda s, c, l: (s * nl + l, c))
    out_shape = jax.ShapeDtypeStruct((nseq * length, cols), out_dtype)
    if gated:
        in_specs = [blk(col_offs[0]), blk(col_offs[1]), blk(col_offs[2]), prev_spec, w_spec]
        args = (proj, proj, proj, prev8, w)
        out_specs = [out_spec, pl.BlockSpec((1, SUBLANES, tc), lambda s, c, l: (s, 0, c))]
        out_shape = [out_shape, jax.ShapeDtypeStruct((nseq, SUBLANES, cols), F32)]
    else:
        in_specs = [blk(col_offs[0]), prev_spec, w_spec, pl.BlockSpec((1, tc), lambda s, c, l: (0, c))]
        args = (proj, prev8, w, bias.reshape(1, cols))
        out_specs = out_spec
    return pl.pallas_call(
        functools.partial(_conv_kernel, width=width, tl=tl, gated=gated),
        grid=(nseq, cols // tc, nl),
        in_specs=in_specs,
        out_specs=out_specs,
        out_shape=out_shape,
        scratch_shapes=[pltpu.VMEM((tl + SUBLANES, tc), F32)],
        compiler_params=_params("parallel", "parallel", "arbitrary"),
        name="gated_conv" if gated else "ssd_conv",
    )(*args)


def _short_conv_kernel(*refs, width, seq, gated):
    if gated:
        in_ref, bg_ref, cg_ref, prev_ref, w_ref, o_ref, u_ref, scr_u, scr_p = refs
    else:
        in_ref, prev_ref, w_ref, bias_ref, o_ref, scr_u, scr_p = refs
    rows = in_ref.shape[0]
    u = cg_ref[...] * in_ref[...] if gated else in_ref[...]
    zeros = jnp.zeros((SUBLANES, u.shape[1]), F32)
    scr_u[0:SUBLANES, :] = zeros
    scr_u[SUBLANES:SUBLANES + rows, :] = u
    scr_p[0:rows, :] = prev_ref[...]
    scr_p[rows:rows + SUBLANES, :] = zeros
    pos = lax.broadcasted_iota(jnp.int32, u.shape, 0) % seq
    acc = w_ref[width - 1:width, :] * u
    for k in range(width - 1):
        shift = width - 1 - k
        lo = SUBLANES - shift
        operand = jnp.where(pos >= shift, scr_u[lo:lo + rows, :], scr_p[lo:lo + rows, :])
        acc = acc + w_ref[k:k + 1, :] * operand
    if gated:
        o_ref[...] = (bg_ref[...] * acc).astype(o_ref.dtype)
        u_ref[...] = u
    else:
        y = acc + bias_ref[...]
        o_ref[...] = (y * jax.nn.sigmoid(y)).astype(o_ref.dtype)


def short_seq_conv(proj, prev8, w, *, row0, nseq, seq, cols, col_offs, gated, bias=None, out_dtype=F32,
                   rows=128, tc=2048):
    width = w.shape[0]
    assert seq == SUBLANES and width - 1 <= seq
    total = nseq * seq
    rows, tc = _pick(total, rows), _pick(cols, tc)
    while any(o % tc for o in col_offs):
        tc //= 2
    rb0 = row0 // rows
    assert row0 % rows == 0

    def blk(off):
        return pl.BlockSpec((rows, tc), lambda r, c, off=off: (rb0 + r, off // tc + c))

    own = pl.BlockSpec((rows, tc), lambda r, c: (r, c))
    w_spec = pl.BlockSpec((width, tc), lambda r, c: (0, c))
    out_shape = jax.ShapeDtypeStruct((total, cols), out_dtype)
    if gated:
        in_specs = [blk(col_offs[0]), blk(col_offs[1]), blk(col_offs[2]), own, w_spec]
        args = (proj, proj, proj, prev8, w)
        out_specs = [own, own]
        out_shape = [out_shape, jax.ShapeDtypeStruct((total, cols), F32)]
    else:
        in_specs = [blk(col_offs[0]), own, w_spec, pl.BlockSpec((1, tc), lambda r, c: (0, c))]
        args = (proj, prev8, w, bias.reshape(1, cols))
        out_specs = own
    return pl.pallas_call(
        functools.partial(_short_conv_kernel, width=width, seq=seq, gated=gated),
        grid=(total // rows, cols // tc),
        in_specs=in_specs,
        out_specs=out_specs,
        out_shape=out_shape,
        scratch_shapes=[pltpu.VMEM((rows + SUBLANES, tc), F32), pltpu.VMEM((rows + SUBLANES, tc), F32)],
        compiler_params=_params("parallel", "parallel"),
        name="gated_conv_short" if gated else "ssd_conv_short",
    )(*args)


def _softplus(x):
    return jnp.maximum(x, 0.0) + jnp.log1p(jnp.exp(-jnp.abs(x)))


def _ssd_kernel(*refs, nseq, dims, chained):
    if chained:
        (xs_ref, b_ref, c_ref, z_ref, dt_ref, bias_ref, alog_ref, d_ref, nw_ref, sel_ref,
         y_ref, hout_ref, h_scr) = refs
    else:
        (xs_ref, b_ref, c_ref, z_ref, dt_ref, bias_ref, alog_ref, d_ref, nw_ref, sel_ref, h0_ref,
         y_ref, hout_ref) = refs
    q = dims.ssd_chunk
    r_heads = dims.heads_per_group
    p = dims.ssd_head_dim
    gw = dims.group_width
    seg = q // nseq

    if chained:
        @pl.when(pl.program_id(2) == 0)
        def _():
            h_scr[...] = jnp.zeros_like(h_scr)

    row = lax.broadcasted_iota(jnp.int32, (q, q), 0)
    col = lax.broadcasted_iota(jnp.int32, (q, q), 1)
    same = (row // seg) == (col // seg)
    dt = _softplus(dt_ref[0] + bias_ref[0])
    adt = dt * (-jnp.exp(alog_ref[0]))
    cum_mask = jnp.where(same & (row <= col), 1.0, 0.0).astype(F32)
    acum_t = jnp.dot(adt, cum_mask, precision=HIGHEST, preferred_element_type=F32)
    if nseq == 1:
        atot_t = jnp.broadcast_to(acum_t[:, q - 1:q], (r_heads, q))
    else:
        atot_t = jnp.dot(adt, jnp.where(same, 1.0, 0.0).astype(F32), precision=HIGHEST,
                         preferred_element_type=F32)
    stack = jnp.concatenate(
        [acum_t, dt, dt * jnp.exp(atot_t - acum_t), jnp.exp(acum_t),
         jnp.zeros((q - 4 * r_heads, q), F32)], axis=0)
    cols_form = stack.T
    sel = sel_ref[...]
    hi = cols_form.astype(BF16)
    rest = cols_form - hi.astype(F32)
    mid = rest.astype(BF16)
    low = (rest - mid.astype(F32)).astype(BF16)
    expand = (jnp.dot(hi, sel, preferred_element_type=F32) + jnp.dot(mid, sel, preferred_element_type=F32)
              + jnp.dot(low, sel, preferred_element_type=F32))
    e_dt, e_st, e_ac = expand[:, :gw], expand[:, gw:2 * gw], expand[:, 2 * gw:]

    x = xs_ref[...]
    bb = b_ref[...].astype(BF16)
    cb_ = c_ref[...].astype(BF16)
    cb = lax.dot_general(cb_, bb, (((1,), (1,)), ((), ())), preferred_element_type=F32)
    x_dt = x * e_dt
    x_dt_b = x_dt.astype(BF16)
    causal = same & (row >= col)
    lane = lax.broadcasted_iota(jnp.int32, (q, LANES), 1)
    heads_per_tile = LANES // p

    y_tiles = []
    for tile in range(gw // LANES):
        xt = x_dt_b[:, tile * LANES:(tile + 1) * LANES]
        acc = None
        for k in range(heads_per_tile):
            r = tile * heads_per_tile + k
            segm = cols_form[:, r:r + 1] - acum_t[r:r + 1, :]
            decay = jnp.exp(jnp.where(causal, segm, NEG_INF))
            m_r = (cb * decay).astype(BF16)
            x_r = jnp.where((lane >= k * p) & (lane < (k + 1) * p), xt, jnp.zeros_like(xt))
            part = jnp.dot(m_r, x_r, preferred_element_type=F32)
            acc = part if acc is None else acc + part
        y_tiles.append(acc)
    y = jnp.concatenate(y_tiles, axis=1)

    x_st = x * e_st
    eac_t = jnp.exp(atot_t)

    if chained:
        h = h_scr[...]
        y_off = lax.dot_general(cb_, h.astype(BF16), (((1,), (1,)), ((), ())),
                                preferred_element_type=F32)
        y = y + y_off * e_ac
        s_new = lax.dot_general(x_st.astype(BF16), bb, (((0,), (0,)), ((), ())),
                                preferred_element_type=F32)
        scale = jnp.broadcast_to(eac_t[:, q - 1:q], (r_heads, LANES))
        for r in range(r_heads):
            rows = slice(r * p, (r + 1) * p)
            h_scr[rows, :] = h[rows, :] * jnp.broadcast_to(scale[r:r + 1, :], (p, LANES)) + s_new[rows, :]
        for r in range(r_heads):
            hout_ref[0, r] = h_scr[r * p:(r + 1) * p, :]
    else:
        x_st_t = x_st.T
        rows_q = lax.broadcasted_iota(jnp.int32, (q, gw), 0)
        lanes_q = lax.broadcasted_iota(jnp.int32, (gw, q), 1)
        y_off = jnp.zeros((q, gw), F32)
        for s in range(nseq):
            h_s = h0_ref[s].reshape(r_heads * p, dims.ssd_state)
            y_s = lax.dot_general(cb_, h_s.astype(BF16), (((1,), (1,)), ((), ())),
                                  preferred_element_type=F32)
            y_off = jnp.where(rows_q // seg == s, y_s, y_off)
            xs_s = jnp.where(lanes_q // seg == s, x_st_t, 0.0).astype(BF16)
            s_new = jnp.dot(xs_s, bb, preferred_element_type=F32)
            scale = jnp.broadcast_to(eac_t[:, s * seg:s * seg + 1], (r_heads, LANES))
            for r in range(r_heads):
                rows = slice(r * p, (r + 1) * p)
                hout_ref[s, r] = (h_s[rows, :] * jnp.broadcast_to(scale[r:r + 1, :], (p, LANES))
                                  + s_new[rows, :])
        y = y + y_off * e_ac

    y = y + d_ref[...] * x
    z = z_ref[...]
    y = y * (z * jax.nn.sigmoid(z))
    inv = lax.rsqrt(jnp.mean(y * y, axis=-1, keepdims=True) + EPS)
    y_ref[...] = ((y * inv) * nw_ref[...]).astype(y_ref.dtype)


def _ssd_selector(dims):
    r_heads, p, gw = dims.heads_per_group, dims.ssd_head_dim, dims.group_width
    sel = np.zeros((dims.ssd_chunk, 3 * gw), np.float32)
    for part in range(3):
        for r in range(r_heads):
            sel[(part + 1) * r_heads + r, part * gw + r * p:part * gw + (r + 1) * p] = 1.0
    return jnp.asarray(sel, BF16)


def ssd_block(dims, xbc, proj, dt_t, bias_g, alog_g, d_exp, norm_w, *, row0, nrows, chained, h0=None):
    q = dims.ssd_chunk
    g = dims.ssd_groups
    gw = dims.group_width
    n = dims.ssd_state
    r_heads, p = dims.heads_per_group, dims.ssd_head_dim
    assert LANES % p == 0 and q == LANES and 4 * r_heads <= q
    sel = _ssd_selector(dims)
    rb0 = row0 // q
    zoff = dims.off_z // gw
    boff = dims.ssd_inner // n
    coff = boff + g

    if chained:
        nb, nc = dims.batch, dims.seq // q
        grid = (nb, g, nc)
        rowblk = lambda b, gi, c: b * nc + c
        sems = ("parallel", "parallel", "arbitrary")
        nseq = 1
    else:
        nseq = q // dims.dec_seq
        nb = dims.dec_batch // nseq
        grid = (nb, g)
        rowblk = lambda b, gi: b
        sems = ("parallel", "parallel")

    def spec(shape, fn):
        return pl.BlockSpec(shape, fn)

    if chained:
        ix = lambda f: (lambda b, gi, c: f(rowblk(b, gi, c), gi))
    else:
        ix = lambda f: (lambda b, gi: f(rowblk(b, gi), gi))

    in_specs = [
        spec((q, gw), ix(lambda rb, gi: (rb, gi))),
        spec((q, n), ix(lambda rb, gi: (rb, boff + gi))),
        spec((q, n), ix(lambda rb, gi: (rb, coff + gi))),
        spec((q, gw), ix(lambda rb, gi: (rb0 + rb, zoff + gi))),
        spec((1, r_heads, q), ix(lambda rb, gi: (gi, 0, rb0 + rb))),
        spec((1, r_heads, 1), ix(lambda rb, gi: (gi, 0, 0))),
        spec((1, r_heads, 1), ix(lambda rb, gi: (gi, 0, 0))),
        spec((1, gw), ix(lambda rb, gi: (0, gi))),
        spec((1, gw), ix(lambda rb, gi: (0, gi))),
        spec((q, 3 * gw), ix(lambda rb, gi: (0, 0))),
    ]
    args = [xbc, xbc, xbc, proj, dt_t, bias_g, alog_g, d_exp, norm_w, sel]
    y_spec = spec((q, gw), ix(lambda rb, gi: (rb, gi)))
    if chained:
        h_spec = pl.BlockSpec((1, r_heads, p, n), lambda b, gi, c: (b, gi, 0, 0))
        h_shape = jax.ShapeDtypeStruct((nb, dims.ssd_heads, p, n), F32)
        scratch = [pltpu.VMEM((r_heads * p, n), F32)]
    else:
        in_specs.append(pl.BlockSpec((nseq, r_heads, p, n), lambda b, gi: (b, gi, 0, 0)))
        args.append(h0)
        h_spec = pl.BlockSpec((nseq, r_heads, p, n), lambda b, gi: (b, gi, 0, 0))
        h_shape = jax.ShapeDtypeStruct((dims.dec_batch, dims.ssd_heads, p, n), F32)
        scratch = []
    return pl.pallas_call(
        functools.partial(_ssd_kernel, nseq=nseq, dims=dims, chained=chained),
        grid=grid,
        in_specs=in_specs,
        out_specs=[y_spec, h_spec],
        out_shape=[jax.ShapeDtypeStruct((nrows, dims.ssd_inner), BF16), h_shape],
        scratch_shapes=scratch,
        compiler_params=_params(*sems),
        name="ssd_prompt" if chained else "ssd_sample",
    )(*args)


def _xattn_kernel(q_ref, k_ref, v_ref, o_ref, *, heads, head_dim, per_head_layout):
    scale = head_dim ** -0.5
    for h in range(heads):
        cs = slice(h * head_dim, (h + 1) * head_dim)
        qh = q_ref[:, cs].astype(BF16)
        if per_head_layout:
            kh = k_ref[0, :, h, :].astype(BF16)
            vh = v_ref[0, :, h, :].astype(BF16)
        else:
            kh = k_ref[0, :, cs].astype(BF16)
            vh = v_ref[0, :, cs].astype(BF16)
        s = lax.dot_general(qh, kh, (((1,), (1,)), ((), ())), preferred_element_type=F32) * scale
        s = s - jnp.max(s, axis=-1, keepdims=True)
        e = jnp.exp(s)
        pr = (e / jnp.sum(e, axis=-1, keepdims=True)).astype(BF16)
        o_ref[:, cs] = jnp.dot(pr, vh, preferred_element_type=F32).astype(o_ref.dtype)


def cross_attention(dims, proj, k, v, *, row0, nseq, length, tq=512):
    w = dims.xatt_width
    tq = _pick(length, tq)
    nq = length // tq
    rb0 = row0 // tq
    assert row0 % tq == 0 and dims.off_q % w == 0
    qoff = dims.off_q // w
    per_head = k.ndim == 4
    if per_head:
        kv_spec = pl.BlockSpec((1, dims.n_mem, dims.xatt_heads, dims.xatt_head_dim), lambda b, i: (b, 0, 0, 0))
    else:
        kv_spec = pl.BlockSpec((1, dims.n_mem, w), lambda b, i: (b, 0, 0))
    return pl.pallas_call(
        functools.partial(_xattn_kernel, heads=dims.xatt_heads, head_dim=dims.xatt_head_dim,
                          per_head_layout=per_head),
        grid=(nseq, nq),
        in_specs=[pl.BlockSpec((tq, w), lambda b, i: (rb0 + b * nq + i, qoff)), kv_spec, kv_spec],
        out_specs=pl.BlockSpec((tq, w), lambda b, i: (b * nq + i, 0)),
        out_shape=jax.ShapeDtypeStruct((nseq * length, w), BF16),
        compiler_params=_params("parallel", "parallel"),
        name="cross_attention",
    )(proj, k, v)


def _merge_kernel(vap_ref, ybp_ref, ocp_ref, vas_ref, ybs_ref, ocs_ref, wa_ref, wb_ref, wc_ref,
                  ga_ref, gb_ref, gc_ref, o_ref, *, npb):
    i = pl.program_id(1)

    def body(va_ref, yb_ref, oc_ref):
        ha = jnp.dot(va_ref[...], wa_ref[...], preferred_element_type=F32)
        hb = jnp.dot(yb_ref[...], wb_ref[...], preferred_element_type=F32)
        hc = jnp.dot(oc_ref[...], wc_ref[...], preferred_element_type=F32)
        mix = (jax.nn.sigmoid(ga_ref[...]) * ha + jax.nn.sigmoid(gb_ref[...]) * hb
               + jax.nn.sigmoid(gc_ref[...]) * hc)
        o_ref[...] = mix.astype(o_ref.dtype)

    @pl.when(i < npb)
    def _():
        body(vap_ref, ybp_ref, ocp_ref)

    @pl.when(i >= npb)
    def _():
        body(vas_ref, ybs_ref, ocs_ref)


def merge_branches(dims, prompt, sample, wa, wb, wc, proj, tm=256, tn=512):
    d = dims.d_model
    mp, ms = prompt[0].shape[0], sample[0].shape[0]
    tm, tn = _pick(ms, _pick(mp, tm)), _pick(d, tn)
    npb = mp // tm
    g0 = dims.off_gates // tn
    gd = d // tn
    assert dims.off_gates % tn == 0

    def lhs_p(a):
        return pl.BlockSpec((tm, a.shape[1]), lambda j, i: (jnp.minimum(i, npb - 1), 0))

    def lhs_s(a):
        return pl.BlockSpec((tm, a.shape[1]), lambda j, i: (jnp.maximum(i - npb, 0), 0))

    def rhs(width):
        return pl.BlockSpec((width, tn), lambda j, i: (0, j))

    def gate(k):
        return pl.BlockSpec((tm, tn), lambda j, i, k=k: (i, g0 + k * gd + j))

    return pl.pallas_call(
        functools.partial(_merge_kernel, npb=npb),
        grid=(d // tn, (mp + ms) // tm),
        in_specs=[lhs_p(a) for a in prompt] + [lhs_s(a) for a in sample]
        + [rhs(wa.shape[0]), rhs(wb.shape[0]), rhs(wc.shape[0]), gate(0), gate(1), gate(2)],
        out_specs=pl.BlockSpec((tm, tn), lambda j, i: (i, j)),
        out_shape=jax.ShapeDtypeStruct((mp + ms, d), BF16),
        compiler_params=_params("arbitrary", "arbitrary"),
        name="merge_branches",
    )(*prompt, *sample, wa, wb, wc, proj, proj, proj)


def _peer_select_kernel(q_ref, keys_ref, s1_ref, e1_ref, s2_ref, e2_ref, tau_ref, work, tops, cand, *, dims):
    heads, nk, topk = dims.peer_heads, dims.peer_keys, dims.peer_topk
    half = dims.peer_qdim // 2
    tb = q_ref.shape[0]
    rank = lax.broadcasted_iota(jnp.int32, (topk, tb), 0)

    for h in range(heads):
        for part, out in ((0, s1_ref), (1, s2_ref)):
            k = 2 * h + part
            qh = q_ref[:, k * half:(k + 1) * half].astype(BF16)
            s = lax.dot_general(keys_ref[k].astype(BF16), qh, (((1,), (1,)), ((), ())),
                                preferred_element_type=F32)
            out[h] = s
            work[k] = s
            tops[k] = jnp.full((topk, tb), NEG_INF, F32)

    def extract(r, carry):
        for k in range(2 * heads):
            s = work[k]
            m = jnp.max(s, axis=0, keepdims=True)
            tops[k] = jnp.where(rank == r, m, tops[k])
            work[k] = jnp.where(s == m, NEG_INF, s)
        return carry
    lax.fori_loop(0, topk, extract, 0)

    for h in range(heads):
        a, b = tops[2 * h], tops[2 * h + 1]
        cand[h, 0:topk, :] = a[0:1, :] + b
        for pi in range(1, topk):
            lo = topk + (pi - 1) * SUBLANES
            cand[h, lo:lo + SUBLANES, :] = a[pi:pi + 1, :] + b[0:SUBLANES, :]

    def threshold(r, taus):
        new = []
        for h in range(heads):
            c = cand[h]
            m = jnp.max(c, axis=0, keepdims=True)
            cand[h] = jnp.where(c == m, NEG_INF, c)
            new.append(m)
        return tuple(new)
    taus = lax.fori_loop(0, topk, threshold, tuple(jnp.zeros((1, tb), F32) for _ in range(heads)))

    for h in range(heads):
        a, b = tops[2 * h], tops[2 * h + 1]
        tau = taus[h]
        z = None
        for pi in range(topk):
            rows = topk if pi == 0 else SUBLANES
            c = a[pi:pi + 1, :] + b[0:rows, :]
            part = jnp.sum(jnp.where(c >= tau, jnp.exp(c - (a[0:1, :] + b[0:1, :])), 0.0), axis=0, keepdims=True)
            z = part if z is None else z + part
        e1_ref[h] = jnp.exp(s1_ref[h] - a[0:1, :]) / z
        e2_ref[h] = jnp.exp(s2_ref[h] - b[0:1, :])
        tau_ref[h:h + 1, :] = tau


def peer_select(dims, q, subkeys, tb=128):
    t = q.shape[0]
    heads, nk = dims.peer_heads, dims.peer_keys
    half = dims.peer_qdim // 2
    assert dims.peer_topk >= SUBLANES and heads == SUBLANES
    tab = jax.ShapeDtypeStruct((heads, nk, t), F32)
    tab_spec = pl.BlockSpec((heads, nk, tb), lambda i: (0, 0, i))
    return pl.pallas_call(
        functools.partial(_peer_select_kernel, dims=dims),
        grid=(t // tb,),
        in_specs=[pl.BlockSpec((tb, q.shape[1]), lambda i: (i, 0)),
                  pl.BlockSpec((2 * heads, nk, half), lambda i: (0, 0, 0))],
        out_specs=[tab_spec, tab_spec, tab_spec, tab_spec, pl.BlockSpec((heads, tb), lambda i: (0, i))],
        out_shape=[tab, tab, tab, tab, jax.ShapeDtypeStruct((heads, t), F32)],
        scratch_shapes=[pltpu.VMEM((2 * heads, nk, tb), F32),
                        pltpu.VMEM((2 * heads, dims.peer_topk, tb), F32),
                        pltpu.VMEM((heads, dims.peer_topk + (dims.peer_topk - 1) * SUBLANES, tb), F32)],
        compiler_params=_params("parallel"),
        name="peer_select",
    )(q, subkeys.reshape(2 * heads, nk, half))


def _peer_mix_kernel(x_ref, u_ref, v_ref, s1_ref, e1_ref, s2_ref, e2_ref, tau_ref, o_ref, gw_scr, w_scr, *,
                     dims, sub):
    heads, nk = dims.peer_heads, dims.peer_keys
    eb = pl.program_id(1)
    te = u_ref.shape[0]

    @pl.when(eb == 0)
    def _():
        o_ref[...] = jnp.zeros_like(o_ref)

    tb = x_ref.shape[0]
    tr, tl = 64, LANES
    for ii in range(te // nk):
        for r0 in range(0, nk, tr):
            for c0 in range(0, tb, tl):
                cs = slice(c0, c0 + tl)
                w = None
                for h in range(heads):
                    s = s1_ref[h, ii, :, cs] + s2_ref[h, r0:r0 + tr, cs]
                    wh = jnp.where(s >= tau_ref[h:h + 1, cs],
                                   e1_ref[h, ii, :, cs] * e2_ref[h, r0:r0 + tr, cs], 0.0)
                    w = wh if w is None else w + wh
                w_scr[ii * nk + r0:ii * nk + r0 + tr, cs] = w
    x = x_ref[...]
    for sb in range(te // sub):
        rows = slice(sb * sub, (sb + 1) * sub)
        hh = lax.dot_general(u_ref[rows, :], x, (((1,), (1,)), ((), ())), preferred_element_type=F32)
        gelu = 0.5 * hh * (1.0 + lax.erf(hh * (2.0 ** -0.5)))
        gw_scr[rows, :] = (gelu * w_scr[rows, :]).astype(BF16)
        o_ref[...] += lax.dot_general(gw_scr[rows, :], v_ref[rows, :], (((0,), (0,)), ((), ())),
                                      preferred_element_type=F32)


def peer_mix(dims, xn, u, v, s1, e1, s2, e2, tau, tb=512, te=512, sub=256):
    t, d = xn.shape
    heads, nk = dims.peer_heads, dims.peer_keys
    tb, te = _pick(t, tb), _pick(dims.n_experts, te)
    sub = min(sub, te)
    assert te % sub == 0 and sub % nk == 0
    ni = te // nk
    row_tab = lambda a: a.reshape(heads, nk, 1, t)
    row_spec = pl.BlockSpec((heads, ni, 1, tb), lambda i, e: (0, e, 0, i))
    col_spec = pl.BlockSpec((heads, nk, tb), lambda i, e: (0, 0, i))
    return pl.pallas_call(
        functools.partial(_peer_mix_kernel, dims=dims, sub=sub),
        grid=(t // tb, dims.n_experts // te),
        in_specs=[pl.BlockSpec((tb, d), lambda i, e: (i, 0)),
                  pl.BlockSpec((te, d), lambda i, e: (e, 0)),
                  pl.BlockSpec((te, d), lambda i, e: (e, 0)),
                  row_spec, row_spec, col_spec, col_spec,
                  pl.BlockSpec((heads, tb), lambda i, e: (0, i))],
        out_specs=pl.BlockSpec((tb, d), lambda i, e: (i, 0)),
        out_shape=jax.ShapeDtypeStruct((t, d), F32),
        scratch_shapes=[pltpu.VMEM((te, tb), BF16), pltpu.VMEM((te, tb), F32)],
        compiler_params=_params("parallel", "arbitrary"),
        name="peer_mix",
    )(xn, u, v, row_tab(s1), row_tab(e1), s2, e2, tau)


def _pad_prev(state, width):
    return jnp.pad(state, ((0, 0), (SUBLANES - (width - 1), 0), (0, 0)))


def forward(dims, x_prompt, x_sample, mem_prompt, cache_mem_k, cache_mem_v, state_conv_a,
            state_ssd_conv, state_ssd, norm_mix, norm_mem, norm_ffn, norm_final, w_in,
            a_conv_w, a_out, ssd_conv_w, ssd_conv_b, ssd_dt_bias, ssd_a_log, ssd_d, ssd_norm,
            ssd_out, w_mem_k, w_mem_v, xatt_out, w_o, peer_wq, peer_subkeys, peer_u, peer_v):
    d = dims.d_model
    tp, ts, t = dims.t_prompt, dims.t_sample, dims.tokens
    g, rh, p, n = dims.ssd_groups, dims.heads_per_group, dims.ssd_head_dim, dims.ssd_state
    bf = lambda a: a.astype(BF16)

    xp2, xs2 = x_prompt.reshape(tp, d), x_sample.reshape(ts, d)

    mn = rmsnorm(mem_prompt.reshape(dims.batch * dims.n_mem, d), norm_mem[0], BF16)
    xw = dims.xatt_width
    k_p = matmul(mn, w_mem_k, F32, tn=512, name="mem_k").reshape(dims.batch, dims.n_mem, xw)
    v_p = matmul(mn, w_mem_v, F32, tn=512, name="mem_v").reshape(dims.batch, dims.n_mem, xw)
    mem_k_p = k_p.reshape(1, dims.batch, dims.n_mem, dims.xatt_heads, dims.xatt_head_dim)
    mem_v_p = v_p.reshape(1, dims.batch, dims.n_mem, dims.xatt_heads, dims.xatt_head_dim)

    xn = rmsnorm_parts(xp2, xs2, norm_mix[0], BF16)
    wt = jnp.swapaxes(w_in, 1, 2)[0]
    dt0 = 3 * dims.a_width + dims.ssd_inner + dims.ssd_xbc
    proj = matmul_wt(xn, wt, [(0, dt0), (dt0 + dims.ssd_heads, dims.proj_width - dt0)], name="in_proj")
    dt_t = dt_proj_t(xn, wt, dt0, dims.ssd_heads).reshape(g, rh, t)

    a_cols = (dims.off_ain, dims.off_abg, dims.off_acg)
    zeros_a = jnp.zeros((dims.batch, SUBLANES, dims.a_width), F32)
    va_p, st_a_p = causal_conv(proj, zeros_a, a_conv_w[0], row0=0, nseq=dims.batch, length=dims.seq,
                               cols=dims.a_width, col_offs=a_cols, gated=True, out_dtype=BF16)
    prev_a = _pad_prev(state_conv_a[0], dims.a_conv).reshape(dims.dec_batch * SUBLANES, dims.a_width)
    va_s, u_s = short_seq_conv(proj, prev_a, a_conv_w[0], row0=tp, nseq=dims.dec_batch, seq=dims.dec_seq,
                               cols=dims.a_width, col_offs=a_cols, gated=True, out_dtype=BF16)
    na = dims.a_conv - 1
    conv_a_p = st_a_p[None, :, SUBLANES - na:, :]
    conv_a_s = u_s.reshape(dims.dec_batch, dims.dec_seq, dims.a_width)[None, :, dims.dec_seq - na:, :]

    zeros_b = jnp.zeros((dims.batch, SUBLANES, dims.ssd_xbc), F32)
    xbc_p = causal_conv(proj, zeros_b, ssd_conv_w[0], row0=0, nseq=dims.batch, length=dims.seq,
                        cols=dims.ssd_xbc, col_offs=(dims.off_xbc,), gated=False, bias=ssd_conv_b[0])
    prev_b = _pad_prev(state_ssd_conv[0], dims.ssd_conv).reshape(dims.dec_batch * SUBLANES, dims.ssd_xbc)
    xbc_s = short_seq_conv(proj, prev_b, ssd_conv_w[0], row0=tp, nseq=dims.dec_batch, seq=dims.dec_seq,
                           cols=dims.ssd_xbc, col_offs=(dims.off_xbc,), gated=False, bias=ssd_conv_b[0])
    nb = dims.ssd_conv - 1
    x0, x1c = dims.off_xbc, dims.off_xbc + dims.ssd_xbc
    ssd_conv_p = jnp.stack([lax.slice(proj, ((b + 1) * dims.seq - nb, x0), ((b + 1) * dims.seq, x1c))
                            for b in range(dims.batch)])[None]
    ssd_conv_s = lax.slice(proj.reshape(t // dims.dec_seq, dims.dec_seq, dims.proj_width),
                           (tp // dims.dec_seq, dims.dec_seq - nb, x0),
                           (t // dims.dec_seq, dims.dec_seq, x1c))[None]

    bias_g = ssd_dt_bias[0].reshape(g, rh, 1)
    alog_g = ssd_a_log[0].reshape(g, rh, 1)
    d_exp = jnp.repeat(ssd_d[0], p).reshape(1, dims.ssd_inner)
    norm_w = ssd_norm[0].reshape(1, dims.ssd_inner)
    yb_p, h_p = ssd_block(dims, xbc_p, proj, dt_t, bias_g, alog_g, d_exp, norm_w, row0=0, nrows=tp, chained=True)
    yb_s, h_s = ssd_block(dims, xbc_s, proj, dt_t, bias_g, alog_g, d_exp, norm_w, row0=tp, nrows=ts,
                          chained=False, h0=state_ssd[0])

    oc_p = cross_attention(dims, proj, k_p, v_p, row0=0, nseq=dims.batch, length=dims.seq)
    oc_s = cross_attention(dims, proj, cache_mem_k[0], cache_mem_v[0], row0=tp, nseq=dims.dec_batch,
                           length=dims.dec_seq)

    mix = merge_branches(dims, (va_p, yb_p, oc_p), (va_s, yb_s, oc_s), bf(a_out[0]), bf(ssd_out[0]),
                         bf(xatt_out[0]), proj)
    x1 = matmul(mix, w_o, F32, residual_parts=(xp2, xs2), tn=512, name="out_proj")

    xn2 = rmsnorm(x1, norm_ffn[0], BF16)
    qp = matmul(xn2, peer_wq, F32, tn=512, name="peer_query")
    s1, e1, s2, e2, tau = peer_select(dims, qp, peer_subkeys[0])
    ffn = peer_mix(dims, xn2, cast_layer(peer_u), cast_layer(peer_v), s1, e1, s2, e2, tau)
    y_prompt = rmsnorm(x1, norm_final, F32, residual=ffn, row0=0, nrows=tp).reshape(dims.batch, dims.seq, d)
    y_sample = rmsnorm(x1, norm_final, F32, residual=ffn, row0=tp, nrows=ts).reshape(
        dims.dec_batch, dims.dec_seq, d)
    return (y_prompt, y_sample, mem_k_p, mem_v_p, conv_a_p, ssd_conv_p, h_p[None],
            conv_a_s, ssd_conv_s, h_s[None])


def kernel(x_prompt, x_sample, mem_prompt, cache_mem_k, cache_mem_v, state_conv_a, state_ssd_conv, state_ssd, norm_mix, norm_mem, norm_ffn, norm_final, w_in, a_conv_w, a_out, ssd_conv_w, ssd_conv_b, ssd_dt_bias, ssd_a_log, ssd_d, ssd_norm, ssd_out, w_mem_k, w_mem_v, xatt_out, w_o, peer_wq, peer_subkeys, peer_u, peer_v):
    return forward(FULL, x_prompt, x_sample, mem_prompt, cache_mem_k, cache_mem_v, state_conv_a,
                   state_ssd_conv, state_ssd, norm_mix, norm_mem, norm_ffn, norm_final, w_in,
                   a_conv_w, a_out, ssd_conv_w, ssd_conv_b, ssd_dt_bias, ssd_a_log, ssd_d, ssd_norm,
                   ssd_out, w_mem_k, w_mem_v, xatt_out, w_o, peer_wq, peer_subkeys, peer_u, peer_v)
```

```python
import dataclasses
import functools

import jax
import jax.numpy as jnp
import numpy as np
from jax import lax
from jax.experimental import pallas as pl
from jax.experimental.pallas import tpu as pltpu

F32 = jnp.float32
BF16 = jnp.bfloat16
EPS = 1e-6
HIGHEST = lax.Precision.HIGHEST
NEG_INF = float("-inf")

LANES = 128
SUBLANES = 8
VMEM_LIMIT_BYTES = 56 * 1024 * 1024


@dataclasses.dataclass(frozen=True)
class Dims:
    d_model: int = 4096
    batch: int = 4
    seq: int = 2048
    dec_batch: int = 128
    dec_seq: int = 8
    a_width: int = 2048
    a_conv: int = 3
    ssd_inner: int = 4096
    ssd_head_dim: int = 64
    ssd_groups: int = 8
    ssd_state: int = 128
    ssd_conv: int = 4
    ssd_chunk: int = 128
    n_mem: int = 256
    xatt_heads: int = 4
    xatt_head_dim: int = 512
    peer_heads: int = 8
    peer_keys: int = 128
    peer_topk: int = 16
    peer_qdim: int = 256

    @property
    def ssd_heads(self):
        return self.ssd_inner // self.ssd_head_dim

    @property
    def heads_per_group(self):
        return self.ssd_heads // self.ssd_groups

    @property
    def group_width(self):
        return self.ssd_inner // self.ssd_groups

    @property
    def ssd_xbc(self):
        return self.ssd_inner + 2 * self.ssd_groups * self.ssd_state

    @property
    def xatt_width(self):
        return self.xatt_heads * self.xatt_head_dim

    @property
    def n_experts(self):
        return self.peer_keys * self.peer_keys

    @property
    def t_prompt(self):
        return self.batch * self.seq

    @property
    def t_sample(self):
        return self.dec_batch * self.dec_seq

    @property
    def tokens(self):
        return self.t_prompt + self.t_sample

    @property
    def off_ain(self):
        return 0

    @property
    def off_abg(self):
        return self.a_width

    @property
    def off_acg(self):
        return 2 * self.a_width

    @property
    def off_z(self):
        return 3 * self.a_width

    @property
    def off_xbc(self):
        return self.off_z + self.ssd_inner

    @property
    def off_q(self):
        return self.off_xbc + self.ssd_xbc

    @property
    def off_gates(self):
        return self.off_q + self.xatt_width

    @property
    def proj_width(self):
        return self.off_gates + 3 * self.d_model


FULL = Dims()


def _params(*sem):
    return pltpu.CompilerParams(dimension_semantics=sem, vmem_limit_bytes=VMEM_LIMIT_BYTES)


def _pick(n, pref):
    t = min(n, pref)
    while n % t:
        t //= 2
    return t


def _rmsnorm_rows(x, g_ref, o_ref):
    inv = lax.rsqrt(jnp.mean(x * x, axis=-1, keepdims=True) + EPS)
    o_ref[...] = ((x * inv) * g_ref[...]).astype(o_ref.dtype)


def _rmsnorm_kernel(x_ref, g_ref, o_ref):
    _rmsnorm_rows(x_ref[...], g_ref, o_ref)


def _add_rmsnorm_kernel(x_ref, r_ref, g_ref, o_ref):
    _rmsnorm_rows(x_ref[...] + r_ref[...], g_ref, o_ref)


def _rmsnorm_parts_kernel(xp_ref, xs_ref, g_ref, o_ref, *, npb):
    i = pl.program_id(0)

    @pl.when(i < npb)
    def _():
        _rmsnorm_rows(xp_ref[...], g_ref, o_ref)

    @pl.when(i >= npb)
    def _():
        _rmsnorm_rows(xs_ref[...], g_ref, o_ref)


def rmsnorm(x, g, out_dtype, residual=None, row0=0, nrows=None, tm=256):
    m, d = x.shape
    nrows = m - row0 if nrows is None else nrows
    tm = _pick(nrows, tm)
    rb0 = row0 // tm
    assert row0 % tm == 0
    row = pl.BlockSpec((tm, d), lambda i: (rb0 + i, 0))
    gspec = pl.BlockSpec((1, d), lambda i: (0, 0))
    args = (x,) if residual is None else (x, residual)
    return pl.pallas_call(
        _rmsnorm_kernel if residual is None else _add_rmsnorm_kernel,
        grid=(nrows // tm,),
        in_specs=[row] * len(args) + [gspec],
        out_specs=pl.BlockSpec((tm, d), lambda i: (i, 0)),
        out_shape=jax.ShapeDtypeStruct((nrows, d), out_dtype),
        compiler_params=_params("parallel"),
        name="rmsnorm" if residual is None else "add_rmsnorm",
    )(*args, g.reshape(1, d))


def rmsnorm_parts(xp, xs, g, out_dtype, tm=256):
    d = xp.shape[1]
    tm = _pick(xs.shape[0], _pick(xp.shape[0], tm))
    npb, nsb = xp.shape[0] // tm, xs.shape[0] // tm
    return pl.pallas_call(
        functools.partial(_rmsnorm_parts_kernel, npb=npb),
        grid=(npb + nsb,),
        in_specs=[pl.BlockSpec((tm, d), lambda i: (jnp.minimum(i, npb - 1), 0)),
                  pl.BlockSpec((tm, d), lambda i: (jnp.maximum(i - npb, 0), 0)),
                  pl.BlockSpec((1, d), lambda i: (0, 0))],
        out_specs=pl.BlockSpec((tm, d), lambda i: (i, 0)),
        out_shape=jax.ShapeDtypeStruct((xp.shape[0] + xs.shape[0], d), out_dtype),
        compiler_params=_params("arbitrary"),
        name="rmsnorm_parts",
    )(xp, xs, g.reshape(1, d))


def _cast_kernel(x_ref, o_ref):
    o_ref[...] = x_ref[0].astype(o_ref.dtype)


def cast_layer(w, dtype=BF16, tr=512):
    _, r, c = w.shape
    tr = _pick(r, tr)
    return pl.pallas_call(
        _cast_kernel,
        grid=(r // tr,),
        in_specs=[pl.BlockSpec((1, tr, c), lambda i: (0, i, 0))],
        out_specs=pl.BlockSpec((tr, c), lambda i: (i, 0)),
        out_shape=jax.ShapeDtypeStruct((r, c), dtype),
        compiler_params=_params("parallel"),
        name="cast_layer",
    )(w)


def _mm_kernel(*refs, npb, layer_weight):
    a_ref, b_ref = refs[:2]
    refs = refs[2:]
    if npb is not None:
        rp_ref, rs_ref = refs[:2]
        refs = refs[2:]
    o_ref = refs[0]
    i = pl.program_id(1)
    if layer_weight:
        w_scr = refs[1]

        @pl.when(i == 0)
        def _():
            w_scr[...] = b_ref[0].astype(BF16)

        w = w_scr[...]
    else:
        w = b_ref[...]
    acc = jnp.dot(a_ref[...], w, preferred_element_type=F32)
    if npb is None:
        o_ref[...] = acc.astype(o_ref.dtype)
    else:
        @pl.when(i < npb)
        def _():
            o_ref[...] = (rp_ref[...] + acc).astype(o_ref.dtype)

        @pl.when(i >= npb)
        def _():
            o_ref[...] = (rs_ref[...] + acc).astype(o_ref.dtype)


def matmul(a, b, out_dtype, residual_parts=None, tm=512, tn=1024, name="matmul"):
    m, k = a.shape
    layer_weight = b.ndim == 3
    n = b.shape[-1]
    tm, tn = _pick(m, tm), _pick(n, tn)
    npb = None
    extra_specs, extra_args = [], []
    if residual_parts is not None:
        rp, rs = residual_parts
        tm = _pick(rs.shape[0], _pick(rp.shape[0], tm))
        npb = rp.shape[0] // tm
        extra_specs = [pl.BlockSpec((tm, tn), lambda j, i: (jnp.minimum(i, npb - 1), j)),
                       pl.BlockSpec((tm, tn), lambda j, i: (jnp.maximum(i - npb, 0), j))]
        extra_args = [rp, rs]
    if layer_weight:
        b_spec = pl.BlockSpec((1, k, tn), lambda j, i: (0, 0, j))
        scratch = [pltpu.VMEM((k, tn), BF16)]
    else:
        b_spec = pl.BlockSpec((k, tn), lambda j, i: (0, j))
        scratch = []
    sequential = layer_weight or npb is not None
    return pl.pallas_call(
        functools.partial(_mm_kernel, npb=npb, layer_weight=layer_weight),
        grid=(n // tn, m // tm),
        in_specs=[pl.BlockSpec((tm, k), lambda j, i: (i, 0)), b_spec] + extra_specs,
        out_specs=pl.BlockSpec((tm, tn), lambda j, i: (i, j)),
        out_shape=jax.ShapeDtypeStruct((m, n), out_dtype),
        scratch_shapes=scratch,
        compiler_params=_params(*(("arbitrary", "arbitrary") if sequential else ("parallel", "parallel"))),
        name=name,
    )(a, b, *extra_args)


def _mm_wt_kernel(a_ref, wt_ref, o_ref, w_scr):
    @pl.when(pl.program_id(1) == 0)
    def _():
        w_scr[...] = wt_ref[...].astype(BF16)

    o_ref[...] = lax.dot_general(a_ref[...], w_scr[...], (((1,), (1,)), ((), ())),
                                 preferred_element_type=F32).astype(o_ref.dtype)


def matmul_wt(a, wt, segments, out_dtype=F32, tm=1024, tn=512, name="matmul_wt"):
    m, k = a.shape
    tm = _pick(m, tm)
    for start, length in segments:
        tn = _pick(length, tn)
        assert start % SUBLANES == 0
    assert all(length % tn == 0 for _, length in segments)
    starts = np.concatenate([np.arange(s, s + l, tn) for s, l in segments]).astype(np.int32)
    nblk = len(starts)
    bounds = np.cumsum([l // tn for _, l in segments])[:-1]
    shifts = [segments[i + 1][0] - (segments[i][0] + segments[i][1]) for i in range(len(segments) - 1)]

    def row_start(j):
        r = segments[0][0] + j * tn
        for b, sh in zip(bounds, shifts):
            r = r + jnp.where(j >= b, sh, 0)
        return pl.multiple_of(r, SUBLANES)

    return pl.pallas_call(
        _mm_wt_kernel,
        grid=(nblk, m // tm),
        in_specs=[pl.BlockSpec((tm, k), lambda j, i: (i, 0)),
                  pl.BlockSpec((pl.Element(tn), pl.Element(k)), lambda j, i: (row_start(j), 0))],
        out_specs=pl.BlockSpec((tm, tn), lambda j, i: (i, j)),
        out_shape=jax.ShapeDtypeStruct((m, nblk * tn), out_dtype),
        scratch_shapes=[pltpu.VMEM((tn, k), BF16)],
        compiler_params=_params("arbitrary", "arbitrary"),
        name=name,
    )(a, wt)


def _dt_proj_kernel(a_ref, wt_ref, o_ref):
    o_ref[...] = lax.dot_general(wt_ref[...].astype(BF16), a_ref[...], (((1,), (1,)), ((), ())),
                                 preferred_element_type=F32)


def dt_proj_t(a, wt, start, nrows, tm=1024):
    m, k = a.shape
    tm = _pick(m, tm)
    assert start % SUBLANES == 0
    return pl.pallas_call(
        _dt_proj_kernel,
        grid=(m // tm,),
        in_specs=[pl.BlockSpec((tm, k), lambda i: (i, 0)),
                  pl.BlockSpec((pl.Element(nrows), pl.Element(k)), lambda i: (start, 0))],
        out_specs=pl.BlockSpec((nrows, tm), lambda i: (0, i)),
        out_shape=jax.ShapeDtypeStruct((nrows, m), F32),
        compiler_params=_params("parallel"),
        name="dt_proj",
    )(a, wt)


def _conv_kernel(*refs, width, tl, gated):
    if gated:
        in_ref, bg_ref, cg_ref, prev_ref, w_ref, o_ref, st_ref, scr = refs
    else:
        in_ref, prev_ref, w_ref, bias_ref, o_ref, scr = refs
    lt = pl.program_id(2)

    @pl.when(lt == 0)
    def _():
        scr[0:SUBLANES, :] = prev_ref[0]

    u = cg_ref[...] * in_ref[...] if gated else in_ref[...]
    scr[SUBLANES:SUBLANES + tl, :] = u
    acc = w_ref[width - 1:width, :] * u
    for k in range(width - 1):
        lo = SUBLANES - (width - 1 - k)
        acc = acc + w_ref[k:k + 1, :] * scr[lo:lo + tl, :]
    tail = scr[tl:tl + SUBLANES, :]
    scr[0:SUBLANES, :] = tail
    if gated:
        o_ref[...] = (bg_ref[...] * acc).astype(o_ref.dtype)
        st_ref[0] = tail
    else:
        y = acc + bias_ref[...]
        o_ref[...] = (y * jax.nn.sigmoid(y)).astype(o_ref.dtype)


def causal_conv(proj, prev8, w, *, row0, nseq, length, cols, col_offs, gated, bias=None, out_dtype=F32,
                tl=1024, tc=512):
    width = w.shape[0]
    tl, tc = _pick(length, tl), _pick(cols, tc)
    while any(o % tc for o in col_offs):
        tc //= 2
    nl = length // tl
    rb0 = row0 // tl
    assert row0 % tl == 0 and all(o % tc == 0 for o in col_offs)

    def blk(off):
        return pl.BlockSpec((tl, tc), lambda s, c, l, off=off: (rb0 + s * nl + l, off // tc + c))

    prev_spec = pl.BlockSpec((1, SUBLANES, tc), lambda s, c, l: (s, 0, c))
    w_spec = pl.BlockSpec((width, tc), lambda s, c, l: (0, c))
    out_spec = pl.BlockSpec((tl, tc), lambda s, c, l: (s * nl + l, c))
    out_shape = jax.ShapeDtypeStruct((nseq * length, cols), out_dtype)
    if gated:
        in_specs = [blk(col_offs[0]), blk(col_offs[1]), blk(col_offs[2]), prev_spec, w_spec]
        args = (proj, proj, proj, prev8, w)
        out_specs = [out_spec, pl.BlockSpec((1, SUBLANES, tc), lambda s, c, l: (s, 0, c))]
        out_shape = [out_shape, jax.ShapeDtypeStruct((nseq, SUBLANES, cols), F32)]
    else:
        in_specs = [blk(col_offs[0]), prev_spec, w_spec, pl.BlockSpec((1, tc), lambda s, c, l: (0, c))]
        args = (proj, prev8, w, bias.reshape(1, cols))
        out_specs = out_spec
    return pl.pallas_call(
        functools.partial(_conv_kernel, width=width, tl=tl, gated=gated),
        grid=(nseq, cols // tc, nl),
        in_specs=in_specs,
        out_specs=out_specs,
        out_shape=out_shape,
        scratch_shapes=[pltpu.VMEM((tl + SUBLANES, tc), F32)],
        compiler_params=_params("parallel", "parallel", "arbitrary"),
        name="gated_conv" if gated else "ssd_conv",
    )(*args)


def _short_conv_kernel(*refs, width, seq, gated):
    if gated:
        in_ref, bg_ref, cg_ref, prev_ref, w_ref, o_ref, u_ref, scr_u, scr_p = refs
    else:
        in_ref, prev_ref, w_ref, bias_ref, o_ref, scr_u, scr_p = refs
    rows = in_ref.shape[0]
    u = cg_ref[...] * in_ref[...] if gated else in_ref[...]
    zeros = jnp.zeros((SUBLANES, u.shape[1]), F32)
    scr_u[0:SUBLANES, :] = zeros
    scr_u[SUBLANES:SUBLANES + rows, :] = u
    scr_p[0:rows, :] = prev_ref[...]
    scr_p[rows:rows + SUBLANES, :] = zeros
    pos = lax.broadcasted_iota(jnp.int32, u.shape, 0) % seq
    acc = w_ref[width - 1:width, :] * u
    for k in range(width - 1):
        shift = width - 1 - k
        lo = SUBLANES - shift
        operand = jnp.where(pos >= shift, scr_u[lo:lo + rows, :], scr_p[lo:lo + rows, :])
        acc = acc + w_ref[k:k + 1, :] * operand
    if gated:
        o_ref[...] = (bg_ref[...] * acc).astype(o_ref.dtype)
        u_ref[...] = u
    else:
        y = acc + bias_ref[...]
        o_ref[...] = (y * jax.nn.sigmoid(y)).astype(o_ref.dtype)


def short_seq_conv(proj, prev8, w, *, row0, nseq, seq, cols, col_offs, gated, bias=None, out_dtype=F32,
                   rows=128, tc=2048):
    width = w.shape[0]
    assert seq == SUBLANES and width - 1 <= seq
    total = nseq * seq
    rows, tc = _pick(total, rows), _pick(cols, tc)
    while any(o % tc for o in col_offs):
        tc //= 2
    rb0 = row0 // rows
    assert row0 % rows == 0

    def blk(off):
        return pl.BlockSpec((rows, tc), lambda r, c, off=off: (rb0 + r, off // tc + c))

    own = pl.BlockSpec((rows, tc), lambda r, c: (r, c))
    w_spec = pl.BlockSpec((width, tc), lambda r, c: (0, c))
    out_shape = jax.ShapeDtypeStruct((total, cols), out_dtype)
    if gated:
        in_specs = [blk(col_offs[0]), blk(col_offs[1]), blk(col_offs[2]), own, w_spec]
        args = (proj, proj, proj, prev8, w)
        out_specs = [own, own]
        out_shape = [out_shape, jax.ShapeDtypeStruct((total, cols), F32)]
    else:
        in_specs = [blk(col_offs[0]), own, w_spec, pl.BlockSpec((1, tc), lambda r, c: (0, c))]
        args = (proj, prev8, w, bias.reshape(1, cols))
        out_specs = own
    return pl.pallas_call(
        functools.partial(_short_conv_kernel, width=width, seq=seq, gated=gated),
        grid=(total // rows, cols // tc),
        in_specs=in_specs,
        out_specs=out_specs,
        out_shape=out_shape,
        scratch_shapes=[pltpu.VMEM((rows + SUBLANES, tc), F32), pltpu.VMEM((rows + SUBLANES, tc), F32)],
        compiler_params=_params("parallel", "parallel"),
        name="gated_conv_short" if gated else "ssd_conv_short",
    )(*args)


def _softplus(x):
    return jnp.maximum(x, 0.0) + jnp.log1p(jnp.exp(-jnp.abs(x)))


def _ssd_kernel(*refs, nseq, dims, chained, gpb):
    if chained:
        (xs_ref, b_ref, c_ref, z_ref, dt_ref, bias_ref, alog_ref, d_ref, nw_ref, sel_ref,
         y_ref, hout_ref, h_scr) = refs
    else:
        (xs_ref, b_ref, c_ref, z_ref, dt_ref, bias_ref, alog_ref, d_ref, nw_ref, sel_ref, h0_ref,
         y_ref, hout_ref) = refs
    q = dims.ssd_chunk
    r_heads = dims.heads_per_group
    p = dims.ssd_head_dim
    gw = dims.group_width
    n = dims.ssd_state
    seg = q // nseq

    if chained:
        @pl.when(pl.program_id(2) == 0)
        def _():
            h_scr[...] = jnp.zeros_like(h_scr)

    for gi in range(gpb):
        gcols = slice(gi * gw, (gi + 1) * gw)
        ncols = slice(gi * n, (gi + 1) * n)
        _ssd_group(
            xs_ref.at[:, gcols], b_ref.at[:, ncols], c_ref.at[:, ncols], z_ref.at[:, gcols],
            dt_ref.at[gi], bias_ref.at[gi], alog_ref.at[gi], d_ref.at[:, gcols], nw_ref.at[:, gcols], sel_ref,
            y_ref.at[:, gcols],
            hout_ref.at[:, gi * r_heads:(gi + 1) * r_heads],
            h_scr.at[gi * r_heads * p:(gi + 1) * r_heads * p, :] if chained else None,
            None if chained else h0_ref.at[:, gi * r_heads:(gi + 1) * r_heads],
            nseq=nseq, dims=dims, chained=chained)


def _ssd_group(xs_ref, b_ref, c_ref, z_ref, dt_ref, bias_ref, alog_ref, d_ref, nw_ref, sel_ref, y_ref,
               hout_ref, h_scr, h0_ref, *, nseq, dims, chained):
    q = dims.ssd_chunk
    r_heads = dims.heads_per_group
    p = dims.ssd_head_dim
    gw = dims.group_width
    seg = q // nseq
    row = lax.broadcasted_iota(jnp.int32, (q, q), 0)
    col = lax.broadcasted_iota(jnp.int32, (q, q), 1)
    same = (row // seg) == (col // seg)
    dt = _softplus(dt_ref[...] + bias_ref[...])
    adt = dt * (-jnp.exp(alog_ref[...]))
    cum_mask = jnp.where(same & (row <= col), 1.0, 0.0).astype(F32)
    acum_t = jnp.dot(adt, cum_mask, precision=HIGHEST, preferred_element_type=F32)
    if nseq == 1:
        atot_t = jnp.broadcast_to(acum_t[:, q - 1:q], (r_heads, q))
    else:
        atot_t = jnp.dot(adt, jnp.where(same, 1.0, 0.0).astype(F32), precision=HIGHEST,
                         preferred_element_type=F32)
    stack = jnp.concatenate(
        [acum_t, dt, dt * jnp.exp(atot_t - acum_t), jnp.exp(acum_t),
         jnp.zeros((q - 4 * r_heads, q), F32)], axis=0)
    cols_form = stack.T
    sel = sel_ref[...]
    hi = cols_form.astype(BF16)
    rest = cols_form - hi.astype(F32)
    mid = rest.astype(BF16)
    low = (rest - mid.astype(F32)).astype(BF16)
    expand = (jnp.dot(hi, sel, preferred_element_type=F32) + jnp.dot(mid, sel, preferred_element_type=F32)
              + jnp.dot(low, sel, preferred_element_type=F32))
    e_dt, e_st, e_ac = expand[:, :gw], expand[:, gw:2 * gw], expand[:, 2 * gw:]

    x = xs_ref[...]
    bb = b_ref[...].astype(BF16)
    cb_ = c_ref[...].astype(BF16)
    cb = lax.dot_general(cb_, bb, (((1,), (1,)), ((), ())), preferred_element_type=F32)
    x_dt = x * e_dt
    x_dt_b = x_dt.astype(BF16)
    causal = same & (row >= col)
    lane = lax.broadcasted_iota(jnp.int32, (q, LANES), 1)
    heads_per_tile = LANES // p

    y_tiles = []
    for tile in range(gw // LANES):
        xt = x_dt_b[:, tile * LANES:(tile + 1) * LANES]
        acc = None
        for k in range(heads_per_tile):
            r = tile * heads_per_tile + k
            segm = cols_form[:, r:r + 1] - acum_t[r:r + 1, :]
            decay = jnp.exp(jnp.where(causal, segm, NEG_INF))
            m_r = (cb * decay).astype(BF16)
            x_r = jnp.where((lane >= k * p) & (lane < (k + 1) * p), xt, jnp.zeros_like(xt))
            part = jnp.dot(m_r, x_r, preferred_element_type=F32)
            acc = part if acc is None else acc + part
        y_tiles.append(acc)
    y = jnp.concatenate(y_tiles, axis=1)

    x_st = x * e_st
    eac_t = jnp.exp(atot_t)

    if chained:
        h = h_scr[...]
        y_off = lax.dot_general(cb_, h.astype(BF16), (((1,), (1,)), ((), ())),
                                preferred_element_type=F32)
        y = y + y_off * e_ac
        s_new = lax.dot_general(x_st.astype(BF16), bb, (((0,), (0,)), ((), ())),
                                preferred_element_type=F32)
        scale = jnp.broadcast_to(eac_t[:, q - 1:q], (r_heads, LANES))
        for r in range(r_heads):
            rows = slice(r * p, (r + 1) * p)
            h_scr[rows, :] = h[rows, :] * jnp.broadcast_to(scale[r:r + 1, :], (p, LANES)) + s_new[rows, :]
        for r in range(r_heads):
            hout_ref[0, r] = h_scr[r * p:(r + 1) * p, :]
    else:
        x_st_t = x_st.T
        rows_q = lax.broadcasted_iota(jnp.int32, (q, gw), 0)
        lanes_q = lax.broadcasted_iota(jnp.int32, (gw, q), 1)
        y_off = jnp.zeros((q, gw), F32)
        for s in range(nseq):
            h_s = h0_ref[s].reshape(r_heads * p, dims.ssd_state)
            y_s = lax.dot_general(cb_, h_s.astype(BF16), (((1,), (1,)), ((), ())),
                                  preferred_element_type=F32)
            y_off = jnp.where(rows_q // seg == s, y_s, y_off)
            xs_s = jnp.where(lanes_q // seg == s, x_st_t, 0.0).astype(BF16)
            s_new = jnp.dot(xs_s, bb, preferred_element_type=F32)
            scale = jnp.broadcast_to(eac_t[:, s * seg:s * seg + 1], (r_heads, LANES))
            for r in range(r_heads):
                rows = slice(r * p, (r + 1) * p)
                hout_ref[s, r] = (h_s[rows, :] * jnp.broadcast_to(scale[r:r + 1, :], (p, LANES))
                                  + s_new[rows, :])
        y = y + y_off * e_ac

    y = y + d_ref[...] * x
    z = z_ref[...]
    y = y * (z * jax.nn.sigmoid(z))
    inv = lax.rsqrt(jnp.mean(y * y, axis=-1, keepdims=True) + EPS)
    y_ref[...] = ((y * inv) * nw_ref[...]).astype(y_ref.dtype)


def _ssd_selector(dims):
    r_heads, p, gw = dims.heads_per_group, dims.ssd_head_dim, dims.group_width
    sel = np.zeros((dims.ssd_chunk, 3 * gw), np.float32)
    for part in range(3):
        for r in range(r_heads):
            sel[(part + 1) * r_heads + r, part * gw + r * p:part * gw + (r + 1) * p] = 1.0
    return jnp.asarray(sel, BF16)


def ssd_block(dims, xbc, proj, dt_t, bias_g, alog_g, d_exp, norm_w, *, row0, nrows, chained, h0=None):
    q = dims.ssd_chunk
    g = dims.ssd_groups
    gw = dims.group_width
    n = dims.ssd_state
    r_heads, p = dims.heads_per_group, dims.ssd_head_dim
    assert LANES % p == 0 and q == LANES and 4 * r_heads <= q
    sel = _ssd_selector(dims)
    rb0 = row0 // q
    zoff = dims.off_z // gw
    boff = dims.ssd_inner // n
    coff = boff + g

    gpb = 2 if chained and all(v % 2 == 0 for v in (g, zoff, boff)) else 1
    gblocks = g // gpb
    if chained:
        nb, nc = dims.batch, dims.seq // q
        grid = (nb, gblocks, nc)
        rowblk = lambda b, gi, c: b * nc + c
        sems = ("parallel", "parallel", "arbitrary")
        nseq = 1
    else:
        nseq = q // dims.dec_seq
        nb = dims.dec_batch // nseq
        grid = (nb, gblocks)
        rowblk = lambda b, gi: b
        sems = ("parallel", "parallel")

    def spec(shape, fn):
        return pl.BlockSpec(shape, fn)

    if chained:
        ix = lambda f: (lambda b, gi, c: f(rowblk(b, gi, c), gi))
    else:
        ix = lambda f: (lambda b, gi: f(rowblk(b, gi), gi))

    in_specs = [
        spec((q, gpb * gw), ix(lambda rb, gi: (rb, gi))),
        spec((q, gpb * n), ix(lambda rb, gi: (rb, boff // gpb + gi))),
        spec((q, gpb * n), ix(lambda rb, gi: (rb, coff // gpb + gi))),
        spec((q, gpb * gw), ix(lambda rb, gi: (rb0 + rb, zoff // gpb + gi))),
        spec((gpb, r_heads, q), ix(lambda rb, gi: (gi, 0, rb0 + rb))),
        spec((gpb, r_heads, 1), ix(lambda rb, gi: (gi, 0, 0))),
        spec((gpb, r_heads, 1), ix(lambda rb, gi: (gi, 0, 0))),
        spec((1, gpb * gw), ix(lambda rb, gi: (0, gi))),
        spec((1, gpb * gw), ix(lambda rb, gi: (0, gi))),
        spec((q, 3 * gw), ix(lambda rb, gi: (0, 0))),
    ]
    args = [xbc, xbc, xbc, proj, dt_t, bias_g, alog_g, d_exp, norm_w, sel]
    y_spec = spec((q, gpb * gw), ix(lambda rb, gi: (rb, gi)))
    if chained:
        h_spec = pl.BlockSpec((1, gpb * r_heads, p, n), lambda b, gi, c: (b, gi, 0, 0))
        h_shape = jax.ShapeDtypeStruct((nb, dims.ssd_heads, p, n), F32)
        scratch = [pltpu.VMEM((gpb * r_heads * p, n), F32)]
    else:
        in_specs.append(pl.BlockSpec((nseq, gpb * r_heads, p, n), lambda b, gi: (b, gi, 0, 0)))
        args.append(h0)
        h_spec = pl.BlockSpec((nseq, gpb * r_heads, p, n), lambda b, gi: (b, gi, 0, 0))
        h_shape = jax.ShapeDtypeStruct((dims.dec_batch, dims.ssd_heads, p, n), F32)
        scratch = []
    return pl.pallas_call(
        functools.partial(_ssd_kernel, nseq=nseq, dims=dims, chained=chained, gpb=gpb),
        grid=grid,
        in_specs=in_specs,
        out_specs=[y_spec, h_spec],
        out_shape=[jax.ShapeDtypeStruct((nrows, dims.ssd_inner), BF16), h_shape],
        scratch_shapes=scratch,
        compiler_params=_params(*sems),
        name="ssd_prompt" if chained else "ssd_sample",
    )(*args)


def _attend(q, k, v, scale):
    s = lax.dot_general(q.astype(BF16), k.astype(BF16), (((1,), (1,)), ((), ())),
                        preferred_element_type=F32) * scale
    s = s - jnp.max(s, axis=-1, keepdims=True)
    e = jnp.exp(s)
    pr = (e / jnp.sum(e, axis=-1, keepdims=True)).astype(BF16)
    return jnp.dot(pr, v.astype(BF16), preferred_element_type=F32)


def _xattn_kernel(q_ref, k_ref, v_ref, o_ref, *, heads, head_dim):
    for h in range(heads):
        cs = slice(h * head_dim, (h + 1) * head_dim)
        o_ref[:, cs] = _attend(q_ref[:, cs], k_ref[0, :, cs], v_ref[0, :, cs], head_dim ** -0.5).astype(o_ref.dtype)


def _xattn_cached_kernel(q_ref, k_ref, v_ref, o_ref, *, heads, head_dim):
    tq = q_ref.shape[0]
    m = k_ref.shape[1]
    k = k_ref[0].reshape(m * heads, head_dim).astype(BF16)
    v = v_ref[0].reshape(m * heads, head_dim).astype(BF16)
    q = jnp.concatenate([q_ref[:, h * head_dim:(h + 1) * head_dim] for h in range(heads)], axis=0)
    s = lax.dot_general(q.astype(BF16), k, (((1,), (1,)), ((), ())), preferred_element_type=F32)
    s = s * (head_dim ** -0.5)
    row_head = lax.broadcasted_iota(jnp.int32, s.shape, 0) // tq
    col_head = lax.broadcasted_iota(jnp.int32, s.shape, 1) % heads
    s = jnp.where(row_head == col_head, s, NEG_INF)
    s = s - jnp.max(s, axis=-1, keepdims=True)
    e = jnp.exp(s)
    pr = (e / jnp.sum(e, axis=-1, keepdims=True)).astype(BF16)
    o = jnp.dot(pr, v, preferred_element_type=F32)
    for h in range(heads):
        o_ref[:, h * head_dim:(h + 1) * head_dim] = o[h * tq:(h + 1) * tq, :].astype(o_ref.dtype)


def cross_attention(dims, proj, k, v, *, row0, nseq, length, tq=512):
    w = dims.xatt_width
    tq = _pick(length, tq)
    nq = length // tq
    rb0 = row0 // tq
    assert row0 % tq == 0 and dims.off_q % w == 0
    qoff = dims.off_q // w
    hd, nh = dims.xatt_head_dim, dims.xatt_heads
    if k.ndim == 4:
        assert nq == 1 and tq % SUBLANES == 0
        kv_spec = pl.BlockSpec((1, dims.n_mem, nh, hd), lambda b: (b, 0, 0, 0))
        return pl.pallas_call(
            functools.partial(_xattn_cached_kernel, heads=nh, head_dim=hd),
            grid=(nseq,),
            in_specs=[pl.BlockSpec((tq, w), lambda b: (rb0 + b, qoff)), kv_spec, kv_spec],
            out_specs=pl.BlockSpec((tq, w), lambda b: (b, 0)),
            out_shape=jax.ShapeDtypeStruct((nseq * length, w), BF16),
            compiler_params=_params("parallel"),
            name="cross_attention_cached",
        )(proj, k, v)
    kv_spec = pl.BlockSpec((1, dims.n_mem, w), lambda b, i: (b, 0, 0))
    return pl.pallas_call(
        functools.partial(_xattn_kernel, heads=nh, head_dim=hd),
        grid=(nseq, nq),
        in_specs=[pl.BlockSpec((tq, w), lambda b, i: (rb0 + b * nq + i, qoff)), kv_spec, kv_spec],
        out_specs=pl.BlockSpec((tq, w), lambda b, i: (b * nq + i, 0)),
        out_shape=jax.ShapeDtypeStruct((nseq * length, w), BF16),
        compiler_params=_params("parallel", "parallel"),
        name="cross_attention",
    )(proj, k, v)


def _merge_kernel(vap_ref, ybp_ref, ocp_ref, vas_ref, ybs_ref, ocs_ref, wa_ref, wb_ref, wc_ref,
                  ga_ref, gb_ref, gc_ref, o_ref, *, npb):
    i = pl.program_id(1)

    def body(va_ref, yb_ref, oc_ref):
        ha = jnp.dot(va_ref[...], wa_ref[...], preferred_element_type=F32)
        hb = jnp.dot(yb_ref[...], wb_ref[...], preferred_element_type=F32)
        hc = jnp.dot(oc_ref[...], wc_ref[...], preferred_element_type=F32)
        mix = (jax.nn.sigmoid(ga_ref[...]) * ha + jax.nn.sigmoid(gb_ref[...]) * hb
               + jax.nn.sigmoid(gc_ref[...]) * hc)
        o_ref[...] = mix.astype(o_ref.dtype)

    @pl.when(i < npb)
    def _():
        body(vap_ref, ybp_ref, ocp_ref)

    @pl.when(i >= npb)
    def _():
        body(vas_ref, ybs_ref, ocs_ref)


def merge_branches(dims, prompt, sample, wa, wb, wc, proj, tm=256, tn=512):
    d = dims.d_model
    mp, ms = prompt[0].shape[0], sample[0].shape[0]
    tm, tn = _pick(ms, _pick(mp, tm)), _pick(d, tn)
    npb = mp // tm
    g0 = dims.off_gates // tn
    gd = d // tn
    assert dims.off_gates % tn == 0

    def lhs_p(a):
        return pl.BlockSpec((tm, a.shape[1]), lambda j, i: (jnp.minimum(i, npb - 1), 0))

    def lhs_s(a):
        return pl.BlockSpec((tm, a.shape[1]), lambda j, i: (jnp.maximum(i - npb, 0), 0))

    def rhs(width):
        return pl.BlockSpec((width, tn), lambda j, i: (0, j))

    def gate(k):
        return pl.BlockSpec((tm, tn), lambda j, i, k=k: (i, g0 + k * gd + j))

    return pl.pallas_call(
        functools.partial(_merge_kernel, npb=npb),
        grid=(d // tn, (mp + ms) // tm),
        in_specs=[lhs_p(a) for a in prompt] + [lhs_s(a) for a in sample]
        + [rhs(wa.shape[0]), rhs(wb.shape[0]), rhs(wc.shape[0]), gate(0), gate(1), gate(2)],
        out_specs=pl.BlockSpec((tm, tn), lambda j, i: (i, j)),
        out_shape=jax.ShapeDtypeStruct((mp + ms, d), BF16),
        compiler_params=_params("arbitrary", "arbitrary"),
        name="merge_branches",
    )(*prompt, *sample, wa, wb, wc, proj, proj, proj)


def _peer_select_kernel(q_ref, keys_ref, s1_ref, e1_ref, s2_ref, e2_ref, tau_ref, work, tops, cand, *, dims):
    heads, nk, topk = dims.peer_heads, dims.peer_keys, dims.peer_topk
    half = dims.peer_qdim // 2
    tb = q_ref.shape[0]
    rank = lax.broadcasted_iota(jnp.int32, (topk, tb), 0)

    for h in range(heads):
        for part, out in ((0, s1_ref), (1, s2_ref)):
            k = 2 * h + part
            qh = q_ref[:, k * half:(k + 1) * half].astype(BF16)
            s = lax.dot_general(keys_ref[k].astype(BF16), qh, (((1,), (1,)), ((), ())),
                                preferred_element_type=F32)
            out[h] = s
            work[k] = s
            tops[k] = jnp.full((topk, tb), NEG_INF, F32)

    def extract(r, carry):
        for k in range(2 * heads):
            s = work[k]
            m = jnp.max(s, axis=0, keepdims=True)
            tops[k] = jnp.where(rank == r, m, tops[k])
            work[k] = jnp.where(s == m, NEG_INF, s)
        return carry
    lax.fori_loop(0, topk, extract, 0)

    for h in range(heads):
        a, b = tops[2 * h], tops[2 * h + 1]
        cand[h, 0:topk, :] = a[0:1, :] + b
        for pi in range(1, topk):
            lo = topk + (pi - 1) * SUBLANES
            cand[h, lo:lo + SUBLANES, :] = a[pi:pi + 1, :] + b[0:SUBLANES, :]

    def threshold(r, taus):
        new = []
        for h in range(heads):
            c = cand[h]
            m = jnp.max(c, axis=0, keepdims=True)
            cand[h] = jnp.where(c == m, NEG_INF, c)
            new.append(m)
        return tuple(new)
    taus = lax.fori_loop(0, topk, threshold, tuple(jnp.zeros((1, tb), F32) for _ in range(heads)))

    for h in range(heads):
        a, b = tops[2 * h], tops[2 * h + 1]
        tau = taus[h]
        z = None
        for pi in range(topk):
            rows = topk if pi == 0 else SUBLANES
            c = a[pi:pi + 1, :] + b[0:rows, :]
            part = jnp.sum(jnp.where(c >= tau, jnp.exp(c - (a[0:1, :] + b[0:1, :])), 0.0), axis=0, keepdims=True)
            z = part if z is None else z + part
        e1_ref[h] = jnp.exp(s1_ref[h] - a[0:1, :]) / z
        e2_ref[h] = jnp.exp(s2_ref[h] - b[0:1, :])
        tau_ref[h:h + 1, :] = tau


def peer_select(dims, q, subkeys, tb=128):
    t = q.shape[0]
    heads, nk = dims.peer_heads, dims.peer_keys
    half = dims.peer_qdim // 2
    assert dims.peer_topk >= SUBLANES and heads == SUBLANES
    tab = jax.ShapeDtypeStruct((heads, nk, t), F32)
    tab_spec = pl.BlockSpec((heads, nk, tb), lambda i: (0, 0, i))
    return pl.pallas_call(
        functools.partial(_peer_select_kernel, dims=dims),
        grid=(t // tb,),
        in_specs=[pl.BlockSpec((tb, q.shape[1]), lambda i: (i, 0)),
                  pl.BlockSpec((2 * heads, nk, half), lambda i: (0, 0, 0))],
        out_specs=[tab_spec, tab_spec, tab_spec, tab_spec, pl.BlockSpec((heads, tb), lambda i: (0, i))],
        out_shape=[tab, tab, tab, tab, jax.ShapeDtypeStruct((heads, t), F32)],
        scratch_shapes=[pltpu.VMEM((2 * heads, nk, tb), F32),
                        pltpu.VMEM((2 * heads, dims.peer_topk, tb), F32),
                        pltpu.VMEM((heads, dims.peer_topk + (dims.peer_topk - 1) * SUBLANES, tb), F32)],
        compiler_params=_params("parallel"),
        name="peer_select",
    )(q, subkeys.reshape(2 * heads, nk, half))


def _peer_mix_kernel(x_ref, u_ref, v_ref, s1_ref, e1_ref, s2_ref, e2_ref, tau_ref, o_ref, gw_scr, w_scr, *,
                     dims, sub):
    heads, nk = dims.peer_heads, dims.peer_keys
    eb = pl.program_id(1)
    te = u_ref.shape[0]

    @pl.when(eb == 0)
    def _():
        o_ref[...] = jnp.zeros_like(o_ref)

    tb = x_ref.shape[0]
    tr, tl = 64, LANES
    def weight_tile(ii, r0, c0):
        cs = slice(c0, c0 + tl)
        w = None
        for h in range(heads):
            s = s1_ref[h, ii, :, cs] + s2_ref[h, r0:r0 + tr, cs]
            wh = jnp.where(s >= tau_ref[h:h + 1, cs], e1_ref[h, ii, :, cs] * e2_ref[h, r0:r0 + tr, cs], 0.0)
            w = wh if w is None else w + wh
        w_scr[ii * nk + r0:ii * nk + r0 + tr, cs] = w

    for ii in range(te // nk):
        for r0 in range(0, nk, tr):
            for c0 in range(0, tb, tl):
                weight_tile(ii, r0, c0)
    x = x_ref[...]
    for sb in range(te // sub):
        rows = slice(sb * sub, (sb + 1) * sub)
        hh = lax.dot_general(u_ref[rows, :], x, (((1,), (1,)), ((), ())), preferred_element_type=F32)
        gelu = 0.5 * hh * (1.0 + lax.erf(hh * (2.0 ** -0.5)))
        gw_scr[rows, :] = (gelu * w_scr[rows, :]).astype(BF16)
        o_ref[...] += lax.dot_general(gw_scr[rows, :], v_ref[rows, :], (((0,), (0,)), ((), ())),
                                      preferred_element_type=F32)


def peer_mix(dims, xn, u, v, s1, e1, s2, e2, tau, tb=512, te=512, sub=256):
    t, d = xn.shape
    heads, nk = dims.peer_heads, dims.peer_keys
    tb, te = _pick(t, tb), _pick(dims.n_experts, te)
    sub = min(sub, te)
    assert te % sub == 0 and sub % nk == 0
    ni = te // nk
    row_tab = lambda a: a.reshape(heads, nk, 1, t)
    row_spec = pl.BlockSpec((heads, ni, 1, tb), lambda i, e: (0, e, 0, i))
    col_spec = pl.BlockSpec((heads, nk, tb), lambda i, e: (0, 0, i))
    return pl.pallas_call(
        functools.partial(_peer_mix_kernel, dims=dims, sub=sub),
        grid=(t // tb, dims.n_experts // te),
        in_specs=[pl.BlockSpec((tb, d), lambda i, e: (i, 0)),
                  pl.BlockSpec((te, d), lambda i, e: (e, 0)),
                  pl.BlockSpec((te, d), lambda i, e: (e, 0)),
                  row_spec, row_spec, col_spec, col_spec,
                  pl.BlockSpec((heads, tb), lambda i, e: (0, i))],
        out_specs=pl.BlockSpec((tb, d), lambda i, e: (i, 0)),
        out_shape=jax.ShapeDtypeStruct((t, d), F32),
        scratch_shapes=[pltpu.VMEM((te, tb), BF16), pltpu.VMEM((te, tb), F32)],
        compiler_params=_params("parallel", "arbitrary"),
        name="peer_mix",
    )(xn, u, v, row_tab(s1), row_tab(e1), s2, e2, tau)


def _pad_prev(state, width):
    return jnp.pad(state, ((0, 0), (SUBLANES - (width - 1), 0), (0, 0)))


def forward(dims, x_prompt, x_sample, mem_prompt, cache_mem_k, cache_mem_v, state_conv_a,
            state_ssd_conv, state_ssd, norm_mix, norm_mem, norm_ffn, norm_final, w_in,
            a_conv_w, a_out, ssd_conv_w, ssd_conv_b, ssd_dt_bias, ssd_a_log, ssd_d, ssd_norm,
            ssd_out, w_mem_k, w_mem_v, xatt_out, w_o, peer_wq, peer_subkeys, peer_u, peer_v):
    d = dims.d_model
    tp, ts, t = dims.t_prompt, dims.t_sample, dims.tokens
    g, rh, p, n = dims.ssd_groups, dims.heads_per_group, dims.ssd_head_dim, dims.ssd_state
    bf = lambda a: a.astype(BF16)

    xp2, xs2 = x_prompt.reshape(tp, d), x_sample.reshape(ts, d)

    mn = rmsnorm(mem_prompt.reshape(dims.batch * dims.n_mem, d), norm_mem[0], BF16)
    xw = dims.xatt_width
    k_p = matmul(mn, w_mem_k, F32, tn=512, name="mem_k").reshape(dims.batch, dims.n_mem, xw)
    v_p = matmul(mn, w_mem_v, F32, tn=512, name="mem_v").reshape(dims.batch, dims.n_mem, xw)
    mem_k_p = k_p.reshape(1, dims.batch, dims.n_mem, dims.xatt_heads, dims.xatt_head_dim)
    mem_v_p = v_p.reshape(1, dims.batch, dims.n_mem, dims.xatt_heads, dims.xatt_head_dim)

    xn = rmsnorm_parts(xp2, xs2, norm_mix[0], BF16)
    wt = jnp.swapaxes(w_in, 1, 2)[0]
    dt0 = 3 * dims.a_width + dims.ssd_inner + dims.ssd_xbc
    proj = matmul_wt(xn, wt, [(0, dt0), (dt0 + dims.ssd_heads, dims.proj_width - dt0)], tm=512, tn=1024,
                     name="in_proj")
    dt_t = dt_proj_t(xn, wt, dt0, dims.ssd_heads).reshape(g, rh, t)

    a_cols = (dims.off_ain, dims.off_abg, dims.off_acg)
    zeros_a = jnp.zeros((dims.batch, SUBLANES, dims.a_width), F32)
    va_p, st_a_p = causal_conv(proj, zeros_a, a_conv_w[0], row0=0, nseq=dims.batch, length=dims.seq,
                               cols=dims.a_width, col_offs=a_cols, gated=True, out_dtype=BF16)
    prev_a = _pad_prev(state_conv_a[0], dims.a_conv).reshape(dims.dec_batch * SUBLANES, dims.a_width)
    va_s, u_s = short_seq_conv(proj, prev_a, a_conv_w[0], row0=tp, nseq=dims.dec_batch, seq=dims.dec_seq,
                               cols=dims.a_width, col_offs=a_cols, gated=True, out_dtype=BF16)
    na = dims.a_conv - 1
    conv_a_p = st_a_p[None, :, SUBLANES - na:, :]
    conv_a_s = u_s.reshape(dims.dec_batch, dims.dec_seq, dims.a_width)[None, :, dims.dec_seq - na:, :]

    zeros_b = jnp.zeros((dims.batch, SUBLANES, dims.ssd_xbc), F32)
    xbc_p = causal_conv(proj, zeros_b, ssd_conv_w[0], row0=0, nseq=dims.batch, length=dims.seq,
                        cols=dims.ssd_xbc, col_offs=(dims.off_xbc,), gated=False, bias=ssd_conv_b[0])
    prev_b = _pad_prev(state_ssd_conv[0], dims.ssd_conv).reshape(dims.dec_batch * SUBLANES, dims.ssd_xbc)
    xbc_s = short_seq_conv(proj, prev_b, ssd_conv_w[0], row0=tp, nseq=dims.dec_batch, seq=dims.dec_seq,
                           cols=dims.ssd_xbc, col_offs=(dims.off_xbc,), gated=False, bias=ssd_conv_b[0])
    nb = dims.ssd_conv - 1
    x0, x1c = dims.off_xbc, dims.off_xbc + dims.ssd_xbc
    ssd_conv_p = jnp.stack([lax.slice(proj, ((b + 1) * dims.seq - nb, x0), ((b + 1) * dims.seq, x1c))
                            for b in range(dims.batch)])[None]
    ssd_conv_s = lax.slice(proj.reshape(t // dims.dec_seq, dims.dec_seq, dims.proj_width),
                           (tp // dims.dec_seq, dims.dec_seq - nb, x0),
                           (t // dims.dec_seq, dims.dec_seq, x1c))[None]

    bias_g = ssd_dt_bias[0].reshape(g, rh, 1)
    alog_g = ssd_a_log[0].reshape(g, rh, 1)
    d_exp = jnp.repeat(ssd_d[0], p).reshape(1, dims.ssd_inner)
    norm_w = ssd_norm[0].reshape(1, dims.ssd_inner)
    yb_p, h_p = ssd_block(dims, xbc_p, proj, dt_t, bias_g, alog_g, d_exp, norm_w, row0=0, nrows=tp, chained=True)
    yb_s, h_s = ssd_block(dims, xbc_s, proj, dt_t, bias_g, alog_g, d_exp, norm_w, row0=tp, nrows=ts,
                          chained=False, h0=state_ssd[0])

    oc_p = cross_attention(dims, proj, k_p, v_p, row0=0, nseq=dims.batch, length=dims.seq)
    oc_s = cross_attention(dims, proj, cache_mem_k[0], cache_mem_v[0], row0=tp, nseq=dims.dec_batch,
                           length=dims.dec_seq)

    mix = merge_branches(dims, (va_p, yb_p, oc_p), (va_s, yb_s, oc_s), bf(a_out[0]), bf(ssd_out[0]),
                         bf(xatt_out[0]), proj)
    x1 = matmul(mix, bf(w_o[0]), F32, residual_parts=(xp2, xs2), name="out_proj")

    xn2 = rmsnorm(x1, norm_ffn[0], BF16)
    qp = matmul(xn2, bf(peer_wq[0]), F32, name="peer_query")
    s1, e1, s2, e2, tau = peer_select(dims, qp, peer_subkeys[0])
    ffn = peer_mix(dims, xn2, cast_layer(peer_u), cast_layer(peer_v), s1, e1, s2, e2, tau)
    y_prompt = rmsnorm(x1, norm_final, F32, residual=ffn, row0=0, nrows=tp).reshape(dims.batch, dims.seq, d)
    y_sample = rmsnorm(x1, norm_final, F32, residual=ffn, row0=tp, nrows=ts).reshape(
        dims.dec_batch, dims.dec_seq, d)
    return (y_prompt, y_sample, mem_k_p, mem_v_p, conv_a_p, ssd_conv_p, h_p[None],
            conv_a_s, ssd_conv_s, h_s[None])


def kernel(x_prompt, x_sample, mem_prompt, cache_mem_k, cache_mem_v, state_conv_a, state_ssd_conv, state_ssd, norm_mix, norm_mem, norm_ffn, norm_final, w_in, a_conv_w, a_out, ssd_conv_w, ssd_conv_b, ssd_dt_bias, ssd_a_log, ssd_d, ssd_norm, ssd_out, w_mem_k, w_mem_v, xatt_out, w_o, peer_wq, peer_subkeys, peer_u, peer_v):
    return forward(FULL, x_prompt, x_sample, mem_prompt, cache_mem_k, cache_mem_v, state_conv_a,
                   state_ssd_conv, state_ssd, norm_mix, norm_mem, norm_ffn, norm_final, w_in,
                   a_conv_w, a_out, ssd_conv_w, ssd_conv_b, ssd_dt_bias, ssd_a_log, ssd_d, ssd_norm,
                   ssd_out, w_mem_k, w_mem_v, xatt_out, w_o, peer_wq, peer_subkeys, peer_u, peer_v)
```

```python
import dataclasses
import functools

import jax
import jax.numpy as jnp
import numpy as np
from jax import lax
from jax.experimental import pallas as pl
from jax.experimental.pallas import tpu as pltpu

F32 = jnp.float32
BF16 = jnp.bfloat16
EPS = 1e-6
HIGHEST = lax.Precision.HIGHEST
NEG_INF = float("-inf")

LANES = 128
SUBLANES = 8
VMEM_LIMIT_BYTES = 56 * 1024 * 1024


@dataclasses.dataclass(frozen=True)
class Dims:
    d_model: int = 4096
    batch: int = 4
    seq: int = 2048
    dec_batch: int = 128
    dec_seq: int = 8
    a_width: int = 2048
    a_conv: int = 3
    ssd_inner: int = 4096
    ssd_head_dim: int = 64
    ssd_groups: int = 8
    ssd_state: int = 128
    ssd_conv: int = 4
    ssd_chunk: int = 128
    n_mem: int = 256
    xatt_heads: int = 4
    xatt_head_dim: int = 512
    peer_heads: int = 8
    peer_keys: int = 128
    peer_topk: int = 16
    peer_qdim: int = 256

    @property
    def ssd_heads(self):
        return self.ssd_inner // self.ssd_head_dim

    @property
    def heads_per_group(self):
        return self.ssd_heads // self.ssd_groups

    @property
    def group_width(self):
        return self.ssd_inner // self.ssd_groups

    @property
    def ssd_xbc(self):
        return self.ssd_inner + 2 * self.ssd_groups * self.ssd_state

    @property
    def xatt_width(self):
        return self.xatt_heads * self.xatt_head_dim

    @property
    def n_experts(self):
        return self.peer_keys * self.peer_keys

    @property
    def t_prompt(self):
        return self.batch * self.seq

    @property
    def t_sample(self):
        return self.dec_batch * self.dec_seq

    @property
    def tokens(self):
        return self.t_prompt + self.t_sample

    @property
    def off_ain(self):
        return 0

    @property
    def off_abg(self):
        return self.a_width

    @property
    def off_acg(self):
        return 2 * self.a_width

    @property
    def off_z(self):
        return 3 * self.a_width

    @property
    def off_xbc(self):
        return self.off_z + self.ssd_inner

    @property
    def off_q(self):
        return self.off_xbc + self.ssd_xbc

    @property
    def off_gates(self):
        return self.off_q + self.xatt_width

    @property
    def proj_width(self):
        return self.off_gates + 3 * self.d_model


FULL = Dims()


def _params(*sem):
    return pltpu.CompilerParams(dimension_semantics=sem, vmem_limit_bytes=VMEM_LIMIT_BYTES)


def _pick(n, pref):
    t = min(n, pref)
    while n % t:
        t //= 2
    return t


def _rmsnorm_rows(x, g_ref, o_ref):
    inv = lax.rsqrt(jnp.mean(x * x, axis=-1, keepdims=True) + EPS)
    o_ref[...] = ((x * inv) * g_ref[...]).astype(o_ref.dtype)


def _rmsnorm_kernel(x_ref, g_ref, o_ref):
    _rmsnorm_rows(x_ref[...], g_ref, o_ref)


def _add_rmsnorm_kernel(x_ref, r_ref, g_ref, o_ref):
    _rmsnorm_rows(x_ref[...] + r_ref[...], g_ref, o_ref)


def _rmsnorm_parts_kernel(xp_ref, xs_ref, g_ref, o_ref, *, npb):
    i = pl.program_id(0)

    @pl.when(i < npb)
    def _():
        _rmsnorm_rows(xp_ref[...], g_ref, o_ref)

    @pl.when(i >= npb)
    def _():
        _rmsnorm_rows(xs_ref[...], g_ref, o_ref)


def rmsnorm(x, g, out_dtype, residual=None, row0=0, nrows=None, tm=256):
    m, d = x.shape
    nrows = m - row0 if nrows is None else nrows
    tm = _pick(nrows, tm)
    rb0 = row0 // tm
    assert row0 % tm == 0
    row = pl.BlockSpec((tm, d), lambda i: (rb0 + i, 0))
    gspec = pl.BlockSpec((1, d), lambda i: (0, 0))
    args = (x,) if residual is None else (x, residual)
    return pl.pallas_call(
        _rmsnorm_kernel if residual is None else _add_rmsnorm_kernel,
        grid=(nrows // tm,),
        in_specs=[row] * len(args) + [gspec],
        out_specs=pl.BlockSpec((tm, d), lambda i: (i, 0)),
        out_shape=jax.ShapeDtypeStruct((nrows, d), out_dtype),
        compiler_params=_params("parallel"),
        name="rmsnorm" if residual is None else "add_rmsnorm",
    )(*args, g.reshape(1, d))


def rmsnorm_parts(xp, xs, g, out_dtype, tm=256):
    d = xp.shape[1]
    tm = _pick(xs.shape[0], _pick(xp.shape[0], tm))
    npb, nsb = xp.shape[0] // tm, xs.shape[0] // tm
    return pl.pallas_call(
        functools.partial(_rmsnorm_parts_kernel, npb=npb),
        grid=(npb + nsb,),
        in_specs=[pl.BlockSpec((tm, d), lambda i: (jnp.minimum(i, npb - 1), 0)),
                  pl.BlockSpec((tm, d), lambda i: (jnp.maximum(i - npb, 0), 0)),
                  pl.BlockSpec((1, d), lambda i: (0, 0))],
        out_specs=pl.BlockSpec((tm, d), lambda i: (i, 0)),
        out_shape=jax.ShapeDtypeStruct((xp.shape[0] + xs.shape[0], d), out_dtype),
        compiler_params=_params("arbitrary"),
        name="rmsnorm_parts",
    )(xp, xs, g.reshape(1, d))


def _cast_kernel(x_ref, o_ref):
    o_ref[...] = x_ref[0].astype(o_ref.dtype)


def cast_layer(w, dtype=BF16, tr=512):
    _, r, c = w.shape
    tr = _pick(r, tr)
    return pl.pallas_call(
        _cast_kernel,
        grid=(r // tr,),
        in_specs=[pl.BlockSpec((1, tr, c), lambda i: (0, i, 0))],
        out_specs=pl.BlockSpec((tr, c), lambda i: (i, 0)),
        out_shape=jax.ShapeDtypeStruct((r, c), dtype),
        compiler_params=_params("parallel"),
        name="cast_layer",
    )(w)


def _mm_kernel(*refs, npb, layer_weight):
    a_ref, b_ref = refs[:2]
    refs = refs[2:]
    if npb is not None:
        rp_ref, rs_ref = refs[:2]
        refs = refs[2:]
    o_ref = refs[0]
    i = pl.program_id(1)
    if layer_weight:
        w_scr = refs[1]

        @pl.when(i == 0)
        def _():
            w_scr[...] = b_ref[0].astype(BF16)

        w = w_scr[...]
    else:
        w = b_ref[...]
    acc = jnp.dot(a_ref[...], w, preferred_element_type=F32)
    if npb is None:
        o_ref[...] = acc.astype(o_ref.dtype)
    else:
        @pl.when(i < npb)
        def _():
            o_ref[...] = (rp_ref[...] + acc).astype(o_ref.dtype)

        @pl.when(i >= npb)
        def _():
            o_ref[...] = (rs_ref[...] + acc).astype(o_ref.dtype)


def matmul(a, b, out_dtype, residual_parts=None, tm=512, tn=1024, name="matmul"):
    m, k = a.shape
    layer_weight = b.ndim == 3
    n = b.shape[-1]
    tm, tn = _pick(m, tm), _pick(n, tn)
    npb = None
    extra_specs, extra_args = [], []
    if residual_parts is not None:
        rp, rs = residual_parts
        tm = _pick(rs.shape[0], _pick(rp.shape[0], tm))
        npb = rp.shape[0] // tm
        extra_specs = [pl.BlockSpec((tm, tn), lambda j, i: (jnp.minimum(i, npb - 1), j)),
                       pl.BlockSpec((tm, tn), lambda j, i: (jnp.maximum(i - npb, 0), j))]
        extra_args = [rp, rs]
    if layer_weight:
        b_spec = pl.BlockSpec((1, k, tn), lambda j, i: (0, 0, j))
        scratch = [pltpu.VMEM((k, tn), BF16)]
    else:
        b_spec = pl.BlockSpec((k, tn), lambda j, i: (0, j))
        scratch = []
    sequential = layer_weight or npb is not None
    return pl.pallas_call(
        functools.partial(_mm_kernel, npb=npb, layer_weight=layer_weight),
        grid=(n // tn, m // tm),
        in_specs=[pl.BlockSpec((tm, k), lambda j, i: (i, 0)), b_spec] + extra_specs,
        out_specs=pl.BlockSpec((tm, tn), lambda j, i: (i, j)),
        out_shape=jax.ShapeDtypeStruct((m, n), out_dtype),
        scratch_shapes=scratch,
        compiler_params=_params(*(("arbitrary", "arbitrary") if sequential else ("parallel", "parallel"))),
        name=name,
    )(a, b, *extra_args)


def _mm_wt_kernel(a_ref, wt_ref, o_ref, w_scr):
    @pl.when(pl.program_id(1) == 0)
    def _():
        w_scr[...] = wt_ref[...].astype(BF16)

    o_ref[...] = lax.dot_general(a_ref[...], w_scr[...], (((1,), (1,)), ((), ())),
                                 preferred_element_type=F32).astype(o_ref.dtype)


def matmul_wt(a, wt, segments, out_dtype=F32, tm=1024, tn=512, name="matmul_wt"):
    m, k = a.shape
    tm = _pick(m, tm)
    for start, length in segments:
        tn = _pick(length, tn)
        assert start % SUBLANES == 0
    assert all(length % tn == 0 for _, length in segments)
    starts = np.concatenate([np.arange(s, s + l, tn) for s, l in segments]).astype(np.int32)
    nblk = len(starts)
    bounds = np.cumsum([l // tn for _, l in segments])[:-1]
    shifts = [segments[i + 1][0] - (segments[i][0] + segments[i][1]) for i in range(len(segments) - 1)]

    def row_start(j):
        r = segments[0][0] + j * tn
        for b, sh in zip(bounds, shifts):
            r = r + jnp.where(j >= b, sh, 0)
        return pl.multiple_of(r, SUBLANES)

    return pl.pallas_call(
        _mm_wt_kernel,
        grid=(nblk, m // tm),
        in_specs=[pl.BlockSpec((tm, k), lambda j, i: (i, 0)),
                  pl.BlockSpec((pl.Element(tn), pl.Element(k)), lambda j, i: (row_start(j), 0))],
        out_specs=pl.BlockSpec((tm, tn), lambda j, i: (i, j)),
        out_shape=jax.ShapeDtypeStruct((m, nblk * tn), out_dtype),
        scratch_shapes=[pltpu.VMEM((tn, k), BF16)],
        compiler_params=_params("arbitrary", "arbitrary"),
        name=name,
    )(a, wt)


def _dt_proj_kernel(a_ref, wt_ref, o_ref):
    o_ref[...] = lax.dot_general(wt_ref[...].astype(BF16), a_ref[...], (((1,), (1,)), ((), ())),
                                 preferred_element_type=F32)


def dt_proj_t(a, wt, start, nrows, tm=1024):
    m, k = a.shape
    tm = _pick(m, tm)
    assert start % SUBLANES == 0
    return pl.pallas_call(
        _dt_proj_kernel,
        grid=(m // tm,),
        in_specs=[pl.BlockSpec((tm, k), lambda i: (i, 0)),
                  pl.BlockSpec((pl.Element(nrows), pl.Element(k)), lambda i: (start, 0))],
        out_specs=pl.BlockSpec((nrows, tm), lambda i: (0, i)),
        out_shape=jax.ShapeDtypeStruct((nrows, m), F32),
        compiler_params=_params("parallel"),
        name="dt_proj",
    )(a, wt)


def _conv_kernel(*refs, width, tl, gated):
    if gated:
        in_ref, bg_ref, cg_ref, prev_ref, w_ref, o_ref, st_ref, scr = refs
    else:
        in_ref, prev_ref, w_ref, bias_ref, o_ref, scr = refs
    lt = pl.program_id(2)

    @pl.when(lt == 0)
    def _():
        scr[0:SUBLANES, :] = prev_ref[0]

    u = cg_ref[...] * in_ref[...] if gated else in_ref[...]
    scr[SUBLANES:SUBLANES + tl, :] = u
    acc = w_ref[width - 1:width, :] * u
    for k in range(width - 1):
        lo = SUBLANES - (width - 1 - k)
        acc = acc + w_ref[k:k + 1, :] * scr[lo:lo + tl, :]
    tail = scr[tl:tl + SUBLANES, :]
    scr[0:SUBLANES, :] = tail
    if gated:
        o_ref[...] = (bg_ref[...] * acc).astype(o_ref.dtype)
        st_ref[0] = tail
    else:
        y = acc + bias_ref[...]
        o_ref[...] = (y * jax.nn.sigmoid(y)).astype(o_ref.dtype)


def causal_conv(proj, prev8, w, *, row0, nseq, length, cols, col_offs, gated, bias=None, out_dtype=F32,
                tl=1024, tc=512):
    width = w.shape[0]
    tl, tc = _pick(length, tl), _pick(cols, tc)
    while any(o % tc for o in col_offs):
        tc //= 2
    nl = length // tl
    rb0 = row0 // tl
    assert row0 % tl == 0 and all(o % tc == 0 for o in col_offs)

    def blk(off):
        return pl.BlockSpec((tl, tc), lambda s, c, l, off=off: (rb0 + s * nl + l, off // tc + c))

    prev_spec = pl.BlockSpec((1, SUBLANES, tc), lambda s, c, l: (s, 0, c))
    w_spec = pl.BlockSpec((width, tc), lambda s, c, l: (0, c))
    out_spec = pl.BlockSpec((tl, tc), lambda s, c, l: (s * nl + l, c))
    out_shape = jax.ShapeDtypeStruct((nseq * length, cols), out_dtype)
    if gated:
        in_specs = [blk(col_offs[0]), blk(col_offs[1]), blk(col_offs[2]), prev_spec, w_spec]
        args = (proj, proj, proj, prev8, w)
        out_specs = [out_spec, pl.BlockSpec((1, SUBLANES, tc), lambda s, c, l: (s, 0, c))]
        out_shape = [out_shape, jax.ShapeDtypeStruct((nseq, SUBLANES, cols), F32)]
    else:
        in_specs = [blk(col_offs[0]), prev_spec, w_spec, pl.BlockSpec((1, tc), lambda s, c, l: (0, c))]
        args = (proj, prev8, w, bias.reshape(1, cols))
        out_specs = out_spec
    return pl.pallas_call(
        functools.partial(_conv_kernel, width=width, tl=tl, gated=gated),
        grid=(nseq, cols // tc, nl),
        in_specs=in_specs,
        out_specs=out_specs,
        out_shape=out_shape,
        scratch_shapes=[pltpu.VMEM((tl + SUBLANES, tc), F32)],
        compiler_params=_params("parallel", "parallel", "arbitrary"),
        name="gated_conv" if gated else "ssd_conv",
    )(*args)


def _short_conv_kernel(*refs, width, seq, gated):
    if gated:
        in_ref, bg_ref, cg_ref, prev_ref, w_ref, o_ref, u_ref, scr_u, scr_p = refs
    else:
        in_ref, prev_ref, w_ref, bias_ref, o_ref, scr_u, scr_p = refs
    rows = in_ref.shape[0]
    u = cg_ref[...] * in_ref[...] if gated else in_ref[...]
    zeros = jnp.zeros((SUBLANES, u.shape[1]), F32)
    scr_u[0:SUBLANES, :] = zeros
    scr_u[SUBLANES:SUBLANES + rows, :] = u
    scr_p[0:rows, :] = prev_ref[...]
    scr_p[rows:rows + SUBLANES, :] = zeros
    pos = lax.broadcasted_iota(jnp.int32, u.shape, 0) % seq
    acc = w_ref[width - 1:width, :] * u
    for k in range(width - 1):
        shift = width - 1 - k
        lo = SUBLANES - shift
        operand = jnp.where(pos >= shift, scr_u[lo:lo + rows, :], scr_p[lo:lo + rows, :])
        acc = acc + w_ref[k:k + 1, :] * operand
    if gated:
        o_ref[...] = (bg_ref[...] * acc).astype(o_ref.dtype)
        u_ref[...] = u
    else:
        y = acc + bias_ref[...]
        o_ref[...] = (y * jax.nn.sigmoid(y)).astype(o_ref.dtype)


def short_seq_conv(proj, prev8, w, *, row0, nseq, seq, cols, col_offs, gated, bias=None, out_dtype=F32,
                   rows=128, tc=2048):
    width = w.shape[0]
    assert seq == SUBLANES and width - 1 <= seq
    total = nseq * seq
    rows, tc = _pick(total, rows), _pick(cols, tc)
    while any(o % tc for o in col_offs):
        tc //= 2
    rb0 = row0 // rows
    assert row0 % rows == 0

    def blk(off):
        return pl.BlockSpec((rows, tc), lambda r, c, off=off: (rb0 + r, off // tc + c))

    own = pl.BlockSpec((rows, tc), lambda r, c: (r, c))
    w_spec = pl.BlockSpec((width, tc), lambda r, c: (0, c))
    out_shape = jax.ShapeDtypeStruct((total, cols), out_dtype)
    if gated:
        in_specs = [blk(col_offs[0]), blk(col_offs[1]), blk(col_offs[2]), own, w_spec]
        args = (proj, proj, proj, prev8, w)
        out_specs = [own, own]
        out_shape = [out_shape, jax.ShapeDtypeStruct((total, cols), F32)]
    else:
        in_specs = [blk(col_offs[0]), own, w_spec, pl.BlockSpec((1, tc), lambda r, c: (0, c))]
        args = (proj, prev8, w, bias.reshape(1, cols))
        out_specs = own
    return pl.pallas_call(
        functools.partial(_short_conv_kernel, width=width, seq=seq, gated=gated),
        grid=(total // rows, cols // tc),
        in_specs=in_specs,
        out_specs=out_specs,
        out_shape=out_shape,
        scratch_shapes=[pltpu.VMEM((rows + SUBLANES, tc), F32), pltpu.VMEM((rows + SUBLANES, tc), F32)],
        compiler_params=_params("parallel", "parallel"),
        name="gated_conv_short" if gated else "ssd_conv_short",
    )(*args)


def _softplus(x):
    return jnp.maximum(x, 0.0) + jnp.log1p(jnp.exp(-jnp.abs(x)))


def _ssd_kernel(*refs, nseq, dims, chained, gpb):
    if chained:
        (xs_ref, b_ref, c_ref, z_ref, dt_ref, bias_ref, alog_ref, d_ref, nw_ref, sel_ref,
         y_ref, hout_ref, h_scr) = refs
    else:
        (xs_ref, b_ref, c_ref, z_ref, dt_ref, bias_ref, alog_ref, d_ref, nw_ref, sel_ref, h0_ref,
         y_ref, hout_ref) = refs
    q = dims.ssd_chunk
    r_heads = dims.heads_per_group
    p = dims.ssd_head_dim
    gw = dims.group_width
    n = dims.ssd_state
    seg = q // nseq

    if chained:
        @pl.when(pl.program_id(2) == 0)
        def _():
            h_scr[...] = jnp.zeros_like(h_scr)

    for gi in range(gpb):
        gcols = slice(gi * gw, (gi + 1) * gw)
        ncols = slice(gi * n, (gi + 1) * n)
        _ssd_group(
            xs_ref.at[:, gcols], b_ref.at[:, ncols], c_ref.at[:, ncols], z_ref.at[:, gcols],
            dt_ref.at[gi], bias_ref.at[gi], alog_ref.at[gi], d_ref.at[:, gcols], nw_ref.at[:, gcols], sel_ref,
            y_ref.at[:, gcols],
            hout_ref.at[:, gi * r_heads:(gi + 1) * r_heads],
            h_scr.at[gi * r_heads * p:(gi + 1) * r_heads * p, :] if chained else None,
            None if chained else h0_ref.at[:, gi * r_heads:(gi + 1) * r_heads],
            nseq=nseq, dims=dims, chained=chained)


def _ssd_group(xs_ref, b_ref, c_ref, z_ref, dt_ref, bias_ref, alog_ref, d_ref, nw_ref, sel_ref, y_ref,
               hout_ref, h_scr, h0_ref, *, nseq, dims, chained):
    q = dims.ssd_chunk
    r_heads = dims.heads_per_group
    p = dims.ssd_head_dim
    gw = dims.group_width
    seg = q // nseq
    row = lax.broadcasted_iota(jnp.int32, (q, q), 0)
    col = lax.broadcasted_iota(jnp.int32, (q, q), 1)
    same = (row // seg) == (col // seg)
    dt = _softplus(dt_ref[...] + bias_ref[...])
    adt = dt * (-jnp.exp(alog_ref[...]))
    cum_mask = jnp.where(same & (row <= col), 1.0, 0.0).astype(F32)
    acum_t = jnp.dot(adt, cum_mask, precision=HIGHEST, preferred_element_type=F32)
    if nseq == 1:
        atot_t = jnp.broadcast_to(acum_t[:, q - 1:q], (r_heads, q))
    else:
        atot_t = jnp.dot(adt, jnp.where(same, 1.0, 0.0).astype(F32), precision=HIGHEST,
                         preferred_element_type=F32)
    stack = jnp.concatenate(
        [acum_t, dt, dt * jnp.exp(atot_t - acum_t), jnp.exp(acum_t),
         jnp.zeros((q - 4 * r_heads, q), F32)], axis=0)
    cols_form = stack.T
    sel = sel_ref[...]
    hi = cols_form.astype(BF16)
    rest = cols_form - hi.astype(F32)
    mid = rest.astype(BF16)
    low = (rest - mid.astype(F32)).astype(BF16)
    expand = (jnp.dot(hi, sel, preferred_element_type=F32) + jnp.dot(mid, sel, preferred_element_type=F32)
              + jnp.dot(low, sel, preferred_element_type=F32))
    e_dt, e_st, e_ac = expand[:, :gw], expand[:, gw:2 * gw], expand[:, 2 * gw:]

    x = xs_ref[...]
    bb = b_ref[...].astype(BF16)
    cb_ = c_ref[...].astype(BF16)
    cb = lax.dot_general(cb_, bb, (((1,), (1,)), ((), ())), preferred_element_type=F32)
    x_dt = x * e_dt
    x_dt_b = x_dt.astype(BF16)
    causal = same & (row >= col)
    lane = lax.broadcasted_iota(jnp.int32, (q, LANES), 1)
    heads_per_tile = LANES // p

    y_tiles = []
    for tile in range(gw // LANES):
        xt = x_dt_b[:, tile * LANES:(tile + 1) * LANES]
        acc = None
        for k in range(heads_per_tile):
            r = tile * heads_per_tile + k
            segm = cols_form[:, r:r + 1] - acum_t[r:r + 1, :]
            decay = jnp.exp(jnp.where(causal, segm, NEG_INF))
            m_r = (cb * decay).astype(BF16)
            x_r = jnp.where((lane >= k * p) & (lane < (k + 1) * p), xt, jnp.zeros_like(xt))
            part = jnp.dot(m_r, x_r, preferred_element_type=F32)
            acc = part if acc is None else acc + part
        y_tiles.append(acc)
    y = jnp.concatenate(y_tiles, axis=1)

    x_st = x * e_st
    eac_t = jnp.exp(atot_t)

    if chained:
        h = h_scr[...]
        y_off = lax.dot_general(cb_, h.astype(BF16), (((1,), (1,)), ((), ())),
                                preferred_element_type=F32)
        y = y + y_off * e_ac
        s_new = lax.dot_general(x_st.astype(BF16), bb, (((0,), (0,)), ((), ())),
                                preferred_element_type=F32)
        scale = jnp.broadcast_to(eac_t[:, q - 1:q], (r_heads, LANES))
        for r in range(r_heads):
            rows = slice(r * p, (r + 1) * p)
            h_scr[rows, :] = h[rows, :] * jnp.broadcast_to(scale[r:r + 1, :], (p, LANES)) + s_new[rows, :]
        for r in range(r_heads):
            hout_ref[0, r] = h_scr[r * p:(r + 1) * p, :]
    else:
        x_st_t = x_st.T
        rows_q = lax.broadcasted_iota(jnp.int32, (q, gw), 0)
        lanes_q = lax.broadcasted_iota(jnp.int32, (gw, q), 1)
        y_off = jnp.zeros((q, gw), F32)
        for s in range(nseq):
            h_s = h0_ref[s].reshape(r_heads * p, dims.ssd_state)
            y_s = lax.dot_general(cb_, h_s.astype(BF16), (((1,), (1,)), ((), ())),
                                  preferred_element_type=F32)
            y_off = jnp.where(rows_q // seg == s, y_s, y_off)
            xs_s = jnp.where(lanes_q // seg == s, x_st_t, 0.0).astype(BF16)
            s_new = jnp.dot(xs_s, bb, preferred_element_type=F32)
            scale = jnp.broadcast_to(eac_t[:, s * seg:s * seg + 1], (r_heads, LANES))
            for r in range(r_heads):
                rows = slice(r * p, (r + 1) * p)
                hout_ref[s, r] = (h_s[rows, :] * jnp.broadcast_to(scale[r:r + 1, :], (p, LANES))
                                  + s_new[rows, :])
        y = y + y_off * e_ac

    y = y + d_ref[...] * x
    z = z_ref[...]
    y = y * (z * jax.nn.sigmoid(z))
    inv = lax.rsqrt(jnp.mean(y * y, axis=-1, keepdims=True) + EPS)
    y_ref[...] = ((y * inv) * nw_ref[...]).astype(y_ref.dtype)


def _ssd_selector(dims):
    r_heads, p, gw = dims.heads_per_group, dims.ssd_head_dim, dims.group_width
    sel = np.zeros((dims.ssd_chunk, 3 * gw), np.float32)
    for part in range(3):
        for r in range(r_heads):
            sel[(part + 1) * r_heads + r, part * gw + r * p:part * gw + (r + 1) * p] = 1.0
    return jnp.asarray(sel, BF16)


def ssd_block(dims, xbc, proj, dt_t, bias_g, alog_g, d_exp, norm_w, *, row0, nrows, chained, h0=None):
    q = dims.ssd_chunk
    g = dims.ssd_groups
    gw = dims.group_width
    n = dims.ssd_state
    r_heads, p = dims.heads_per_group, dims.ssd_head_dim
    assert LANES % p == 0 and q == LANES and 4 * r_heads <= q
    sel = _ssd_selector(dims)
    rb0 = row0 // q
    zoff = dims.off_z // gw
    boff = dims.ssd_inner // n
    coff = boff + g

    gpb = 2 if chained and all(v % 2 == 0 for v in (g, zoff, boff)) else 1
    gblocks = g // gpb
    if chained:
        nb, nc = dims.batch, dims.seq // q
        grid = (nb, gblocks, nc)
        rowblk = lambda b, gi, c: b * nc + c
        sems = ("parallel", "parallel", "arbitrary")
        nseq = 1
    else:
        nseq = q // dims.dec_seq
        nb = dims.dec_batch // nseq
        grid = (nb, gblocks)
        rowblk = lambda b, gi: b
        sems = ("parallel", "parallel")

    def spec(shape, fn):
        return pl.BlockSpec(shape, fn)

    if chained:
        ix = lambda f: (lambda b, gi, c: f(rowblk(b, gi, c), gi))
    else:
        ix = lambda f: (lambda b, gi: f(rowblk(b, gi), gi))

    in_specs = [
        spec((q, gpb * gw), ix(lambda rb, gi: (rb, gi))),
        spec((q, gpb * n), ix(lambda rb, gi: (rb, boff // gpb + gi))),
        spec((q, gpb * n), ix(lambda rb, gi: (rb, coff // gpb + gi))),
        spec((q, gpb * gw), ix(lambda rb, gi: (rb0 + rb, zoff // gpb + gi))),
        spec((gpb, r_heads, q), ix(lambda rb, gi: (gi, 0, rb0 + rb))),
        spec((gpb, r_heads, 1), ix(lambda rb, gi: (gi, 0, 0))),
        spec((gpb, r_heads, 1), ix(lambda rb, gi: (gi, 0, 0))),
        spec((1, gpb * gw), ix(lambda rb, gi: (0, gi))),
        spec((1, gpb * gw), ix(lambda rb, gi: (0, gi))),
        spec((q, 3 * gw), ix(lambda rb, gi: (0, 0))),
    ]
    args = [xbc, xbc, xbc, proj, dt_t, bias_g, alog_g, d_exp, norm_w, sel]
    y_spec = spec((q, gpb * gw), ix(lambda rb, gi: (rb, gi)))
    if chained:
        h_spec = pl.BlockSpec((1, gpb * r_heads, p, n), lambda b, gi, c: (b, gi, 0, 0))
        h_shape = jax.ShapeDtypeStruct((nb, dims.ssd_heads, p, n), F32)
        scratch = [pltpu.VMEM((gpb * r_heads * p, n), F32)]
    else:
        in_specs.append(pl.BlockSpec((nseq, gpb * r_heads, p, n), lambda b, gi: (b, gi, 0, 0)))
        args.append(h0)
        h_spec = pl.BlockSpec((nseq, gpb * r_heads, p, n), lambda b, gi: (b, gi, 0, 0))
        h_shape = jax.ShapeDtypeStruct((dims.dec_batch, dims.ssd_heads, p, n), F32)
        scratch = []
    return pl.pallas_call(
        functools.partial(_ssd_kernel, nseq=nseq, dims=dims, chained=chained, gpb=gpb),
        grid=grid,
        in_specs=in_specs,
        out_specs=[y_spec, h_spec],
        out_shape=[jax.ShapeDtypeStruct((nrows, dims.ssd_inner), BF16), h_shape],
        scratch_shapes=scratch,
        compiler_params=_params(*sems),
        name="ssd_prompt" if chained else "ssd_sample",
    )(*args)


def _attend(q, k, v, scale):
    s = lax.dot_general(q.astype(BF16), k.astype(BF16), (((1,), (1,)), ((), ())),
                        preferred_element_type=F32) * scale
    s = s - jnp.max(s, axis=-1, keepdims=True)
    e = jnp.exp(s)
    pr = (e / jnp.sum(e, axis=-1, keepdims=True)).astype(BF16)
    return jnp.dot(pr, v.astype(BF16), preferred_element_type=F32)


def _xattn_kernel(q_ref, k_ref, v_ref, o_ref, *, heads, head_dim):
    for h in range(heads):
        cs = slice(h * head_dim, (h + 1) * head_dim)
        o_ref[:, cs] = _attend(q_ref[:, cs], k_ref[0, :, cs], v_ref[0, :, cs], head_dim ** -0.5).astype(o_ref.dtype)


def _xattn_cached_kernel(q_ref, k_ref, v_ref, o_ref, *, heads, head_dim):
    tq = q_ref.shape[0]
    m = k_ref.shape[1]
    k = k_ref[0].reshape(m * heads, head_dim).astype(BF16)
    v = v_ref[0].reshape(m * heads, head_dim).astype(BF16)
    q = jnp.concatenate([q_ref[:, h * head_dim:(h + 1) * head_dim] for h in range(heads)], axis=0)
    s = lax.dot_general(q.astype(BF16), k, (((1,), (1,)), ((), ())), preferred_element_type=F32)
    s = s * (head_dim ** -0.5)
    row_head = lax.broadcasted_iota(jnp.int32, s.shape, 0) // tq
    col_head = lax.broadcasted_iota(jnp.int32, s.shape, 1) % heads
    s = jnp.where(row_head == col_head, s, NEG_INF)
    s = s - jnp.max(s, axis=-1, keepdims=True)
    e = jnp.exp(s)
    pr = (e / jnp.sum(e, axis=-1, keepdims=True)).astype(BF16)
    o = jnp.dot(pr, v, preferred_element_type=F32)
    for h in range(heads):
        o_ref[:, h * head_dim:(h + 1) * head_dim] = o[h * tq:(h + 1) * tq, :].astype(o_ref.dtype)


def cross_attention(dims, proj, k, v, *, row0, nseq, length, tq=512):
    w = dims.xatt_width
    tq = _pick(length, tq)
    nq = length // tq
    rb0 = row0 // tq
    assert row0 % tq == 0 and dims.off_q % w == 0
    qoff = dims.off_q // w
    hd, nh = dims.xatt_head_dim, dims.xatt_heads
    if k.ndim == 4:
        assert nq == 1 and tq % SUBLANES == 0
        kv_spec = pl.BlockSpec((1, dims.n_mem, nh, hd), lambda b: (b, 0, 0, 0))
        return pl.pallas_call(
            functools.partial(_xattn_cached_kernel, heads=nh, head_dim=hd),
            grid=(nseq,),
            in_specs=[pl.BlockSpec((tq, w), lambda b: (rb0 + b, qoff)), kv_spec, kv_spec],
            out_specs=pl.BlockSpec((tq, w), lambda b: (b, 0)),
            out_shape=jax.ShapeDtypeStruct((nseq * length, w), BF16),
            compiler_params=_params("parallel"),
            name="cross_attention_cached",
        )(proj, k, v)
    kv_spec = pl.BlockSpec((1, dims.n_mem, w), lambda b, i: (b, 0, 0))
    return pl.pallas_call(
        functools.partial(_xattn_kernel, heads=nh, head_dim=hd),
        grid=(nseq, nq),
        in_specs=[pl.BlockSpec((tq, w), lambda b, i: (rb0 + b * nq + i, qoff)), kv_spec, kv_spec],
        out_specs=pl.BlockSpec((tq, w), lambda b, i: (b * nq + i, 0)),
        out_shape=jax.ShapeDtypeStruct((nseq * length, w), BF16),
        compiler_params=_params("parallel", "parallel"),
        name="cross_attention",
    )(proj, k, v)


def _merge_kernel(vap_ref, ybp_ref, ocp_ref, vas_ref, ybs_ref, ocs_ref, wa_ref, wb_ref, wc_ref,
                  ga_ref, gb_ref, gc_ref, o_ref, *, npb):
    i = pl.program_id(1)

    def body(va_ref, yb_ref, oc_ref):
        ha = jnp.dot(va_ref[...], wa_ref[...], preferred_element_type=F32)
        hb = jnp.dot(yb_ref[...], wb_ref[...], preferred_element_type=F32)
        hc = jnp.dot(oc_ref[...], wc_ref[...], preferred_element_type=F32)
        mix = (jax.nn.sigmoid(ga_ref[...]) * ha + jax.nn.sigmoid(gb_ref[...]) * hb
               + jax.nn.sigmoid(gc_ref[...]) * hc)
        o_ref[...] = mix.astype(o_ref.dtype)

    @pl.when(i < npb)
    def _():
        body(vap_ref, ybp_ref, ocp_ref)

    @pl.when(i >= npb)
    def _():
        body(vas_ref, ybs_ref, ocs_ref)


def merge_branches(dims, prompt, sample, wa, wb, wc, proj, tm=256, tn=512):
    d = dims.d_model
    mp, ms = prompt[0].shape[0], sample[0].shape[0]
    tm, tn = _pick(ms, _pick(mp, tm)), _pick(d, tn)
    npb = mp // tm
    g0 = dims.off_gates // tn
    gd = d // tn
    assert dims.off_gates % tn == 0

    def lhs_p(a):
        return pl.BlockSpec((tm, a.shape[1]), lambda j, i: (jnp.minimum(i, npb - 1), 0))

    def lhs_s(a):
        return pl.BlockSpec((tm, a.shape[1]), lambda j, i: (jnp.maximum(i - npb, 0), 0))

    def rhs(width):
        return pl.BlockSpec((width, tn), lambda j, i: (0, j))

    def gate(k):
        return pl.BlockSpec((tm, tn), lambda j, i, k=k: (i, g0 + k * gd + j))

    return pl.pallas_call(
        functools.partial(_merge_kernel, npb=npb),
        grid=(d // tn, (mp + ms) // tm),
        in_specs=[lhs_p(a) for a in prompt] + [lhs_s(a) for a in sample]
        + [rhs(wa.shape[0]), rhs(wb.shape[0]), rhs(wc.shape[0]), gate(0), gate(1), gate(2)],
        out_specs=pl.BlockSpec((tm, tn), lambda j, i: (i, j)),
        out_shape=jax.ShapeDtypeStruct((mp + ms, d), BF16),
        compiler_params=_params("arbitrary", "arbitrary"),
        name="merge_branches",
    )(*prompt, *sample, wa, wb, wc, proj, proj, proj)


def _peer_select_kernel(q_ref, keys_ref, c1_ref, e1_ref, r2_ref, e2_ref, scores, work, tops, cand, ranks, *,
                        dims):
    heads, nk, topk = dims.peer_heads, dims.peer_keys, dims.peer_topk
    half = dims.peer_qdim // 2
    tb = q_ref.shape[0]
    rank = lax.broadcasted_iota(jnp.int32, (topk, tb), 0)

    for k in range(2 * heads):
        qh = q_ref[:, k * half:(k + 1) * half].astype(BF16)
        s = lax.dot_general(keys_ref[k].astype(BF16), qh, (((1,), (1,)), ((), ())),
                            preferred_element_type=F32)
        scores[k] = s
        work[k] = s
        tops[k] = jnp.full((topk, tb), NEG_INF, F32)

    for h in range(heads):
        ranks[h] = jnp.full((nk, tb), float(topk), F32)

    def extract(r, carry):
        rf = r.astype(F32)
        for k in range(2 * heads):
            s = work[k]
            m = jnp.max(s, axis=0, keepdims=True)
            hit = s == m
            tops[k] = jnp.where(rank == r, m, tops[k])
            work[k] = jnp.where(hit, NEG_INF, s)
            if k % 2:
                ranks[k // 2] = jnp.where(hit, rf, ranks[k // 2])
        return carry
    lax.fori_loop(0, topk, extract, 0)

    for h in range(heads):
        a, b = tops[2 * h], tops[2 * h + 1]
        cand[h, 0:topk, :] = a[0:1, :] + b
        for pi in range(1, topk):
            lo = topk + (pi - 1) * SUBLANES
            cand[h, lo:lo + SUBLANES, :] = a[pi:pi + 1, :] + b[0:SUBLANES, :]

    def threshold(r, taus):
        new = []
        for h in range(heads):
            c = cand[h]
            m = jnp.max(c, axis=0, keepdims=True)
            cand[h] = jnp.where(c == m, NEG_INF, c)
            new.append(m)
        return tuple(new)
    taus = lax.fori_loop(0, topk, threshold, tuple(jnp.zeros((1, tb), F32) for _ in range(heads)))

    for h in range(heads):
        a, b = tops[2 * h], tops[2 * h + 1]
        tau = taus[h]
        z = None
        for pi in range(topk):
            rows = topk if pi == 0 else SUBLANES
            c = a[pi:pi + 1, :] + b[0:rows, :]
            part = jnp.sum(jnp.where(c >= tau, jnp.exp(c - (a[0:1, :] + b[0:1, :])), 0.0), axis=0, keepdims=True)
            z = part if z is None else z + part
        s1, s2 = scores[2 * h], scores[2 * h + 1]
        count = jnp.zeros_like(s1)
        for qi in range(topk):
            reach = a + b[qi:qi + 1, :] >= tau
            alpha = jnp.min(jnp.where(reach, a, float("inf")), axis=0, keepdims=True)
            count = count + jnp.where(s1 >= alpha, 1.0, 0.0)
        c1_ref[h] = count
        r2_ref[h] = ranks[h].astype(r2_ref.dtype)
        e1_ref[h] = jnp.exp(s1 - a[0:1, :]) * (0.5 / z)
        e2_ref[h] = jnp.exp(s2 - b[0:1, :]).astype(e2_ref.dtype)


def peer_select(dims, q, subkeys, tb=128):
    t = q.shape[0]
    heads, nk = dims.peer_heads, dims.peer_keys
    half = dims.peer_qdim // 2
    assert dims.peer_topk >= SUBLANES and heads == SUBLANES
    tab = jax.ShapeDtypeStruct((heads, nk, t), F32)
    tab16 = jax.ShapeDtypeStruct((heads, nk, t), BF16)
    tab_spec = pl.BlockSpec((heads, nk, tb), lambda i: (0, 0, i))
    return pl.pallas_call(
        functools.partial(_peer_select_kernel, dims=dims),
        grid=(t // tb,),
        in_specs=[pl.BlockSpec((tb, q.shape[1]), lambda i: (i, 0)),
                  pl.BlockSpec((2 * heads, nk, half), lambda i: (0, 0, 0))],
        out_specs=[tab_spec, tab_spec, tab_spec, tab_spec],
        out_shape=[tab, tab, tab16, tab16],
        scratch_shapes=[pltpu.VMEM((2 * heads, nk, tb), F32),
                        pltpu.VMEM((2 * heads, nk, tb), F32),
                        pltpu.VMEM((2 * heads, dims.peer_topk, tb), F32),
                        pltpu.VMEM((heads, dims.peer_topk + (dims.peer_topk - 1) * SUBLANES, tb), F32),
                        pltpu.VMEM((heads, nk, tb), F32)],
        compiler_params=_params("parallel"),
        name="peer_select",
    )(q, subkeys.reshape(2 * heads, nk, half))


def _peer_mix_kernel(x_ref, u_ref, v_ref, c1_ref, e1_ref, r2_ref, e2_ref, o_ref, gw_scr, w_scr, *, dims, sub):
    heads, nk = dims.peer_heads, dims.peer_keys
    eb = pl.program_id(1)
    te = u_ref.shape[0]

    @pl.when(eb == 0)
    def _():
        o_ref[...] = jnp.zeros_like(o_ref)

    tb = x_ref.shape[0]
    tr, tl = 2 * SUBLANES, LANES
    zero = jnp.zeros((tr, tl), BF16)
    for ii in range(te // nk):
        for c0 in range(0, tb, tl):
            cs = slice(c0, c0 + tl)
            acc = [None] * (nk // tr)
            for h in range(heads):
                count = jnp.broadcast_to(c1_ref[h, ii, :, cs], (tr, tl)).astype(BF16)
                e1 = jnp.broadcast_to(e1_ref[h, ii, :, cs], (tr, tl)).astype(BF16)
                for s in range(nk // tr):
                    rs = slice(s * tr, (s + 1) * tr)
                    wh = jnp.where(r2_ref[h, rs, cs] < count, e1 * e2_ref[h, rs, cs], zero)
                    acc[s] = wh if acc[s] is None else acc[s] + wh
            for s in range(nk // tr):
                w_scr[ii * nk + s * tr:ii * nk + (s + 1) * tr, cs] = acc[s]
    x = x_ref[...]
    for sb in range(te // sub):
        rows = slice(sb * sub, (sb + 1) * sub)
        hh = lax.dot_general(u_ref[rows, :], x, (((1,), (1,)), ((), ())), preferred_element_type=F32)
        gelu2 = hh * (1.0 + lax.erf(hh * (2.0 ** -0.5)))
        gw_scr[rows, :] = gelu2.astype(BF16) * w_scr[rows, :]
        o_ref[...] += lax.dot_general(gw_scr[rows, :], v_ref[rows, :], (((0,), (0,)), ((), ())),
                                      preferred_element_type=F32)


def peer_mix(dims, xn, u, v, c1, e1, r2, e2, tb=512, te=512, sub=512):
    t, d = xn.shape
    heads, nk = dims.peer_heads, dims.peer_keys
    tb, te = _pick(t, tb), _pick(dims.n_experts, te)
    sub = min(sub, te)
    assert te % sub == 0 and sub % nk == 0
    ni = te // nk
    row_tab = lambda a: a.reshape(heads, nk, 1, t)
    row_spec = pl.BlockSpec((heads, ni, 1, tb), lambda i, e: (0, e, 0, i))
    col_spec = pl.BlockSpec((heads, nk, tb), lambda i, e: (0, 0, i))
    return pl.pallas_call(
        functools.partial(_peer_mix_kernel, dims=dims, sub=sub),
        grid=(t // tb, dims.n_experts // te),
        in_specs=[pl.BlockSpec((tb, d), lambda i, e: (i, 0)),
                  pl.BlockSpec((te, d), lambda i, e: (e, 0)),
                  pl.BlockSpec((te, d), lambda i, e: (e, 0)),
                  row_spec, row_spec, col_spec, col_spec],
        out_specs=pl.BlockSpec((tb, d), lambda i, e: (i, 0)),
        out_shape=jax.ShapeDtypeStruct((t, d), F32),
        scratch_shapes=[pltpu.VMEM((te, tb), BF16), pltpu.VMEM((te, tb), BF16)],
        compiler_params=_params("parallel", "arbitrary"),
        name="peer_mix",
    )(xn, u, v, row_tab(c1), row_tab(e1), r2, e2)


def _pad_prev(state, width):
    return jnp.pad(state, ((0, 0), (SUBLANES - (width - 1), 0), (0, 0)))


def forward(dims, x_prompt, x_sample, mem_prompt, cache_mem_k, cache_mem_v, state_conv_a,
            state_ssd_conv, state_ssd, norm_mix, norm_mem, norm_ffn, norm_final, w_in,
            a_conv_w, a_out, ssd_conv_w, ssd_conv_b, ssd_dt_bias, ssd_a_log, ssd_d, ssd_norm,
            ssd_out, w_mem_k, w_mem_v, xatt_out, w_o, peer_wq, peer_subkeys, peer_u, peer_v):
    d = dims.d_model
    tp, ts, t = dims.t_prompt, dims.t_sample, dims.tokens
    g, rh, p, n = dims.ssd_groups, dims.heads_per_group, dims.ssd_head_dim, dims.ssd_state
    bf = lambda a: a.astype(BF16)

    xp2, xs2 = x_prompt.reshape(tp, d), x_sample.reshape(ts, d)

    mn = rmsnorm(mem_prompt.reshape(dims.batch * dims.n_mem, d), norm_mem[0], BF16)
    xw = dims.xatt_width
    k_p = matmul(mn, w_mem_k, F32, tn=512, name="mem_k").reshape(dims.batch, dims.n_mem, xw)
    v_p = matmul(mn, w_mem_v, F32, tn=512, name="mem_v").reshape(dims.batch, dims.n_mem, xw)
    mem_k_p = k_p.reshape(1, dims.batch, dims.n_mem, dims.xatt_heads, dims.xatt_head_dim)
    mem_v_p = v_p.reshape(1, dims.batch, dims.n_mem, dims.xatt_heads, dims.xatt_head_dim)

    xn = rmsnorm_parts(xp2, xs2, norm_mix[0], BF16)
    wt = jnp.swapaxes(w_in, 1, 2)[0]
    dt0 = 3 * dims.a_width + dims.ssd_inner + dims.ssd_xbc
    proj = matmul_wt(xn, wt, [(0, dt0), (dt0 + dims.ssd_heads, dims.proj_width - dt0)], tm=512, tn=1024,
                     name="in_proj")
    dt_t = dt_proj_t(xn, wt, dt0, dims.ssd_heads).reshape(g, rh, t)

    a_cols = (dims.off_ain, dims.off_abg, dims.off_acg)
    zeros_a = jnp.zeros((dims.batch, SUBLANES, dims.a_width), F32)
    va_p, st_a_p = causal_conv(proj, zeros_a, a_conv_w[0], row0=0, nseq=dims.batch, length=dims.seq,
                               cols=dims.a_width, col_offs=a_cols, gated=True, out_dtype=BF16)
    prev_a = _pad_prev(state_conv_a[0], dims.a_conv).reshape(dims.dec_batch * SUBLANES, dims.a_width)
    va_s, u_s = short_seq_conv(proj, prev_a, a_conv_w[0], row0=tp, nseq=dims.dec_batch, seq=dims.dec_seq,
                               cols=dims.a_width, col_offs=a_cols, gated=True, out_dtype=BF16)
    na = dims.a_conv - 1
    conv_a_p = st_a_p[None, :, SUBLANES - na:, :]
    conv_a_s = u_s.reshape(dims.dec_batch, dims.dec_seq, dims.a_width)[None, :, dims.dec_seq - na:, :]

    zeros_b = jnp.zeros((dims.batch, SUBLANES, dims.ssd_xbc), F32)
    xbc_p = causal_conv(proj, zeros_b, ssd_conv_w[0], row0=0, nseq=dims.batch, length=dims.seq,
                        cols=dims.ssd_xbc, col_offs=(dims.off_xbc,), gated=False, bias=ssd_conv_b[0])
    prev_b = _pad_prev(state_ssd_conv[0], dims.ssd_conv).reshape(dims.dec_batch * SUBLANES, dims.ssd_xbc)
    xbc_s = short_seq_conv(proj, prev_b, ssd_conv_w[0], row0=tp, nseq=dims.dec_batch, seq=dims.dec_seq,
                           cols=dims.ssd_xbc, col_offs=(dims.off_xbc,), gated=False, bias=ssd_conv_b[0])
    nb = dims.ssd_conv - 1
    x0, x1c = dims.off_xbc, dims.off_xbc + dims.ssd_xbc
    ssd_conv_p = jnp.stack([lax.slice(proj, ((b + 1) * dims.seq - nb, x0), ((b + 1) * dims.seq, x1c))
                            for b in range(dims.batch)])[None]
    ssd_conv_s = lax.slice(proj.reshape(t // dims.dec_seq, dims.dec_seq, dims.proj_width),
                           (tp // dims.dec_seq, dims.dec_seq - nb, x0),
                           (t // dims.dec_seq, dims.dec_seq, x1c))[None]

    bias_g = ssd_dt_bias[0].reshape(g, rh, 1)
    alog_g = ssd_a_log[0].reshape(g, rh, 1)
    d_exp = jnp.repeat(ssd_d[0], p).reshape(1, dims.ssd_inner)
    norm_w = ssd_norm[0].reshape(1, dims.ssd_inner)
    yb_p, h_p = ssd_block(dims, xbc_p, proj, dt_t, bias_g, alog_g, d_exp, norm_w, row0=0, nrows=tp, chained=True)
    yb_s, h_s = ssd_block(dims, xbc_s, proj, dt_t, bias_g, alog_g, d_exp, norm_w, row0=tp, nrows=ts,
                          chained=False, h0=state_ssd[0])

    oc_p = cross_attention(dims, proj, k_p, v_p, row0=0, nseq=dims.batch, length=dims.seq)
    oc_s = cross_attention(dims, proj, cache_mem_k[0], cache_mem_v[0], row0=tp, nseq=dims.dec_batch,
                           length=dims.dec_seq)

    mix = merge_branches(dims, (va_p, yb_p, oc_p), (va_s, yb_s, oc_s), bf(a_out[0]), bf(ssd_out[0]),
                         bf(xatt_out[0]), proj)
    x1 = matmul(mix, bf(w_o[0]), F32, residual_parts=(xp2, xs2), name="out_proj")

    xn2 = rmsnorm(x1, norm_ffn[0], BF16)
    qp = matmul(xn2, bf(peer_wq[0]), F32, name="peer_query")
    c1, e1, r2, e2 = peer_select(dims, qp, peer_subkeys[0])
    ffn = peer_mix(dims, xn2, cast_layer(peer_u), cast_layer(peer_v), c1, e1, r2, e2)
    y_prompt = rmsnorm(x1, norm_final, F32, residual=ffn, row0=0, nrows=tp).reshape(dims.batch, dims.seq, d)
    y_sample = rmsnorm(x1, norm_final, F32, residual=ffn, row0=tp, nrows=ts).reshape(
        dims.dec_batch, dims.dec_seq, d)
    return (y_prompt, y_sample, mem_k_p, mem_v_p, conv_a_p, ssd_conv_p, h_p[None],
            conv_a_s, ssd_conv_s, h_s[None])


def kernel(x_prompt, x_sample, mem_prompt, cache_mem_k, cache_mem_v, state_conv_a, state_ssd_conv, state_ssd, norm_mix, norm_mem, norm_ffn, norm_final, w_in, a_conv_w, a_out, ssd_conv_w, ssd_conv_b, ssd_dt_bias, ssd_a_log, ssd_d, ssd_norm, ssd_out, w_mem_k, w_mem_v, xatt_out, w_o, peer_wq, peer_subkeys, peer_u, peer_v):
    return forward(FULL, x_prompt, x_sample, mem_prompt, cache_mem_k, cache_mem_v, state_conv_a,
                   state_ssd_conv, state_ssd, norm_mix, norm_mem, norm_ffn, norm_final, w_in,
                   a_conv_w, a_out, ssd_conv_w, ssd_conv_b, ssd_dt_bias, ssd_a_log, ssd_d, ssd_norm,
                   ssd_out, w_mem_k, w_mem_v, xatt_out, w_o, peer_wq, peer_subkeys, peer_u, peer_v)
```

```python
import dataclasses
import functools

import jax
import jax.numpy as jnp
import numpy as np
from jax import lax
from jax.experimental import pallas as pl
from jax.experimental.pallas import tpu as pltpu

F32 = jnp.float32
BF16 = jnp.bfloat16
EPS = 1e-6
HIGHEST = lax.Precision.HIGHEST
NEG_INF = float("-inf")

LANES = 128
SUBLANES = 8
VMEM_LIMIT_BYTES = 56 * 1024 * 1024


@dataclasses.dataclass(frozen=True)
class Dims:
    d_model: int = 4096
    batch: int = 4
    seq: int = 2048
    dec_batch: int = 128
    dec_seq: int = 8
    a_width: int = 2048
    a_conv: int = 3
    ssd_inner: int = 4096
    ssd_head_dim: int = 64
    ssd_groups: int = 8
    ssd_state: int = 128
    ssd_conv: int = 4
    ssd_chunk: int = 128
    n_mem: int = 256
    xatt_heads: int = 4
    xatt_head_dim: int = 512
    peer_heads: int = 8
    peer_keys: int = 128
    peer_topk: int = 16
    peer_qdim: int = 256

    @property
    def ssd_heads(self):
        return self.ssd_inner // self.ssd_head_dim

    @property
    def heads_per_group(self):
        return self.ssd_heads // self.ssd_groups

    @property
    def group_width(self):
        return self.ssd_inner // self.ssd_groups

    @property
    def ssd_xbc(self):
        return self.ssd_inner + 2 * self.ssd_groups * self.ssd_state

    @property
    def xatt_width(self):
        return self.xatt_heads * self.xatt_head_dim

    @property
    def n_experts(self):
        return self.peer_keys * self.peer_keys

    @property
    def t_prompt(self):
        return self.batch * self.seq

    @property
    def t_sample(self):
        return self.dec_batch * self.dec_seq

    @property
    def tokens(self):
        return self.t_prompt + self.t_sample

    @property
    def off_ain(self):
        return 0

    @property
    def off_abg(self):
        return self.a_width

    @property
    def off_acg(self):
        return 2 * self.a_width

    @property
    def off_z(self):
        return 3 * self.a_width

    @property
    def off_xbc(self):
        return self.off_z + self.ssd_inner

    @property
    def off_q(self):
        return self.off_xbc + self.ssd_xbc

    @property
    def off_gates(self):
        return self.off_q + self.xatt_width

    @property
    def proj_width(self):
        return self.off_gates + 3 * self.d_model


FULL = Dims()


def _params(*sem):
    return pltpu.CompilerParams(dimension_semantics=sem, vmem_limit_bytes=VMEM_LIMIT_BYTES)


def _pick(n, pref):
    t = min(n, pref)
    while n % t:
        t //= 2
    return t


def _rmsnorm_rows(x, g_ref, o_ref):
    inv = lax.rsqrt(jnp.mean(x * x, axis=-1, keepdims=True) + EPS)
    o_ref[...] = ((x * inv) * g_ref[...]).astype(o_ref.dtype)


def _rmsnorm_kernel(x_ref, g_ref, o_ref):
    _rmsnorm_rows(x_ref[...], g_ref, o_ref)


def _add_rmsnorm_kernel(x_ref, r_ref, g_ref, o_ref):
    _rmsnorm_rows(x_ref[...] + r_ref[...], g_ref, o_ref)


def _rmsnorm_parts_kernel(xp_ref, xs_ref, g_ref, o_ref, *, npb):
    i = pl.program_id(0)

    @pl.when(i < npb)
    def _():
        _rmsnorm_rows(xp_ref[...], g_ref, o_ref)

    @pl.when(i >= npb)
    def _():
        _rmsnorm_rows(xs_ref[...], g_ref, o_ref)


def rmsnorm(x, g, out_dtype, residual=None, row0=0, nrows=None, tm=256):
    m, d = x.shape
    nrows = m - row0 if nrows is None else nrows
    tm = _pick(nrows, tm)
    rb0 = row0 // tm
    assert row0 % tm == 0
    row = pl.BlockSpec((tm, d), lambda i: (rb0 + i, 0))
    gspec = pl.BlockSpec((1, d), lambda i: (0, 0))
    args = (x,) if residual is None else (x, residual)
    return pl.pallas_call(
        _rmsnorm_kernel if residual is None else _add_rmsnorm_kernel,
        grid=(nrows // tm,),
        in_specs=[row] * len(args) + [gspec],
        out_specs=pl.BlockSpec((tm, d), lambda i: (i, 0)),
        out_shape=jax.ShapeDtypeStruct((nrows, d), out_dtype),
        compiler_params=_params("parallel"),
        name="rmsnorm" if residual is None else "add_rmsnorm",
    )(*args, g.reshape(1, d))


def rmsnorm_parts(xp, xs, g, out_dtype, tm=256):
    d = xp.shape[1]
    tm = _pick(xs.shape[0], _pick(xp.shape[0], tm))
    npb, nsb = xp.shape[0] // tm, xs.shape[0] // tm
    return pl.pallas_call(
        functools.partial(_rmsnorm_parts_kernel, npb=npb),
        grid=(npb + nsb,),
        in_specs=[pl.BlockSpec((tm, d), lambda i: (jnp.minimum(i, npb - 1), 0)),
                  pl.BlockSpec((tm, d), lambda i: (jnp.maximum(i - npb, 0), 0)),
                  pl.BlockSpec((1, d), lambda i: (0, 0))],
        out_specs=pl.BlockSpec((tm, d), lambda i: (i, 0)),
        out_shape=jax.ShapeDtypeStruct((xp.shape[0] + xs.shape[0], d), out_dtype),
        compiler_params=_params("arbitrary"),
        name="rmsnorm_parts",
    )(xp, xs, g.reshape(1, d))


def _cast_kernel(x_ref, o_ref):
    o_ref[...] = x_ref[0].astype(o_ref.dtype)


def cast_layer(w, dtype=BF16, tr=512):
    _, r, c = w.shape
    tr = _pick(r, tr)
    return pl.pallas_call(
        _cast_kernel,
        grid=(r // tr,),
        in_specs=[pl.BlockSpec((1, tr, c), lambda i: (0, i, 0))],
        out_specs=pl.BlockSpec((tr, c), lambda i: (i, 0)),
        out_shape=jax.ShapeDtypeStruct((r, c), dtype),
        compiler_params=_params("parallel"),
        name="cast_layer",
    )(w)


def _mm_kernel(*refs, npb, layer_weight):
    a_ref, b_ref = refs[:2]
    refs = refs[2:]
    if npb is not None:
        rp_ref, rs_ref = refs[:2]
        refs = refs[2:]
    o_ref = refs[0]
    i = pl.program_id(1)
    if layer_weight:
        w_scr = refs[1]

        @pl.when(i == 0)
        def _():
            w_scr[...] = b_ref[0].astype(BF16)

        w = w_scr[...]
    else:
        w = b_ref[...]
    acc = jnp.dot(a_ref[...], w, preferred_element_type=F32)
    if npb is None:
        o_ref[...] = acc.astype(o_ref.dtype)
    else:
        @pl.when(i < npb)
        def _():
            o_ref[...] = (rp_ref[...] + acc).astype(o_ref.dtype)

        @pl.when(i >= npb)
        def _():
            o_ref[...] = (rs_ref[...] + acc).astype(o_ref.dtype)


def matmul(a, b, out_dtype, residual_parts=None, tm=512, tn=1024, name="matmul"):
    m, k = a.shape
    layer_weight = b.ndim == 3
    n = b.shape[-1]
    tm, tn = _pick(m, tm), _pick(n, tn)
    npb = None
    extra_specs, extra_args = [], []
    if residual_parts is not None:
        rp, rs = residual_parts
        tm = _pick(rs.shape[0], _pick(rp.shape[0], tm))
        npb = rp.shape[0] // tm
        extra_specs = [pl.BlockSpec((tm, tn), lambda j, i: (jnp.minimum(i, npb - 1), j)),
                       pl.BlockSpec((tm, tn), lambda j, i: (jnp.maximum(i - npb, 0), j))]
        extra_args = [rp, rs]
    if layer_weight:
        b_spec = pl.BlockSpec((1, k, tn), lambda j, i: (0, 0, j))
        scratch = [pltpu.VMEM((k, tn), BF16)]
    else:
        b_spec = pl.BlockSpec((k, tn), lambda j, i: (0, j))
        scratch = []
    sequential = layer_weight or npb is not None
    return pl.pallas_call(
        functools.partial(_mm_kernel, npb=npb, layer_weight=layer_weight),
        grid=(n // tn, m // tm),
        in_specs=[pl.BlockSpec((tm, k), lambda j, i: (i, 0)), b_spec] + extra_specs,
        out_specs=pl.BlockSpec((tm, tn), lambda j, i: (i, j)),
        out_shape=jax.ShapeDtypeStruct((m, n), out_dtype),
        scratch_shapes=scratch,
        compiler_params=_params(*(("arbitrary", "arbitrary") if sequential else ("parallel", "parallel"))),
        name=name,
    )(a, b, *extra_args)


def _mm_wt_kernel(a_ref, wt_ref, o_ref, w_scr):
    @pl.when(pl.program_id(1) == 0)
    def _():
        w_scr[...] = wt_ref[...].astype(BF16)

    o_ref[...] = lax.dot_general(a_ref[...], w_scr[...], (((1,), (1,)), ((), ())),
                                 preferred_element_type=F32).astype(o_ref.dtype)


def matmul_wt(a, wt, segments, out_dtype=F32, tm=1024, tn=512, name="matmul_wt"):
    m, k = a.shape
    tm = _pick(m, tm)
    for start, length in segments:
        tn = _pick(length, tn)
        assert start % SUBLANES == 0
    assert all(length % tn == 0 for _, length in segments)
    starts = np.concatenate([np.arange(s, s + l, tn) for s, l in segments]).astype(np.int32)
    nblk = len(starts)
    bounds = np.cumsum([l // tn for _, l in segments])[:-1]
    shifts = [segments[i + 1][0] - (segments[i][0] + segments[i][1]) for i in range(len(segments) - 1)]

    def row_start(j):
        r = segments[0][0] + j * tn
        for b, sh in zip(bounds, shifts):
            r = r + jnp.where(j >= b, sh, 0)
        return pl.multiple_of(r, SUBLANES)

    return pl.pallas_call(
        _mm_wt_kernel,
        grid=(nblk, m // tm),
        in_specs=[pl.BlockSpec((tm, k), lambda j, i: (i, 0)),
                  pl.BlockSpec((pl.Element(tn), pl.Element(k)), lambda j, i: (row_start(j), 0))],
        out_specs=pl.BlockSpec((tm, tn), lambda j, i: (i, j)),
        out_shape=jax.ShapeDtypeStruct((m, nblk * tn), out_dtype),
        scratch_shapes=[pltpu.VMEM((tn, k), BF16)],
        compiler_params=_params("arbitrary", "arbitrary"),
        name=name,
    )(a, wt)


def _dt_proj_kernel(a_ref, wt_ref, o_ref):
    o_ref[...] = lax.dot_general(wt_ref[...].astype(BF16), a_ref[...], (((1,), (1,)), ((), ())),
                                 preferred_element_type=F32)


def dt_proj_t(a, wt, start, nrows, tm=1024):
    m, k = a.shape
    tm = _pick(m, tm)
    assert start % SUBLANES == 0
    return pl.pallas_call(
        _dt_proj_kernel,
        grid=(m // tm,),
        in_specs=[pl.BlockSpec((tm, k), lambda i: (i, 0)),
                  pl.BlockSpec((pl.Element(nrows), pl.Element(k)), lambda i: (start, 0))],
        out_specs=pl.BlockSpec((nrows, tm), lambda i: (0, i)),
        out_shape=jax.ShapeDtypeStruct((nrows, m), F32),
        compiler_params=_params("parallel"),
        name="dt_proj",
    )(a, wt)


def _conv_kernel(*refs, width, tl, gated):
    if gated:
        in_ref, bg_ref, cg_ref, prev_ref, w_ref, o_ref, st_ref, scr = refs
    else:
        in_ref, prev_ref, w_ref, bias_ref, o_ref, scr = refs
    lt = pl.program_id(2)

    @pl.when(lt == 0)
    def _():
        scr[0:SUBLANES, :] = prev_ref[0]

    u = cg_ref[...] * in_ref[...] if gated else in_ref[...]
    scr[SUBLANES:SUBLANES + tl, :] = u
    acc = w_ref[width - 1:width, :] * u
    for k in range(width - 1):
        lo = SUBLANES - (width - 1 - k)
        acc = acc + w_ref[k:k + 1, :] * scr[lo:lo + tl, :]
    tail = scr[tl:tl + SUBLANES, :]
    scr[0:SUBLANES, :] = tail
    if gated:
        o_ref[...] = (bg_ref[...] * acc).astype(o_ref.dtype)
        st_ref[0] = tail
    else:
        y = acc + bias_ref[...]
        o_ref[...] = (y * jax.nn.sigmoid(y)).astype(o_ref.dtype)


def causal_conv(proj, prev8, w, *, row0, nseq, length, cols, col_offs, gated, bias=None, out_dtype=F32,
                tl=1024, tc=512):
    width = w.shape[0]
    tl, tc = _pick(length, tl), _pick(cols, tc)
    while any(o % tc for o in col_offs):
        tc //= 2
    nl = length // tl
    rb0 = row0 // tl
    assert row0 % tl == 0 and all(o % tc == 0 for o in col_offs)

    def blk(off):
        return pl.BlockSpec((tl, tc), lambda s, c, l, off=off: (rb0 + s * nl + l, off // tc + c))

    prev_spec = pl.BlockSpec((1, SUBLANES, tc), lambda s, c, l: (s, 0, c))
    w_spec = pl.BlockSpec((width, tc), lambda s, c, l: (0, c))
    out_spec = pl.BlockSpec((tl, tc), lambda s, c, l: (s * nl + l, c))
    out_shape = jax.ShapeDtypeStruct((nseq * length, cols), out_dtype)
    if gated:
        in_specs = [blk(col_offs[0]), blk(col_offs[1]), blk(col_offs[2]), prev_spec, w_spec]
        args = (proj, proj, proj, prev8, w)
        out_specs = [out_spec, pl.BlockSpec((1, SUBLANES, tc), lambda s, c, l: (s, 0, c))]
        out_shape = [out_shape, jax.ShapeDtypeStruct((nseq, SUBLANES, cols), F32)]
    else:
        in_specs = [blk(col_offs[0]), prev_spec, w_spec, pl.BlockSpec((1, tc), lambda s, c, l: (0, c))]
        args = (proj, prev8, w, bias.reshape(1, cols))
        out_specs = out_spec
    return pl.pallas_call(
        functools.partial(_conv_kernel, width=width, tl=tl, gated=gated),
        grid=(nseq, cols // tc, nl),
        in_specs=in_specs,
        out_specs=out_specs,
        out_shape=out_shape,
        scratch_shapes=[pltpu.VMEM((tl + SUBLANES, tc), F32)],
        compiler_params=_params("parallel", "parallel", "arbitrary"),
        name="gated_conv" if gated else "ssd_conv",
    )(*args)


def _short_conv_kernel(*refs, width, seq, gated):
    if gated:
        in_ref, bg_ref, cg_ref, prev_ref, w_ref, o_ref, u_ref, scr_u, scr_p = refs
    else:
        in_ref, prev_ref, w_ref, bias_ref, o_ref, scr_u, scr_p = refs
    rows = in_ref.shape[0]
    u = cg_ref[...] * in_ref[...] if gated else in_ref[...]
    zeros = jnp.zeros((SUBLANES, u.shape[1]), F32)
    scr_u[0:SUBLANES, :] = zeros
    scr_u[SUBLANES:SUBLANES + rows, :] = u
    scr_p[0:rows, :] = prev_ref[...]
    scr_p[rows:rows + SUBLANES, :] = zeros
    pos = lax.broadcasted_iota(jnp.int32, u.shape, 0) % seq
    acc = w_ref[width - 1:width, :] * u
    for k in range(width - 1):
        shift = width - 1 - k
        lo = SUBLANES - shift
        operand = jnp.where(pos >= shift, scr_u[lo:lo + rows, :], scr_p[lo:lo + rows, :])
        acc = acc + w_ref[k:k + 1, :] * operand
    if gated:
        o_ref[...] = (bg_ref[...] * acc).astype(o_ref.dtype)
        u_ref[...] = u
    else:
        y = acc + bias_ref[...]
        o_ref[...] = (y * jax.nn.sigmoid(y)).astype(o_ref.dtype)


def short_seq_conv(proj, prev8, w, *, row0, nseq, seq, cols, col_offs, gated, bias=None, out_dtype=F32,
                   rows=128, tc=2048):
    width = w.shape[0]
    assert seq == SUBLANES and width - 1 <= seq
    total = nseq * seq
    rows, tc = _pick(total, rows), _pick(cols, tc)
    while any(o % tc for o in col_offs):
        tc //= 2
    rb0 = row0 // rows
    assert row0 % rows == 0

    def blk(off):
        return pl.BlockSpec((rows, tc), lambda r, c, off=off: (rb0 + r, off // tc + c))

    own = pl.BlockSpec((rows, tc), lambda r, c: (r, c))
    w_spec = pl.BlockSpec((width, tc), lambda r, c: (0, c))
    out_shape = jax.ShapeDtypeStruct((total, cols), out_dtype)
    if gated:
        in_specs = [blk(col_offs[0]), blk(col_offs[1]), blk(col_offs[2]), own, w_spec]
        args = (proj, proj, proj, prev8, w)
        out_specs = [own, own]
        out_shape = [out_shape, jax.ShapeDtypeStruct((total, cols), F32)]
    else:
        in_specs = [blk(col_offs[0]), own, w_spec, pl.BlockSpec((1, tc), lambda r, c: (0, c))]
        args = (proj, prev8, w, bias.reshape(1, cols))
        out_specs = own
    return pl.pallas_call(
        functools.partial(_short_conv_kernel, width=width, seq=seq, gated=gated),
        grid=(total // rows, cols // tc),
        in_specs=in_specs,
        out_specs=out_specs,
        out_shape=out_shape,
        scratch_shapes=[pltpu.VMEM((rows + SUBLANES, tc), F32), pltpu.VMEM((rows + SUBLANES, tc), F32)],
        compiler_params=_params("parallel", "parallel"),
        name="gated_conv_short" if gated else "ssd_conv_short",
    )(*args)


def _softplus(x):
    return jnp.maximum(x, 0.0) + jnp.log1p(jnp.exp(-jnp.abs(x)))


def _ssd_kernel(*refs, nseq, dims, chained, gpb):
    if chained:
        (xs_ref, b_ref, c_ref, z_ref, dt_ref, bias_ref, alog_ref, d_ref, nw_ref, sel_ref,
         y_ref, hout_ref, h_scr) = refs
    else:
        (xs_ref, b_ref, c_ref, z_ref, dt_ref, bias_ref, alog_ref, d_ref, nw_ref, sel_ref, h0_ref,
         y_ref, hout_ref) = refs
    q = dims.ssd_chunk
    r_heads = dims.heads_per_group
    p = dims.ssd_head_dim
    gw = dims.group_width
    n = dims.ssd_state
    seg = q // nseq

    if chained:
        @pl.when(pl.program_id(2) == 0)
        def _():
            h_scr[...] = jnp.zeros_like(h_scr)

    for gi in range(gpb):
        gcols = slice(gi * gw, (gi + 1) * gw)
        ncols = slice(gi * n, (gi + 1) * n)
        _ssd_group(
            xs_ref.at[:, gcols], b_ref.at[:, ncols], c_ref.at[:, ncols], z_ref.at[:, gcols],
            dt_ref.at[gi], bias_ref.at[gi], alog_ref.at[gi], d_ref.at[:, gcols], nw_ref.at[:, gcols], sel_ref,
            y_ref.at[:, gcols],
            hout_ref.at[:, gi * r_heads:(gi + 1) * r_heads],
            h_scr.at[gi * r_heads * p:(gi + 1) * r_heads * p, :] if chained else None,
            None if chained else h0_ref.at[:, gi * r_heads:(gi + 1) * r_heads],
            nseq=nseq, dims=dims, chained=chained)


def _ssd_group(xs_ref, b_ref, c_ref, z_ref, dt_ref, bias_ref, alog_ref, d_ref, nw_ref, sel_ref, y_ref,
               hout_ref, h_scr, h0_ref, *, nseq, dims, chained):
    q = dims.ssd_chunk
    r_heads = dims.heads_per_group
    p = dims.ssd_head_dim
    gw = dims.group_width
    seg = q // nseq
    row = lax.broadcasted_iota(jnp.int32, (q, q), 0)
    col = lax.broadcasted_iota(jnp.int32, (q, q), 1)
    same = (row // seg) == (col // seg)
    dt = _softplus(dt_ref[...] + bias_ref[...])
    adt = dt * (-jnp.exp(alog_ref[...]))
    cum_mask = jnp.where(same & (row <= col), 1.0, 0.0).astype(F32)
    acum_t = jnp.dot(adt, cum_mask, precision=HIGHEST, preferred_element_type=F32)
    if nseq == 1:
        atot_t = jnp.broadcast_to(acum_t[:, q - 1:q], (r_heads, q))
    else:
        atot_t = jnp.dot(adt, jnp.where(same, 1.0, 0.0).astype(F32), precision=HIGHEST,
                         preferred_element_type=F32)
    stack = jnp.concatenate(
        [acum_t, dt, dt * jnp.exp(atot_t - acum_t), jnp.exp(acum_t),
         jnp.zeros((q - 4 * r_heads, q), F32)], axis=0)
    cols_form = stack.T
    sel = sel_ref[...]
    hi = cols_form.astype(BF16)
    rest = cols_form - hi.astype(F32)
    mid = rest.astype(BF16)
    low = (rest - mid.astype(F32)).astype(BF16)
    expand = (jnp.dot(hi, sel, preferred_element_type=F32) + jnp.dot(mid, sel, preferred_element_type=F32)
              + jnp.dot(low, sel, preferred_element_type=F32))
    e_dt, e_st, e_ac = expand[:, :gw], expand[:, gw:2 * gw], expand[:, 2 * gw:]

    x = xs_ref[...]
    bb = b_ref[...].astype(BF16)
    cb_ = c_ref[...].astype(BF16)
    cb = lax.dot_general(cb_, bb, (((1,), (1,)), ((), ())), preferred_element_type=F32)
    x_dt = x * e_dt
    x_dt_b = x_dt.astype(BF16)
    causal = same & (row >= col)
    lane = lax.broadcasted_iota(jnp.int32, (q, LANES), 1)
    heads_per_tile = LANES // p

    y_tiles = []
    for tile in range(gw // LANES):
        xt = x_dt_b[:, tile * LANES:(tile + 1) * LANES]
        acc = None
        for k in range(heads_per_tile):
            r = tile * heads_per_tile + k
            segm = cols_form[:, r:r + 1] - acum_t[r:r + 1, :]
            decay = jnp.exp(jnp.where(causal, segm, NEG_INF))
            m_r = (cb * decay).astype(BF16)
            x_r = jnp.where((lane >= k * p) & (lane < (k + 1) * p), xt, jnp.zeros_like(xt))
            part = jnp.dot(m_r, x_r, preferred_element_type=F32)
            acc = part if acc is None else acc + part
        y_tiles.append(acc)
    y = jnp.concatenate(y_tiles, axis=1)

    x_st = x * e_st
    eac_t = jnp.exp(atot_t)

    if chained:
        h = h_scr[...]
        y_off = lax.dot_general(cb_, h.astype(BF16), (((1,), (1,)), ((), ())),
                                preferred_element_type=F32)
        y = y + y_off * e_ac
        s_new = lax.dot_general(x_st.astype(BF16), bb, (((0,), (0,)), ((), ())),
                                preferred_element_type=F32)
        scale = jnp.broadcast_to(eac_t[:, q - 1:q], (r_heads, LANES))
        for r in range(r_heads):
            rows = slice(r * p, (r + 1) * p)
            h_scr[rows, :] = h[rows, :] * jnp.broadcast_to(scale[r:r + 1, :], (p, LANES)) + s_new[rows, :]
        for r in range(r_heads):
            hout_ref[0, r] = h_scr[r * p:(r + 1) * p, :]
    else:
        x_st_t = x_st.T
        rows_q = lax.broadcasted_iota(jnp.int32, (q, gw), 0)
        lanes_q = lax.broadcasted_iota(jnp.int32, (gw, q), 1)
        y_off = jnp.zeros((q, gw), F32)
        for s in range(nseq):
            h_s = h0_ref[s].reshape(r_heads * p, dims.ssd_state)
            y_s = lax.dot_general(cb_, h_s.astype(BF16), (((1,), (1,)), ((), ())),
                                  preferred_element_type=F32)
            y_off = jnp.where(rows_q // seg == s, y_s, y_off)
            xs_s = jnp.where(lanes_q // seg == s, x_st_t, 0.0).astype(BF16)
            s_new = jnp.dot(xs_s, bb, preferred_element_type=F32)
            scale = jnp.broadcast_to(eac_t[:, s * seg:s * seg + 1], (r_heads, LANES))
            for r in range(r_heads):
                rows = slice(r * p, (r + 1) * p)
                hout_ref[s, r] = (h_s[rows, :] * jnp.broadcast_to(scale[r:r + 1, :], (p, LANES))
                                  + s_new[rows, :])
        y = y + y_off * e_ac

    y = y + d_ref[...] * x
    z = z_ref[...]
    y = y * (z * jax.nn.sigmoid(z))
    inv = lax.rsqrt(jnp.mean(y * y, axis=-1, keepdims=True) + EPS)
    y_ref[...] = ((y * inv) * nw_ref[...]).astype(y_ref.dtype)


def _ssd_selector(dims):
    r_heads, p, gw = dims.heads_per_group, dims.ssd_head_dim, dims.group_width
    sel = np.zeros((dims.ssd_chunk, 3 * gw), np.float32)
    for part in range(3):
        for r in range(r_heads):
            sel[(part + 1) * r_heads + r, part * gw + r * p:part * gw + (r + 1) * p] = 1.0
    return jnp.asarray(sel, BF16)


def ssd_block(dims, xbc, proj, dt_t, bias_g, alog_g, d_exp, norm_w, *, row0, nrows, chained, h0=None):
    q = dims.ssd_chunk
    g = dims.ssd_groups
    gw = dims.group_width
    n = dims.ssd_state
    r_heads, p = dims.heads_per_group, dims.ssd_head_dim
    assert LANES % p == 0 and q == LANES and 4 * r_heads <= q
    sel = _ssd_selector(dims)
    rb0 = row0 // q
    zoff = dims.off_z // gw
    boff = dims.ssd_inner // n
    coff = boff + g

    gpb = 1
    if chained:
        for cand_gpb in (4, 2):
            if all(v % cand_gpb == 0 for v in (g, zoff, boff)):
                gpb = cand_gpb
                break
    gblocks = g // gpb
    if chained:
        nb, nc = dims.batch, dims.seq // q
        grid = (nb, gblocks, nc)
        rowblk = lambda b, gi, c: b * nc + c
        sems = ("parallel", "parallel", "arbitrary")
        nseq = 1
    else:
        nseq = q // dims.dec_seq
        nb = dims.dec_batch // nseq
        grid = (nb, gblocks)
        rowblk = lambda b, gi: b
        sems = ("parallel", "parallel")

    def spec(shape, fn):
        return pl.BlockSpec(shape, fn)

    if chained:
        ix = lambda f: (lambda b, gi, c: f(rowblk(b, gi, c), gi))
    else:
        ix = lambda f: (lambda b, gi: f(rowblk(b, gi), gi))

    in_specs = [
        spec((q, gpb * gw), ix(lambda rb, gi: (rb, gi))),
        spec((q, gpb * n), ix(lambda rb, gi: (rb, boff // gpb + gi))),
        spec((q, gpb * n), ix(lambda rb, gi: (rb, coff // gpb + gi))),
        spec((q, gpb * gw), ix(lambda rb, gi: (rb0 + rb, zoff // gpb + gi))),
        spec((gpb, r_heads, q), ix(lambda rb, gi: (gi, 0, rb0 + rb))),
        spec((gpb, r_heads, 1), ix(lambda rb, gi: (gi, 0, 0))),
        spec((gpb, r_heads, 1), ix(lambda rb, gi: (gi, 0, 0))),
        spec((1, gpb * gw), ix(lambda rb, gi: (0, gi))),
        spec((1, gpb * gw), ix(lambda rb, gi: (0, gi))),
        spec((q, 3 * gw), ix(lambda rb, gi: (0, 0))),
    ]
    args = [xbc, xbc, xbc, proj, dt_t, bias_g, alog_g, d_exp, norm_w, sel]
    y_spec = spec((q, gpb * gw), ix(lambda rb, gi: (rb, gi)))
    if chained:
        h_spec = pl.BlockSpec((1, gpb * r_heads, p, n), lambda b, gi, c: (b, gi, 0, 0))
        h_shape = jax.ShapeDtypeStruct((nb, dims.ssd_heads, p, n), F32)
        scratch = [pltpu.VMEM((gpb * r_heads * p, n), F32)]
    else:
        in_specs.append(pl.BlockSpec((nseq, gpb * r_heads, p, n), lambda b, gi: (b, gi, 0, 0)))
        args.append(h0)
        h_spec = pl.BlockSpec((nseq, gpb * r_heads, p, n), lambda b, gi: (b, gi, 0, 0))
        h_shape = jax.ShapeDtypeStruct((dims.dec_batch, dims.ssd_heads, p, n), F32)
        scratch = []
    return pl.pallas_call(
        functools.partial(_ssd_kernel, nseq=nseq, dims=dims, chained=chained, gpb=gpb),
        grid=grid,
        in_specs=in_specs,
        out_specs=[y_spec, h_spec],
        out_shape=[jax.ShapeDtypeStruct((nrows, dims.ssd_inner), BF16), h_shape],
        scratch_shapes=scratch,
        compiler_params=_params(*sems),
        name="ssd_prompt" if chained else "ssd_sample",
    )(*args)


def _attend(q, k, v, scale):
    s = lax.dot_general(q.astype(BF16), k.astype(BF16), (((1,), (1,)), ((), ())),
                        preferred_element_type=F32) * scale
    s = s - jnp.max(s, axis=-1, keepdims=True)
    e = jnp.exp(s)
    pr = (e / jnp.sum(e, axis=-1, keepdims=True)).astype(BF16)
    return jnp.dot(pr, v.astype(BF16), preferred_element_type=F32)


def _xattn_kernel(q_ref, k_ref, v_ref, o_ref, *, heads, head_dim):
    for h in range(heads):
        cs = slice(h * head_dim, (h + 1) * head_dim)
        o_ref[:, cs] = _attend(q_ref[:, cs], k_ref[0, :, cs], v_ref[0, :, cs], head_dim ** -0.5).astype(o_ref.dtype)


def _xattn_cached_kernel(q_ref, k_ref, v_ref, o_ref, *, heads, head_dim):
    tq = q_ref.shape[0]
    m = k_ref.shape[1]
    k = k_ref[0].reshape(m * heads, head_dim).astype(BF16)
    v = v_ref[0].reshape(m * heads, head_dim).astype(BF16)
    q = jnp.concatenate([q_ref[:, h * head_dim:(h + 1) * head_dim] for h in range(heads)], axis=0)
    s = lax.dot_general(q.astype(BF16), k, (((1,), (1,)), ((), ())), preferred_element_type=F32)
    s = s * (head_dim ** -0.5)
    row_head = lax.broadcasted_iota(jnp.int32, s.shape, 0) // tq
    col_head = lax.broadcasted_iota(jnp.int32, s.shape, 1) % heads
    s = jnp.where(row_head == col_head, s, NEG_INF)
    s = s - jnp.max(s, axis=-1, keepdims=True)
    e = jnp.exp(s)
    pr = (e / jnp.sum(e, axis=-1, keepdims=True)).astype(BF16)
    o = jnp.dot(pr, v, preferred_element_type=F32)
    for h in range(heads):
        o_ref[:, h * head_dim:(h + 1) * head_dim] = o[h * tq:(h + 1) * tq, :].astype(o_ref.dtype)


def cross_attention(dims, proj, k, v, *, row0, nseq, length, tq=512):
    w = dims.xatt_width
    tq = _pick(length, tq)
    nq = length // tq
    rb0 = row0 // tq
    assert row0 % tq == 0 and dims.off_q % w == 0
    qoff = dims.off_q // w
    hd, nh = dims.xatt_head_dim, dims.xatt_heads
    if k.ndim == 4:
        assert nq == 1 and tq % SUBLANES == 0
        kv_spec = pl.BlockSpec((1, dims.n_mem, nh, hd), lambda b: (b, 0, 0, 0))
        return pl.pallas_call(
            functools.partial(_xattn_cached_kernel, heads=nh, head_dim=hd),
            grid=(nseq,),
            in_specs=[pl.BlockSpec((tq, w), lambda b: (rb0 + b, qoff)), kv_spec, kv_spec],
            out_specs=pl.BlockSpec((tq, w), lambda b: (b, 0)),
            out_shape=jax.ShapeDtypeStruct((nseq * length, w), BF16),
            compiler_params=_params("parallel"),
            name="cross_attention_cached",
        )(proj, k, v)
    kv_spec = pl.BlockSpec((1, dims.n_mem, w), lambda b, i: (b, 0, 0))
    return pl.pallas_call(
        functools.partial(_xattn_kernel, heads=nh, head_dim=hd),
        grid=(nseq, nq),
        in_specs=[pl.BlockSpec((tq, w), lambda b, i: (rb0 + b * nq + i, qoff)), kv_spec, kv_spec],
        out_specs=pl.BlockSpec((tq, w), lambda b, i: (b * nq + i, 0)),
        out_shape=jax.ShapeDtypeStruct((nseq * length, w), BF16),
        compiler_params=_params("parallel", "parallel"),
        name="cross_attention",
    )(proj, k, v)


def _merge_kernel(vap_ref, ybp_ref, ocp_ref, vas_ref, ybs_ref, ocs_ref, wa_ref, wb_ref, wc_ref,
                  ga_ref, gb_ref, gc_ref, o_ref, *, npb):
    i = pl.program_id(1)

    def body(va_ref, yb_ref, oc_ref):
        ha = jnp.dot(va_ref[...], wa_ref[...], preferred_element_type=F32)
        hb = jnp.dot(yb_ref[...], wb_ref[...], preferred_element_type=F32)
        hc = jnp.dot(oc_ref[...], wc_ref[...], preferred_element_type=F32)
        mix = (jax.nn.sigmoid(ga_ref[...]) * ha + jax.nn.sigmoid(gb_ref[...]) * hb
               + jax.nn.sigmoid(gc_ref[...]) * hc)
        o_ref[...] = mix.astype(o_ref.dtype)

    @pl.when(i < npb)
    def _():
        body(vap_ref, ybp_ref, ocp_ref)

    @pl.when(i >= npb)
    def _():
        body(vas_ref, ybs_ref, ocs_ref)


def merge_branches(dims, prompt, sample, wa, wb, wc, proj, tm=512, tn=512):
    d = dims.d_model
    mp, ms = prompt[0].shape[0], sample[0].shape[0]
    tm, tn = _pick(ms, _pick(mp, tm)), _pick(d, tn)
    npb = mp // tm
    g0 = dims.off_gates // tn
    gd = d // tn
    assert dims.off_gates % tn == 0

    def lhs_p(a):
        return pl.BlockSpec((tm, a.shape[1]), lambda j, i: (jnp.minimum(i, npb - 1), 0))

    def lhs_s(a):
        return pl.BlockSpec((tm, a.shape[1]), lambda j, i: (jnp.maximum(i - npb, 0), 0))

    def rhs(width):
        return pl.BlockSpec((width, tn), lambda j, i: (0, j), pipeline_mode=pl.Buffered(1))

    def gate(k):
        return pl.BlockSpec((tm, tn), lambda j, i, k=k: (i, g0 + k * gd + j))

    return pl.pallas_call(
        functools.partial(_merge_kernel, npb=npb),
        grid=(d // tn, (mp + ms) // tm),
        in_specs=[lhs_p(a) for a in prompt] + [lhs_s(a) for a in sample]
        + [rhs(wa.shape[0]), rhs(wb.shape[0]), rhs(wc.shape[0]), gate(0), gate(1), gate(2)],
        out_specs=pl.BlockSpec((tm, tn), lambda j, i: (i, j)),
        out_shape=jax.ShapeDtypeStruct((mp + ms, d), BF16),
        compiler_params=_params("arbitrary", "arbitrary"),
        name="merge_branches",
    )(*prompt, *sample, wa, wb, wc, proj, proj, proj)


def _peer_select_kernel(q_ref, keys_ref, c1_ref, e1_ref, r2_ref, e2_ref, scores, work, tops, cand, ranks, *,
                        dims):
    heads, nk, topk = dims.peer_heads, dims.peer_keys, dims.peer_topk
    half = dims.peer_qdim // 2
    tb = q_ref.shape[0]
    rank = lax.broadcasted_iota(jnp.int32, (topk, tb), 0)

    for k in range(2 * heads):
        qh = q_ref[:, k * half:(k + 1) * half].astype(BF16)
        s = lax.dot_general(keys_ref[k].astype(BF16), qh, (((1,), (1,)), ((), ())),
                            preferred_element_type=F32)
        scores[k] = s
        work[k] = s
        tops[k] = jnp.full((topk, tb), NEG_INF, F32)

    for h in range(heads):
        ranks[h] = jnp.full((nk, tb), float(topk), F32)

    def extract(r, carry):
        rf = jnp.asarray(r, F32)
        for k in range(2 * heads):
            s = work[k]
            m = jnp.max(s, axis=0, keepdims=True)
            hit = s == m
            tops[k] = jnp.where(rank == r, m, tops[k])
            work[k] = jnp.where(hit, NEG_INF, s)
            if k % 2:
                ranks[k // 2] = jnp.where(hit, rf, ranks[k // 2])
        return carry
    lax.fori_loop(0, topk, extract, 0)

    for h in range(heads):
        a, b = tops[2 * h], tops[2 * h + 1]
        cand[h, 0:topk, :] = a[0:1, :] + b
        for pi in range(1, topk):
            lo = topk + (pi - 1) * SUBLANES
            cand[h, lo:lo + SUBLANES, :] = a[pi:pi + 1, :] + b[0:SUBLANES, :]

    def threshold(r, taus):
        new = []
        for h in range(heads):
            c = cand[h]
            m = jnp.max(c, axis=0, keepdims=True)
            cand[h] = jnp.where(c == m, NEG_INF, c)
            new.append(m)
        return tuple(new)
    taus = lax.fori_loop(0, topk, threshold, tuple(jnp.zeros((1, tb), F32) for _ in range(heads)))

    for h in range(heads):
        a, b = tops[2 * h], tops[2 * h + 1]
        tau = taus[h]
        z = None
        for pi in range(topk):
            rows = topk if pi == 0 else SUBLANES
            c = a[pi:pi + 1, :] + b[0:rows, :]
            part = jnp.sum(jnp.where(c >= tau, jnp.exp(c - (a[0:1, :] + b[0:1, :])), 0.0), axis=0, keepdims=True)
            z = part if z is None else z + part
        s1, s2 = scores[2 * h], scores[2 * h + 1]
        count = jnp.zeros_like(s1)
        for qi in range(topk):
            reach = a + b[qi:qi + 1, :] >= tau
            alpha = jnp.min(jnp.where(reach, a, float("inf")), axis=0, keepdims=True)
            count = count + jnp.where(s1 >= alpha, 1.0, 0.0)
        c1_ref[h] = count
        r2_ref[h] = ranks[h].astype(r2_ref.dtype)
        e1_ref[h] = jnp.exp(s1 - a[0:1, :]) * (0.5 / z)
        e2_ref[h] = jnp.exp(s2 - b[0:1, :]).astype(e2_ref.dtype)


def peer_select(dims, q, subkeys, tb=128):
    t = q.shape[0]
    heads, nk = dims.peer_heads, dims.peer_keys
    half = dims.peer_qdim // 2
    assert dims.peer_topk >= SUBLANES and heads == SUBLANES
    tab = jax.ShapeDtypeStruct((heads, nk, t), F32)
    tab16 = jax.ShapeDtypeStruct((heads, nk, t), BF16)
    tab_spec = pl.BlockSpec((heads, nk, tb), lambda i: (0, 0, i))
    return pl.pallas_call(
        functools.partial(_peer_select_kernel, dims=dims),
        grid=(t // tb,),
        in_specs=[pl.BlockSpec((tb, q.shape[1]), lambda i: (i, 0)),
                  pl.BlockSpec((2 * heads, nk, half), lambda i: (0, 0, 0))],
        out_specs=[tab_spec, tab_spec, tab_spec, tab_spec],
        out_shape=[tab, tab, tab16, tab16],
        scratch_shapes=[pltpu.VMEM((2 * heads, nk, tb), F32),
                        pltpu.VMEM((2 * heads, nk, tb), F32),
                        pltpu.VMEM((2 * heads, dims.peer_topk, tb), F32),
                        pltpu.VMEM((heads, dims.peer_topk + (dims.peer_topk - 1) * SUBLANES, tb), F32),
                        pltpu.VMEM((heads, nk, tb), F32)],
        compiler_params=_params("parallel"),
        name="peer_select",
    )(q, subkeys.reshape(2 * heads, nk, half))


def _peer_mix_kernel(x_ref, u_ref, v_ref, c1_ref, e1_ref, r2_ref, e2_ref, o_ref, gw_scr, w_scr, *, dims, sub):
    heads, nk = dims.peer_heads, dims.peer_keys
    eb = pl.program_id(1)
    te = u_ref.shape[0]

    @pl.when(eb == 0)
    def _():
        o_ref[...] = jnp.zeros_like(o_ref)

    tb = x_ref.shape[0]
    tr, tl = 2 * SUBLANES, LANES
    zero = jnp.zeros((tr, tl), BF16)
    for ii in range(te // nk):
        for c0 in range(0, tb, tl):
            cs = slice(c0, c0 + tl)
            acc = [None] * (nk // tr)
            for h in range(heads):
                count = jnp.broadcast_to(c1_ref[h, ii, :, cs], (tr, tl)).astype(BF16)
                e1 = jnp.broadcast_to(e1_ref[h, ii, :, cs], (tr, tl)).astype(BF16)
                for s in range(nk // tr):
                    rs = slice(s * tr, (s + 1) * tr)
                    wh = jnp.where(r2_ref[h, rs, cs] < count, e1 * e2_ref[h, rs, cs], zero)
                    acc[s] = wh if acc[s] is None else acc[s] + wh
            for s in range(nk // tr):
                w_scr[ii * nk + s * tr:ii * nk + (s + 1) * tr, cs] = acc[s]
    x = x_ref[...]
    for sb in range(te // sub):
        rows = slice(sb * sub, (sb + 1) * sub)
        hh = lax.dot_general(u_ref[rows, :], x, (((1,), (1,)), ((), ())), preferred_element_type=F32)
        gelu2 = hh * (1.0 + lax.erf(hh * (2.0 ** -0.5)))
        gw_scr[rows, :] = gelu2.astype(BF16) * w_scr[rows, :]
        o_ref[...] += lax.dot_general(gw_scr[rows, :], v_ref[rows, :], (((0,), (0,)), ((), ())),
                                      preferred_element_type=F32)


def peer_mix(dims, xn, u, v, c1, e1, r2, e2, tb=512, te=512, sub=512):
    t, d = xn.shape
    heads, nk = dims.peer_heads, dims.peer_keys
    tb, te = _pick(t, tb), _pick(dims.n_experts, te)
    sub = min(sub, te)
    assert te % sub == 0 and sub % nk == 0
    ni = te // nk
    row_tab = lambda a: a.reshape(heads, nk, 1, t)
    row_spec = pl.BlockSpec((heads, ni, 1, tb), lambda i, e: (0, e, 0, i))
    col_spec = pl.BlockSpec((heads, nk, tb), lambda i, e: (0, 0, i))
    return pl.pallas_call(
        functools.partial(_peer_mix_kernel, dims=dims, sub=sub),
        grid=(t // tb, dims.n_experts // te),
        in_specs=[pl.BlockSpec((tb, d), lambda i, e: (i, 0)),
                  pl.BlockSpec((te, d), lambda i, e: (e, 0)),
                  pl.BlockSpec((te, d), lambda i, e: (e, 0)),
                  row_spec, row_spec, col_spec, col_spec],
        out_specs=pl.BlockSpec((tb, d), lambda i, e: (i, 0)),
        out_shape=jax.ShapeDtypeStruct((t, d), F32),
        scratch_shapes=[pltpu.VMEM((te, tb), BF16), pltpu.VMEM((te, tb), BF16)],
        compiler_params=_params("parallel", "arbitrary"),
        name="peer_mix",
    )(xn, u, v, row_tab(c1), row_tab(e1), r2, e2)


def _pad_prev(state, width):
    return jnp.pad(state, ((0, 0), (SUBLANES - (width - 1), 0), (0, 0)))


def forward(dims, x_prompt, x_sample, mem_prompt, cache_mem_k, cache_mem_v, state_conv_a,
            state_ssd_conv, state_ssd, norm_mix, norm_mem, norm_ffn, norm_final, w_in,
            a_conv_w, a_out, ssd_conv_w, ssd_conv_b, ssd_dt_bias, ssd_a_log, ssd_d, ssd_norm,
            ssd_out, w_mem_k, w_mem_v, xatt_out, w_o, peer_wq, peer_subkeys, peer_u, peer_v):
    d = dims.d_model
    tp, ts, t = dims.t_prompt, dims.t_sample, dims.tokens
    g, rh, p, n = dims.ssd_groups, dims.heads_per_group, dims.ssd_head_dim, dims.ssd_state
    bf = lambda a: a.astype(BF16)

    xp2, xs2 = x_prompt.reshape(tp, d), x_sample.reshape(ts, d)

    mn = rmsnorm(mem_prompt.reshape(dims.batch * dims.n_mem, d), norm_mem[0], BF16)
    xw = dims.xatt_width
    k_p = matmul(mn, w_mem_k, F32, tn=512, name="mem_k").reshape(dims.batch, dims.n_mem, xw)
    v_p = matmul(mn, w_mem_v, F32, tn=512, name="mem_v").reshape(dims.batch, dims.n_mem, xw)
    mem_k_p = k_p.reshape(1, dims.batch, dims.n_mem, dims.xatt_heads, dims.xatt_head_dim)
    mem_v_p = v_p.reshape(1, dims.batch, dims.n_mem, dims.xatt_heads, dims.xatt_head_dim)

    xn = rmsnorm_parts(xp2, xs2, norm_mix[0], BF16)
    wt = jnp.swapaxes(w_in, 1, 2)[0]
    dt0 = 3 * dims.a_width + dims.ssd_inner + dims.ssd_xbc
    proj = matmul_wt(xn, wt, [(0, dt0), (dt0 + dims.ssd_heads, dims.proj_width - dt0)], tm=512, tn=1024,
                     name="in_proj")
    dt_t = dt_proj_t(xn, wt, dt0, dims.ssd_heads).reshape(g, rh, t)

    a_cols = (dims.off_ain, dims.off_abg, dims.off_acg)
    zeros_a = jnp.zeros((dims.batch, SUBLANES, dims.a_width), F32)
    va_p, st_a_p = causal_conv(proj, zeros_a, a_conv_w[0], row0=0, nseq=dims.batch, length=dims.seq,
                               cols=dims.a_width, col_offs=a_cols, gated=True, out_dtype=BF16)
    prev_a = _pad_prev(state_conv_a[0], dims.a_conv).reshape(dims.dec_batch * SUBLANES, dims.a_width)
    va_s, u_s = short_seq_conv(proj, prev_a, a_conv_w[0], row0=tp, nseq=dims.dec_batch, seq=dims.dec_seq,
                               cols=dims.a_width, col_offs=a_cols, gated=True, out_dtype=BF16)
    na = dims.a_conv - 1
    conv_a_p = st_a_p[None, :, SUBLANES - na:, :]
    conv_a_s = u_s.reshape(dims.dec_batch, dims.dec_seq, dims.a_width)[None, :, dims.dec_seq - na:, :]

    zeros_b = jnp.zeros((dims.batch, SUBLANES, dims.ssd_xbc), F32)
    xbc_p = causal_conv(proj, zeros_b, ssd_conv_w[0], row0=0, nseq=dims.batch, length=dims.seq,
                        cols=dims.ssd_xbc, col_offs=(dims.off_xbc,), gated=False, bias=ssd_conv_b[0])
    prev_b = _pad_prev(state_ssd_conv[0], dims.ssd_conv).reshape(dims.dec_batch * SUBLANES, dims.ssd_xbc)
    xbc_s = short_seq_conv(proj, prev_b, ssd_conv_w[0], row0=tp, nseq=dims.dec_batch, seq=dims.dec_seq,
                           cols=dims.ssd_xbc, col_offs=(dims.off_xbc,), gated=False, bias=ssd_conv_b[0])
    nb = dims.ssd_conv - 1
    x0, x1c = dims.off_xbc, dims.off_xbc + dims.ssd_xbc
    ssd_conv_p = jnp.stack([lax.slice(proj, ((b + 1) * dims.seq - nb, x0), ((b + 1) * dims.seq, x1c))
                            for b in range(dims.batch)])[None]
    ssd_conv_s = lax.slice(proj.reshape(t // dims.dec_seq, dims.dec_seq, dims.proj_width),
                           (tp // dims.dec_seq, dims.dec_seq - nb, x0),
                           (t // dims.dec_seq, dims.dec_seq, x1c))[None]

    bias_g = ssd_dt_bias[0].reshape(g, rh, 1)
    alog_g = ssd_a_log[0].reshape(g, rh, 1)
    d_exp = jnp.repeat(ssd_d[0], p).reshape(1, dims.ssd_inner)
    norm_w = ssd_norm[0].reshape(1, dims.ssd_inner)
    yb_p, h_p = ssd_block(dims, xbc_p, proj, dt_t, bias_g, alog_g, d_exp, norm_w, row0=0, nrows=tp, chained=True)
    yb_s, h_s = ssd_block(dims, xbc_s, proj, dt_t, bias_g, alog_g, d_exp, norm_w, row0=tp, nrows=ts,
                          chained=False, h0=state_ssd[0])

    oc_p = cross_attention(dims, proj, k_p, v_p, row0=0, nseq=dims.batch, length=dims.seq)
    oc_s = cross_attention(dims, proj, cache_mem_k[0], cache_mem_v[0], row0=tp, nseq=dims.dec_batch,
                           length=dims.dec_seq)

    mix = merge_branches(dims, (va_p, yb_p, oc_p), (va_s, yb_s, oc_s), bf(a_out[0]), bf(ssd_out[0]),
                         bf(xatt_out[0]), proj)
    x1 = matmul(mix, bf(w_o[0]), F32, residual_parts=(xp2, xs2), name="out_proj")

    xn2 = rmsnorm(x1, norm_ffn[0], BF16)
    qp = matmul(xn2, bf(peer_wq[0]), F32, name="peer_query")
    c1, e1, r2, e2 = peer_select(dims, qp, peer_subkeys[0])
    ffn = peer_mix(dims, xn2, cast_layer(peer_u), cast_layer(peer_v), c1, e1, r2, e2)
    y_prompt = rmsnorm(x1, norm_final, F32, residual=ffn, row0=0, nrows=tp).reshape(dims.batch, dims.seq, d)
    y_sample = rmsnorm(x1, norm_final, F32, residual=ffn, row0=tp, nrows=ts).reshape(
        dims.dec_batch, dims.dec_seq, d)
    return (y_prompt, y_sample, mem_k_p, mem_v_p, conv_a_p, ssd_conv_p, h_p[None],
            conv_a_s, ssd_conv_s, h_s[None])


def kernel(x_prompt, x_sample, mem_prompt, cache_mem_k, cache_mem_v, state_conv_a, state_ssd_conv, state_ssd, norm_mix, norm_mem, norm_ffn, norm_final, w_in, a_conv_w, a_out, ssd_conv_w, ssd_conv_b, ssd_dt_bias, ssd_a_log, ssd_d, ssd_norm, ssd_out, w_mem_k, w_mem_v, xatt_out, w_o, peer_wq, peer_subkeys, peer_u, peer_v):
    return forward(FULL, x_prompt, x_sample, mem_prompt, cache_mem_k, cache_mem_v, state_conv_a,
                   state_ssd_conv, state_ssd, norm_mix, norm_mem, norm_ffn, norm_final, w_in,
                   a_conv_w, a_out, ssd_conv_w, ssd_conv_b, ssd_dt_bias, ssd_a_log, ssd_d, ssd_norm,
                   ssd_out, w_mem_k, w_mem_v, xatt_out, w_o, peer_wq, peer_subkeys, peer_u, peer_v)
```

```python
import dataclasses
import functools

import jax
import jax.numpy as jnp
import numpy as np
from jax import lax
from jax.experimental import pallas as pl
from jax.experimental.pallas import tpu as pltpu

F32 = jnp.float32
BF16 = jnp.bfloat16
EPS = 1e-6
HIGHEST = lax.Precision.HIGHEST
NEG_INF = float("-inf")

LANES = 128
SUBLANES = 8
VMEM_LIMIT_BYTES = 56 * 1024 * 1024


@dataclasses.dataclass(frozen=True)
class Dims:
    d_model: int = 4096
    batch: int = 4
    seq: int = 2048
    dec_batch: int = 128
    dec_seq: int = 8
    a_width: int = 2048
    a_conv: int = 3
    ssd_inner: int = 4096
    ssd_head_dim: int = 64
    ssd_groups: int = 8
    ssd_state: int = 128
    ssd_conv: int = 4
    ssd_chunk: int = 128
    n_mem: int = 256
    xatt_heads: int = 4
    xatt_head_dim: int = 512
    peer_heads: int = 8
    peer_keys: int = 128
    peer_topk: int = 16
    peer_qdim: int = 256

    @property
    def ssd_heads(self):
        return self.ssd_inner // self.ssd_head_dim

    @property
    def heads_per_group(self):
        return self.ssd_heads // self.ssd_groups

    @property
    def group_width(self):
        return self.ssd_inner // self.ssd_groups

    @property
    def ssd_xbc(self):
        return self.ssd_inner + 2 * self.ssd_groups * self.ssd_state

    @property
    def xatt_width(self):
        return self.xatt_heads * self.xatt_head_dim

    @property
    def n_experts(self):
        return self.peer_keys * self.peer_keys

    @property
    def t_prompt(self):
        return self.batch * self.seq

    @property
    def t_sample(self):
        return self.dec_batch * self.dec_seq

    @property
    def tokens(self):
        return self.t_prompt + self.t_sample

    @property
    def off_ain(self):
        return 0

    @property
    def off_abg(self):
        return self.a_width

    @property
    def off_acg(self):
        return 2 * self.a_width

    @property
    def off_z(self):
        return 3 * self.a_width

    @property
    def off_xbc(self):
        return self.off_z + self.ssd_inner

    @property
    def off_q(self):
        return self.off_xbc + self.ssd_xbc

    @property
    def off_gates(self):
        return self.off_q + self.xatt_width

    @property
    def proj_width(self):
        return self.off_gates + 3 * self.d_model


FULL = Dims()


def _params(*sem):
    return pltpu.CompilerParams(dimension_semantics=sem, vmem_limit_bytes=VMEM_LIMIT_BYTES)


def _pick(n, pref):
    t = min(n, pref)
    while n % t:
        t //= 2
    return t


def _rmsnorm_rows(x, g_ref, o_ref):
    inv = lax.rsqrt(jnp.mean(x * x, axis=-1, keepdims=True) + EPS)
    o_ref[...] = ((x * inv) * g_ref[...]).astype(o_ref.dtype)


def _rmsnorm_kernel(x_ref, g_ref, o_ref):
    _rmsnorm_rows(x_ref[...], g_ref, o_ref)


def _add_rmsnorm_kernel(x_ref, r_ref, g_ref, o_ref):
    _rmsnorm_rows(x_ref[...] + r_ref[...], g_ref, o_ref)


def _rmsnorm_parts_kernel(xp_ref, xs_ref, g_ref, o_ref, *, npb):
    i = pl.program_id(0)

    @pl.when(i < npb)
    def _():
        _rmsnorm_rows(xp_ref[...], g_ref, o_ref)

    @pl.when(i >= npb)
    def _():
        _rmsnorm_rows(xs_ref[...], g_ref, o_ref)


def rmsnorm(x, g, out_dtype, residual=None, row0=0, nrows=None, tm=256):
    m, d = x.shape
    nrows = m - row0 if nrows is None else nrows
    tm = _pick(nrows, tm)
    rb0 = row0 // tm
    assert row0 % tm == 0
    row = pl.BlockSpec((tm, d), lambda i: (rb0 + i, 0))
    gspec = pl.BlockSpec((1, d), lambda i: (0, 0))
    args = (x,) if residual is None else (x, residual)
    return pl.pallas_call(
        _rmsnorm_kernel if residual is None else _add_rmsnorm_kernel,
        grid=(nrows // tm,),
        in_specs=[row] * len(args) + [gspec],
        out_specs=pl.BlockSpec((tm, d), lambda i: (i, 0)),
        out_shape=jax.ShapeDtypeStruct((nrows, d), out_dtype),
        compiler_params=_params("parallel"),
        name="rmsnorm" if residual is None else "add_rmsnorm",
    )(*args, g.reshape(1, d))


def rmsnorm_parts(xp, xs, g, out_dtype, tm=256):
    d = xp.shape[1]
    tm = _pick(xs.shape[0], _pick(xp.shape[0], tm))
    npb, nsb = xp.shape[0] // tm, xs.shape[0] // tm
    return pl.pallas_call(
        functools.partial(_rmsnorm_parts_kernel, npb=npb),
        grid=(npb + nsb,),
        in_specs=[pl.BlockSpec((tm, d), lambda i: (jnp.minimum(i, npb - 1), 0)),
                  pl.BlockSpec((tm, d), lambda i: (jnp.maximum(i - npb, 0), 0)),
                  pl.BlockSpec((1, d), lambda i: (0, 0))],
        out_specs=pl.BlockSpec((tm, d), lambda i: (i, 0)),
        out_shape=jax.ShapeDtypeStruct((xp.shape[0] + xs.shape[0], d), out_dtype),
        compiler_params=_params("arbitrary"),
        name="rmsnorm_parts",
    )(xp, xs, g.reshape(1, d))


def _cast_kernel(x_ref, o_ref):
    o_ref[...] = x_ref[0].astype(o_ref.dtype)


def cast_layer(w, dtype=BF16, tr=512):
    _, r, c = w.shape
    tr = _pick(r, tr)
    return pl.pallas_call(
        _cast_kernel,
        grid=(r // tr,),
        in_specs=[pl.BlockSpec((1, tr, c), lambda i: (0, i, 0))],
        out_specs=pl.BlockSpec((tr, c), lambda i: (i, 0)),
        out_shape=jax.ShapeDtypeStruct((r, c), dtype),
        compiler_params=_params("parallel"),
        name="cast_layer",
    )(w)


def _mm_kernel(*refs, npb, layer_weight):
    a_ref, b_ref = refs[:2]
    refs = refs[2:]
    if npb is not None:
        rp_ref, rs_ref = refs[:2]
        refs = refs[2:]
    o_ref = refs[0]
    i = pl.program_id(1)
    if layer_weight:
        w_scr = refs[1]

        @pl.when(i == 0)
        def _():
            w_scr[...] = b_ref[0].astype(BF16)

        w = w_scr[...]
    else:
        w = b_ref[...]
    acc = jnp.dot(a_ref[...], w, preferred_element_type=F32)
    if npb is None:
        o_ref[...] = acc.astype(o_ref.dtype)
    else:
        @pl.when(i < npb)
        def _():
            o_ref[...] = (rp_ref[...] + acc).astype(o_ref.dtype)

        @pl.when(i >= npb)
        def _():
            o_ref[...] = (rs_ref[...] + acc).astype(o_ref.dtype)


def matmul(a, b, out_dtype, residual_parts=None, tm=512, tn=1024, name="matmul"):
    m, k = a.shape
    layer_weight = b.ndim == 3
    n = b.shape[-1]
    tm, tn = _pick(m, tm), _pick(n, tn)
    npb = None
    extra_specs, extra_args = [], []
    if residual_parts is not None:
        rp, rs = residual_parts
        tm = _pick(rs.shape[0], _pick(rp.shape[0], tm))
        npb = rp.shape[0] // tm
        extra_specs = [pl.BlockSpec((tm, tn), lambda j, i: (jnp.minimum(i, npb - 1), j)),
                       pl.BlockSpec((tm, tn), lambda j, i: (jnp.maximum(i - npb, 0), j))]
        extra_args = [rp, rs]
    if layer_weight:
        b_spec = pl.BlockSpec((1, k, tn), lambda j, i: (0, 0, j))
        scratch = [pltpu.VMEM((k, tn), BF16)]
    else:
        b_spec = pl.BlockSpec((k, tn), lambda j, i: (0, j))
        scratch = []
    sequential = layer_weight or npb is not None
    return pl.pallas_call(
        functools.partial(_mm_kernel, npb=npb, layer_weight=layer_weight),
        grid=(n // tn, m // tm),
        in_specs=[pl.BlockSpec((tm, k), lambda j, i: (i, 0)), b_spec] + extra_specs,
        out_specs=pl.BlockSpec((tm, tn), lambda j, i: (i, j)),
        out_shape=jax.ShapeDtypeStruct((m, n), out_dtype),
        scratch_shapes=scratch,
        compiler_params=_params(*(("arbitrary", "arbitrary") if sequential else ("parallel", "parallel"))),
        name=name,
    )(a, b, *extra_args)


def _mm_wt_kernel(a_ref, wt_ref, o_ref, w_scr):
    @pl.when(pl.program_id(1) == 0)
    def _():
        w_scr[...] = wt_ref[...].astype(BF16)

    o_ref[...] = lax.dot_general(a_ref[...], w_scr[...], (((1,), (1,)), ((), ())),
                                 preferred_element_type=F32).astype(o_ref.dtype)


def matmul_wt(a, wt, segments, out_dtype=F32, tm=1024, tn=512, name="matmul_wt"):
    m, k = a.shape
    tm = _pick(m, tm)
    for start, length in segments:
        tn = _pick(length, tn)
        assert start % SUBLANES == 0
    assert all(length % tn == 0 for _, length in segments)
    starts = np.concatenate([np.arange(s, s + l, tn) for s, l in segments]).astype(np.int32)
    nblk = len(starts)
    bounds = np.cumsum([l // tn for _, l in segments])[:-1]
    shifts = [segments[i + 1][0] - (segments[i][0] + segments[i][1]) for i in range(len(segments) - 1)]

    def row_start(j):
        r = segments[0][0] + j * tn
        for b, sh in zip(bounds, shifts):
            r = r + jnp.where(j >= b, sh, 0)
        return pl.multiple_of(r, SUBLANES)

    return pl.pallas_call(
        _mm_wt_kernel,
        grid=(nblk, m // tm),
        in_specs=[pl.BlockSpec((tm, k), lambda j, i: (i, 0)),
                  pl.BlockSpec((pl.Element(tn), pl.Element(k)), lambda j, i: (row_start(j), 0))],
        out_specs=pl.BlockSpec((tm, tn), lambda j, i: (i, j)),
        out_shape=jax.ShapeDtypeStruct((m, nblk * tn), out_dtype),
        scratch_shapes=[pltpu.VMEM((tn, k), BF16)],
        compiler_params=_params("arbitrary", "arbitrary"),
        name=name,
    )(a, wt)


def _dt_proj_kernel(a_ref, wt_ref, o_ref):
    o_ref[...] = lax.dot_general(wt_ref[...].astype(BF16), a_ref[...], (((1,), (1,)), ((), ())),
                                 preferred_element_type=F32)


def dt_proj_t(a, wt, start, nrows, tm=1024):
    m, k = a.shape
    tm = _pick(m, tm)
    assert start % SUBLANES == 0
    return pl.pallas_call(
        _dt_proj_kernel,
        grid=(m // tm,),
        in_specs=[pl.BlockSpec((tm, k), lambda i: (i, 0)),
                  pl.BlockSpec((pl.Element(nrows), pl.Element(k)), lambda i: (start, 0))],
        out_specs=pl.BlockSpec((nrows, tm), lambda i: (0, i)),
        out_shape=jax.ShapeDtypeStruct((nrows, m), F32),
        compiler_params=_params("parallel"),
        name="dt_proj",
    )(a, wt)


def _conv_kernel(*refs, width, tl, gated):
    if gated:
        in_ref, bg_ref, cg_ref, prev_ref, w_ref, o_ref, st_ref, scr = refs
    else:
        in_ref, prev_ref, w_ref, bias_ref, o_ref, scr = refs
    lt = pl.program_id(2)

    @pl.when(lt == 0)
    def _():
        scr[0:SUBLANES, :] = prev_ref[0]

    u = cg_ref[...] * in_ref[...] if gated else in_ref[...]
    scr[SUBLANES:SUBLANES + tl, :] = u
    acc = w_ref[width - 1:width, :] * u
    for k in range(width - 1):
        lo = SUBLANES - (width - 1 - k)
        acc = acc + w_ref[k:k + 1, :] * scr[lo:lo + tl, :]
    tail = scr[tl:tl + SUBLANES, :]
    scr[0:SUBLANES, :] = tail
    if gated:
        o_ref[...] = (bg_ref[...] * acc).astype(o_ref.dtype)
        st_ref[0] = tail
    else:
        y = acc + bias_ref[...]
        o_ref[...] = (y * jax.nn.sigmoid(y)).astype(o_ref.dtype)


def causal_conv(proj, prev8, w, *, row0, nseq, length, cols, col_offs, gated, bias=None, out_dtype=F32,
                tl=1024, tc=512):
    width = w.shape[0]
    tl, tc = _pick(length, tl), _pick(cols, tc)
    while any(o % tc for o in col_offs):
        tc //= 2
    nl = length // tl
    rb0 = row0 // tl
    assert row0 % tl == 0 and all(o % tc == 0 for o in col_offs)

    def blk(off):
        return pl.BlockSpec((tl, tc), lambda s, c, l, off=off: (rb0 + s * nl + l, off // tc + c))

    prev_spec = pl.BlockSpec((1, SUBLANES, tc), lambda s, c, l: (s, 0, c))
    w_spec = pl.BlockSpec((width, tc), lambda s, c, l: (0, c))
    out_spec = pl.BlockSpec((tl, tc), lambda s, c, l: (s * nl + l, c))
    out_shape = jax.ShapeDtypeStruct((nseq * length, cols), out_dtype)
    if gated:
        in_specs = [blk(col_offs[0]), blk(col_offs[1]), blk(col_offs[2]), prev_spec, w_spec]
        args = (proj, proj, proj, prev8, w)
        out_specs = [out_spec, pl.BlockSpec((1, SUBLANES, tc), lambda s, c, l: (s, 0, c))]
        out_shape = [out_shape, jax.ShapeDtypeStruct((nseq, SUBLANES, cols), F32)]
    else:
        in_specs = [blk(col_offs[0]), prev_spec, w_spec, pl.BlockSpec((1, tc), lambda s, c, l: (0, c))]
        args = (proj, prev8, w, bias.reshape(1, cols))
        out_specs = out_spec
    return pl.pallas_call(
        functools.partial(_conv_kernel, width=width, tl=tl, gated=gated),
        grid=(nseq, cols // tc, nl),
        in_specs=in_specs,
        out_specs=out_specs,
        out_shape=out_shape,
        scratch_shapes=[pltpu.VMEM((tl + SUBLANES, tc), F32)],
        compiler_params=_params("parallel", "parallel", "arbitrary"),
        name="gated_conv" if gated else "ssd_conv",
    )(*args)


def _short_conv_kernel(*refs, width, seq, gated):
    if gated:
        in_ref, bg_ref, cg_ref, prev_ref, w_ref, o_ref, u_ref, scr_u, scr_p = refs
    else:
        in_ref, prev_ref, w_ref, bias_ref, o_ref, scr_u, scr_p = refs
    rows = in_ref.shape[0]
    u = cg_ref[...] * in_ref[...] if gated else in_ref[...]
    zeros = jnp.zeros((SUBLANES, u.shape[1]), F32)
    scr_u[0:SUBLANES, :] = zeros
    scr_u[SUBLANES:SUBLANES + rows, :] = u
    scr_p[0:rows, :] = prev_ref[...]
    scr_p[rows:rows + SUBLANES, :] = zeros
    pos = lax.broadcasted_iota(jnp.int32, u.shape, 0) % seq
    acc = w_ref[width - 1:width, :] * u
    for k in range(width - 1):
        shift = width - 1 - k
        lo = SUBLANES - shift
        operand = jnp.where(pos >= shift, scr_u[lo:lo + rows, :], scr_p[lo:lo + rows, :])
        acc = acc + w_ref[k:k + 1, :] * operand
    if gated:
        o_ref[...] = (bg_ref[...] * acc).astype(o_ref.dtype)
        u_ref[...] = u
    else:
        y = acc + bias_ref[...]
        o_ref[...] = (y * jax.nn.sigmoid(y)).astype(o_ref.dtype)


def short_seq_conv(proj, prev8, w, *, row0, nseq, seq, cols, col_offs, gated, bias=None, out_dtype=F32,
                   rows=128, tc=2048):
    width = w.shape[0]
    assert seq == SUBLANES and width - 1 <= seq
    total = nseq * seq
    rows, tc = _pick(total, rows), _pick(cols, tc)
    while any(o % tc for o in col_offs):
        tc //= 2
    rb0 = row0 // rows
    assert row0 % rows == 0

    def blk(off):
        return pl.BlockSpec((rows, tc), lambda r, c, off=off: (rb0 + r, off // tc + c))

    own = pl.BlockSpec((rows, tc), lambda r, c: (r, c))
    w_spec = pl.BlockSpec((width, tc), lambda r, c: (0, c))
    out_shape = jax.ShapeDtypeStruct((total, cols), out_dtype)
    if gated:
        in_specs = [blk(col_offs[0]), blk(col_offs[1]), blk(col_offs[2]), own, w_spec]
        args = (proj, proj, proj, prev8, w)
        out_specs = [own, own]
        out_shape = [out_shape, jax.ShapeDtypeStruct((total, cols), F32)]
    else:
        in_specs = [blk(col_offs[0]), own, w_spec, pl.BlockSpec((1, tc), lambda r, c: (0, c))]
        args = (proj, prev8, w, bias.reshape(1, cols))
        out_specs = own
    return pl.pallas_call(
        functools.partial(_short_conv_kernel, width=width, seq=seq, gated=gated),
        grid=(total // rows, cols // tc),
        in_specs=in_specs,
        out_specs=out_specs,
        out_shape=out_shape,
        scratch_shapes=[pltpu.VMEM((rows + SUBLANES, tc), F32), pltpu.VMEM((rows + SUBLANES, tc), F32)],
        compiler_params=_params("parallel", "parallel"),
        name="gated_conv_short" if gated else "ssd_conv_short",
    )(*args)


def _softplus(x):
    return jnp.maximum(x, 0.0) + jnp.log1p(jnp.exp(-jnp.abs(x)))


def _ssd_kernel(*refs, nseq, dims, chained, gpb):
    if chained:
        (xs_ref, b_ref, c_ref, z_ref, dt_ref, bias_ref, alog_ref, d_ref, nw_ref, sel_ref,
         y_ref, hout_ref, h_scr) = refs
    else:
        (xs_ref, b_ref, c_ref, z_ref, dt_ref, bias_ref, alog_ref, d_ref, nw_ref, sel_ref, h0_ref,
         y_ref, hout_ref) = refs
    q = dims.ssd_chunk
    r_heads = dims.heads_per_group
    p = dims.ssd_head_dim
    gw = dims.group_width
    n = dims.ssd_state
    seg = q // nseq

    if chained:
        @pl.when(pl.program_id(2) == 0)
        def _():
            h_scr[...] = jnp.zeros_like(h_scr)

    for gi in range(gpb):
        gcols = slice(gi * gw, (gi + 1) * gw)
        ncols = slice(gi * n, (gi + 1) * n)
        _ssd_group(
            xs_ref.at[:, gcols], b_ref.at[:, ncols], c_ref.at[:, ncols], z_ref.at[:, gcols],
            dt_ref.at[gi], bias_ref.at[gi], alog_ref.at[gi], d_ref.at[:, gcols], nw_ref.at[:, gcols], sel_ref,
            y_ref.at[:, gcols],
            hout_ref.at[:, gi * r_heads:(gi + 1) * r_heads],
            h_scr.at[gi * r_heads * p:(gi + 1) * r_heads * p, :] if chained else None,
            None if chained else h0_ref.at[:, gi * r_heads:(gi + 1) * r_heads],
            nseq=nseq, dims=dims, chained=chained)


def _ssd_group(xs_ref, b_ref, c_ref, z_ref, dt_ref, bias_ref, alog_ref, d_ref, nw_ref, sel_ref, y_ref,
               hout_ref, h_scr, h0_ref, *, nseq, dims, chained):
    q = dims.ssd_chunk
    r_heads = dims.heads_per_group
    p = dims.ssd_head_dim
    gw = dims.group_width
    seg = q // nseq
    row = lax.broadcasted_iota(jnp.int32, (q, q), 0)
    col = lax.broadcasted_iota(jnp.int32, (q, q), 1)
    same = (row // seg) == (col // seg)
    dt = _softplus(dt_ref[...] + bias_ref[...])
    adt = dt * (-jnp.exp(alog_ref[...]))
    cum_mask = jnp.where(same & (row <= col), 1.0, 0.0).astype(F32)
    acum_t = jnp.dot(adt, cum_mask, precision=HIGHEST, preferred_element_type=F32)
    if nseq == 1:
        atot_t = jnp.broadcast_to(acum_t[:, q - 1:q], (r_heads, q))
    else:
        atot_t = jnp.dot(adt, jnp.where(same, 1.0, 0.0).astype(F32), precision=HIGHEST,
                         preferred_element_type=F32)
    stack = jnp.concatenate(
        [acum_t, dt, dt * jnp.exp(atot_t - acum_t), jnp.exp(acum_t),
         jnp.zeros((q - 4 * r_heads, q), F32)], axis=0)
    cols_form = stack.T
    sel = sel_ref[...]
    hi = cols_form.astype(BF16)
    rest = cols_form - hi.astype(F32)
    mid = rest.astype(BF16)
    low = (rest - mid.astype(F32)).astype(BF16)
    expand = (jnp.dot(hi, sel, preferred_element_type=F32) + jnp.dot(mid, sel, preferred_element_type=F32)
              + jnp.dot(low, sel, preferred_element_type=F32))
    e_dt, e_st, e_ac = expand[:, :gw], expand[:, gw:2 * gw], expand[:, 2 * gw:]

    x = xs_ref[...]
    bb = b_ref[...].astype(BF16)
    cb_ = c_ref[...].astype(BF16)
    cb = lax.dot_general(cb_, bb, (((1,), (1,)), ((), ())), preferred_element_type=F32)
    x_dt = x * e_dt
    x_dt_b = x_dt.astype(BF16)
    causal = same & (row >= col)
    lane = lax.broadcasted_iota(jnp.int32, (q, LANES), 1)
    heads_per_tile = LANES // p

    y_tiles = []
    for tile in range(gw // LANES):
        xt = x_dt_b[:, tile * LANES:(tile + 1) * LANES]
        acc = None
        for k in range(heads_per_tile):
            r = tile * heads_per_tile + k
            segm = cols_form[:, r:r + 1] - acum_t[r:r + 1, :]
            decay = jnp.exp(jnp.where(causal, segm, NEG_INF))
            m_r = (cb * decay).astype(BF16)
            x_r = jnp.where((lane >= k * p) & (lane < (k + 1) * p), xt, jnp.zeros_like(xt))
            part = jnp.dot(m_r, x_r, preferred_element_type=F32)
            acc = part if acc is None else acc + part
        y_tiles.append(acc)
    y = jnp.concatenate(y_tiles, axis=1)

    x_st = x * e_st
    eac_t = jnp.exp(atot_t)

    if chained:
        h = h_scr[...]
        y_off = lax.dot_general(cb_, h.astype(BF16), (((1,), (1,)), ((), ())),
                                preferred_element_type=F32)
        y = y + y_off * e_ac
        s_new = lax.dot_general(x_st.astype(BF16), bb, (((0,), (0,)), ((), ())),
                                preferred_element_type=F32)
        scale = jnp.broadcast_to(eac_t[:, q - 1:q], (r_heads, LANES))
        for r in range(r_heads):
            rows = slice(r * p, (r + 1) * p)
            h_scr[rows, :] = h[rows, :] * jnp.broadcast_to(scale[r:r + 1, :], (p, LANES)) + s_new[rows, :]
        for r in range(r_heads):
            hout_ref[0, r] = h_scr[r * p:(r + 1) * p, :]
    else:
        x_st_t = x_st.T
        rows_q = lax.broadcasted_iota(jnp.int32, (q, gw), 0)
        lanes_q = lax.broadcasted_iota(jnp.int32, (gw, q), 1)
        y_off = jnp.zeros((q, gw), F32)
        for s in range(nseq):
            h_s = h0_ref[s].reshape(r_heads * p, dims.ssd_state)
            y_s = lax.dot_general(cb_, h_s.astype(BF16), (((1,), (1,)), ((), ())),
                                  preferred_element_type=F32)
            y_off = jnp.where(rows_q // seg == s, y_s, y_off)
            xs_s = jnp.where(lanes_q // seg == s, x_st_t, 0.0).astype(BF16)
            s_new = jnp.dot(xs_s, bb, preferred_element_type=F32)
            scale = jnp.broadcast_to(eac_t[:, s * seg:s * seg + 1], (r_heads, LANES))
            for r in range(r_heads):
                rows = slice(r * p, (r + 1) * p)
                hout_ref[s, r] = (h_s[rows, :] * jnp.broadcast_to(scale[r:r + 1, :], (p, LANES))
                                  + s_new[rows, :])
        y = y + y_off * e_ac

    y = y + d_ref[...] * x
    z = z_ref[...]
    y = y * (z * jax.nn.sigmoid(z))
    inv = lax.rsqrt(jnp.mean(y * y, axis=-1, keepdims=True) + EPS)
    y_ref[...] = ((y * inv) * nw_ref[...]).astype(y_ref.dtype)


def _ssd_selector(dims):
    r_heads, p, gw = dims.heads_per_group, dims.ssd_head_dim, dims.group_width
    sel = np.zeros((dims.ssd_chunk, 3 * gw), np.float32)
    for part in range(3):
        for r in range(r_heads):
            sel[(part + 1) * r_heads + r, part * gw + r * p:part * gw + (r + 1) * p] = 1.0
    return jnp.asarray(sel, BF16)


def ssd_block(dims, xbc, proj, dt_t, bias_g, alog_g, d_exp, norm_w, *, row0, nrows, chained, h0=None):
    q = dims.ssd_chunk
    g = dims.ssd_groups
    gw = dims.group_width
    n = dims.ssd_state
    r_heads, p = dims.heads_per_group, dims.ssd_head_dim
    assert LANES % p == 0 and q == LANES and 4 * r_heads <= q
    sel = _ssd_selector(dims)
    rb0 = row0 // q
    zoff = dims.off_z // gw
    boff = dims.ssd_inner // n
    coff = boff + g

    gpb = 1
    if chained:
        for cand_gpb in (4, 2):
            if all(v % cand_gpb == 0 for v in (g, zoff, boff)):
                gpb = cand_gpb
                break
    gblocks = g // gpb
    if chained:
        nb, nc = dims.batch, dims.seq // q
        grid = (nb, gblocks, nc)
        rowblk = lambda b, gi, c: b * nc + c
        sems = ("parallel", "parallel", "arbitrary")
        nseq = 1
    else:
        nseq = q // dims.dec_seq
        nb = dims.dec_batch // nseq
        grid = (nb, gblocks)
        rowblk = lambda b, gi: b
        sems = ("parallel", "parallel")

    def spec(shape, fn):
        return pl.BlockSpec(shape, fn)

    if chained:
        ix = lambda f: (lambda b, gi, c: f(rowblk(b, gi, c), gi))
    else:
        ix = lambda f: (lambda b, gi: f(rowblk(b, gi), gi))

    in_specs = [
        spec((q, gpb * gw), ix(lambda rb, gi: (rb, gi))),
        spec((q, gpb * n), ix(lambda rb, gi: (rb, boff // gpb + gi))),
        spec((q, gpb * n), ix(lambda rb, gi: (rb, coff // gpb + gi))),
        spec((q, gpb * gw), ix(lambda rb, gi: (rb0 + rb, zoff // gpb + gi))),
        spec((gpb, r_heads, q), ix(lambda rb, gi: (gi, 0, rb0 + rb))),
        spec((gpb, r_heads, 1), ix(lambda rb, gi: (gi, 0, 0))),
        spec((gpb, r_heads, 1), ix(lambda rb, gi: (gi, 0, 0))),
        spec((1, gpb * gw), ix(lambda rb, gi: (0, gi))),
        spec((1, gpb * gw), ix(lambda rb, gi: (0, gi))),
        spec((q, 3 * gw), ix(lambda rb, gi: (0, 0))),
    ]
    args = [xbc, xbc, xbc, proj, dt_t, bias_g, alog_g, d_exp, norm_w, sel]
    y_spec = spec((q, gpb * gw), ix(lambda rb, gi: (rb, gi)))
    if chained:
        h_spec = pl.BlockSpec((1, gpb * r_heads, p, n), lambda b, gi, c: (b, gi, 0, 0))
        h_shape = jax.ShapeDtypeStruct((nb, dims.ssd_heads, p, n), F32)
        scratch = [pltpu.VMEM((gpb * r_heads * p, n), F32)]
    else:
        in_specs.append(pl.BlockSpec((nseq, gpb * r_heads, p, n), lambda b, gi: (b, gi, 0, 0)))
        args.append(h0)
        h_spec = pl.BlockSpec((nseq, gpb * r_heads, p, n), lambda b, gi: (b, gi, 0, 0))
        h_shape = jax.ShapeDtypeStruct((dims.dec_batch, dims.ssd_heads, p, n), F32)
        scratch = []
    return pl.pallas_call(
        functools.partial(_ssd_kernel, nseq=nseq, dims=dims, chained=chained, gpb=gpb),
        grid=grid,
        in_specs=in_specs,
        out_specs=[y_spec, h_spec],
        out_shape=[jax.ShapeDtypeStruct((nrows, dims.ssd_inner), BF16), h_shape],
        scratch_shapes=scratch,
        compiler_params=_params(*sems),
        name="ssd_prompt" if chained else "ssd_sample",
    )(*args)


def _attend(q, k, v, scale):
    s = lax.dot_general(q.astype(BF16), k.astype(BF16), (((1,), (1,)), ((), ())),
                        preferred_element_type=F32) * scale
    s = s - jnp.max(s, axis=-1, keepdims=True)
    e = jnp.exp(s)
    pr = (e / jnp.sum(e, axis=-1, keepdims=True)).astype(BF16)
    return jnp.dot(pr, v.astype(BF16), preferred_element_type=F32)


def _xattn_kernel(q_ref, k_ref, v_ref, o_ref, *, heads, head_dim):
    for h in range(heads):
        cs = slice(h * head_dim, (h + 1) * head_dim)
        o_ref[:, cs] = _attend(q_ref[:, cs], k_ref[0, :, cs], v_ref[0, :, cs], head_dim ** -0.5).astype(o_ref.dtype)


def _xattn_cached_kernel(q_ref, k_ref, v_ref, o_ref, *, heads, head_dim):
    tq = q_ref.shape[0]
    m = k_ref.shape[1]
    k = k_ref[0].reshape(m * heads, head_dim).astype(BF16)
    v = v_ref[0].reshape(m * heads, head_dim).astype(BF16)
    q = jnp.concatenate([q_ref[:, h * head_dim:(h + 1) * head_dim] for h in range(heads)], axis=0)
    s = lax.dot_general(q.astype(BF16), k, (((1,), (1,)), ((), ())), preferred_element_type=F32)
    s = s * (head_dim ** -0.5)
    row_head = lax.broadcasted_iota(jnp.int32, s.shape, 0) // tq
    col_head = lax.broadcasted_iota(jnp.int32, s.shape, 1) % heads
    s = jnp.where(row_head == col_head, s, NEG_INF)
    s = s - jnp.max(s, axis=-1, keepdims=True)
    e = jnp.exp(s)
    pr = (e / jnp.sum(e, axis=-1, keepdims=True)).astype(BF16)
    o = jnp.dot(pr, v, preferred_element_type=F32)
    for h in range(heads):
        o_ref[:, h * head_dim:(h + 1) * head_dim] = o[h * tq:(h + 1) * tq, :].astype(o_ref.dtype)


def cross_attention(dims, proj, k, v, *, row0, nseq, length, tq=512):
    w = dims.xatt_width
    tq = _pick(length, tq)
    nq = length // tq
    rb0 = row0 // tq
    assert row0 % tq == 0 and dims.off_q % w == 0
    qoff = dims.off_q // w
    hd, nh = dims.xatt_head_dim, dims.xatt_heads
    if k.ndim == 4:
        assert nq == 1 and tq % SUBLANES == 0
        kv_spec = pl.BlockSpec((1, dims.n_mem, nh, hd), lambda b: (b, 0, 0, 0))
        return pl.pallas_call(
            functools.partial(_xattn_cached_kernel, heads=nh, head_dim=hd),
            grid=(nseq,),
            in_specs=[pl.BlockSpec((tq, w), lambda b: (rb0 + b, qoff)), kv_spec, kv_spec],
            out_specs=pl.BlockSpec((tq, w), lambda b: (b, 0)),
            out_shape=jax.ShapeDtypeStruct((nseq * length, w), BF16),
            compiler_params=_params("parallel"),
            name="cross_attention_cached",
        )(proj, k, v)
    kv_spec = pl.BlockSpec((1, dims.n_mem, w), lambda b, i: (b, 0, 0))
    return pl.pallas_call(
        functools.partial(_xattn_kernel, heads=nh, head_dim=hd),
        grid=(nseq, nq),
        in_specs=[pl.BlockSpec((tq, w), lambda b, i: (rb0 + b * nq + i, qoff)), kv_spec, kv_spec],
        out_specs=pl.BlockSpec((tq, w), lambda b, i: (b * nq + i, 0)),
        out_shape=jax.ShapeDtypeStruct((nseq * length, w), BF16),
        compiler_params=_params("parallel", "parallel"),
        name="cross_attention",
    )(proj, k, v)


def _merge_kernel(vap_ref, ybp_ref, ocp_ref, vas_ref, ybs_ref, ocs_ref, wa_ref, wb_ref, wc_ref,
                  ga_ref, gb_ref, gc_ref, o_ref, *, npb):
    i = pl.program_id(1)

    def body(va_ref, yb_ref, oc_ref):
        ha = jnp.dot(va_ref[...], wa_ref[...], preferred_element_type=F32)
        hb = jnp.dot(yb_ref[...], wb_ref[...], preferred_element_type=F32)
        hc = jnp.dot(oc_ref[...], wc_ref[...], preferred_element_type=F32)
        mix = (jax.nn.sigmoid(ga_ref[...]) * ha + jax.nn.sigmoid(gb_ref[...]) * hb
               + jax.nn.sigmoid(gc_ref[...]) * hc)
        o_ref[...] = mix.astype(o_ref.dtype)

    @pl.when(i < npb)
    def _():
        body(vap_ref, ybp_ref, ocp_ref)

    @pl.when(i >= npb)
    def _():
        body(vas_ref, ybs_ref, ocs_ref)


def merge_branches(dims, prompt, sample, wa, wb, wc, proj, tm=512, tn=512):
    d = dims.d_model
    mp, ms = prompt[0].shape[0], sample[0].shape[0]
    tm, tn = _pick(ms, _pick(mp, tm)), _pick(d, tn)
    npb = mp // tm
    g0 = dims.off_gates // tn
    gd = d // tn
    assert dims.off_gates % tn == 0

    def lhs_p(a):
        return pl.BlockSpec((tm, a.shape[1]), lambda j, i: (jnp.minimum(i, npb - 1), 0))

    def lhs_s(a):
        return pl.BlockSpec((tm, a.shape[1]), lambda j, i: (jnp.maximum(i - npb, 0), 0))

    def rhs(width):
        return pl.BlockSpec((width, tn), lambda j, i: (0, j), pipeline_mode=pl.Buffered(1))

    def gate(k):
        return pl.BlockSpec((tm, tn), lambda j, i, k=k: (i, g0 + k * gd + j))

    return pl.pallas_call(
        functools.partial(_merge_kernel, npb=npb),
        grid=(d // tn, (mp + ms) // tm),
        in_specs=[lhs_p(a) for a in prompt] + [lhs_s(a) for a in sample]
        + [rhs(wa.shape[0]), rhs(wb.shape[0]), rhs(wc.shape[0]), gate(0), gate(1), gate(2)],
        out_specs=pl.BlockSpec((tm, tn), lambda j, i: (i, j)),
        out_shape=jax.ShapeDtypeStruct((mp + ms, d), BF16),
        compiler_params=_params("arbitrary", "arbitrary"),
        name="merge_branches",
    )(*prompt, *sample, wa, wb, wc, proj, proj, proj)


def _peer_select_kernel(q_ref, keys_ref, c1_ref, e1_ref, r2_ref, e2_ref, scores, work, tops, cand, ranks, *,
                        dims):
    heads, nk, topk = dims.peer_heads, dims.peer_keys, dims.peer_topk
    half = dims.peer_qdim // 2
    tb = q_ref.shape[0]
    rank = lax.broadcasted_iota(jnp.int32, (topk, tb), 0)

    for k in range(2 * heads):
        qh = q_ref[:, k * half:(k + 1) * half].astype(BF16)
        s = lax.dot_general(keys_ref[k].astype(BF16), qh, (((1,), (1,)), ((), ())),
                            preferred_element_type=F32)
        scores[k] = s
        work[k] = s
        tops[k] = jnp.full((topk, tb), NEG_INF, F32)

    for h in range(heads):
        ranks[h] = jnp.full((nk, tb), float(topk), F32)

    def extract(r, carry):
        rf = jnp.asarray(r, F32)
        for k in range(2 * heads):
            s = work[k]
            m = jnp.max(s, axis=0, keepdims=True)
            hit = s == m
            tops[k] = jnp.where(rank == r, m, tops[k])
            work[k] = jnp.where(hit, NEG_INF, s)
            if k % 2:
                ranks[k // 2] = jnp.where(hit, rf, ranks[k // 2])
        return carry
    lax.fori_loop(0, topk, extract, 0)

    for h in range(heads):
        a, b = tops[2 * h], tops[2 * h + 1]
        cand[h, 0:topk, :] = a[0:1, :] + b
        for pi in range(1, topk):
            lo = topk + (pi - 1) * SUBLANES
            cand[h, lo:lo + SUBLANES, :] = a[pi:pi + 1, :] + b[0:SUBLANES, :]

    def threshold(r, taus):
        new = []
        for h in range(heads):
            c = cand[h]
            m = jnp.max(c, axis=0, keepdims=True)
            cand[h] = jnp.where(c == m, NEG_INF, c)
            new.append(m)
        return tuple(new)
    taus = lax.fori_loop(0, topk, threshold, tuple(jnp.zeros((1, tb), F32) for _ in range(heads)))

    for h in range(heads):
        a, b = tops[2 * h], tops[2 * h + 1]
        tau = taus[h]
        z = None
        for pi in range(topk):
            rows = topk if pi == 0 else SUBLANES
            c = a[pi:pi + 1, :] + b[0:rows, :]
            part = jnp.sum(jnp.where(c >= tau, jnp.exp(c - (a[0:1, :] + b[0:1, :])), 0.0), axis=0, keepdims=True)
            z = part if z is None else z + part
        s1, s2 = scores[2 * h], scores[2 * h + 1]
        count = jnp.zeros_like(s1)
        for qi in range(topk):
            reach = a + b[qi:qi + 1, :] >= tau
            alpha = jnp.min(jnp.where(reach, a, float("inf")), axis=0, keepdims=True)
            count = count + jnp.where(s1 >= alpha, 1.0, 0.0)
        c1_ref[h] = count
        r2_ref[h] = ranks[h].astype(r2_ref.dtype)
        e1_ref[h] = jnp.exp(s1 - a[0:1, :]) * (0.5 / z)
        e2_ref[h] = jnp.exp(s2 - b[0:1, :]).astype(e2_ref.dtype)


def peer_select(dims, q, subkeys, tb=128):
    t = q.shape[0]
    heads, nk = dims.peer_heads, dims.peer_keys
    half = dims.peer_qdim // 2
    assert dims.peer_topk >= SUBLANES and heads == SUBLANES
    tab = jax.ShapeDtypeStruct((heads, nk, t), F32)
    tab16 = jax.ShapeDtypeStruct((heads, nk, t), BF16)
    tab_spec = pl.BlockSpec((heads, nk, tb), lambda i: (0, 0, i))
    return pl.pallas_call(
        functools.partial(_peer_select_kernel, dims=dims),
        grid=(t // tb,),
        in_specs=[pl.BlockSpec((tb, q.shape[1]), lambda i: (i, 0)),
                  pl.BlockSpec((2 * heads, nk, half), lambda i: (0, 0, 0))],
        out_specs=[tab_spec, tab_spec, tab_spec, tab_spec],
        out_shape=[tab, tab, tab16, tab16],
        scratch_shapes=[pltpu.VMEM((2 * heads, nk, tb), F32),
                        pltpu.VMEM((2 * heads, nk, tb), F32),
                        pltpu.VMEM((2 * heads, dims.peer_topk, tb), F32),
                        pltpu.VMEM((heads, dims.peer_topk + (dims.peer_topk - 1) * SUBLANES, tb), F32),
                        pltpu.VMEM((heads, nk, tb), F32)],
        compiler_params=_params("parallel"),
        name="peer_select",
    )(q, subkeys.reshape(2 * heads, nk, half))


def _peer_mix_kernel(x_ref, u_ref, v_ref, c1_ref, e1_ref, r2_ref, e2_ref, o_ref, gw_scr, w_scr, *, dims, sub):
    heads, nk = dims.peer_heads, dims.peer_keys
    eb = pl.program_id(1)
    te = u_ref.shape[0]

    @pl.when(eb == 0)
    def _():
        o_ref[...] = jnp.zeros_like(o_ref)

    tb = x_ref.shape[0]
    tr, tl = 2 * SUBLANES, LANES
    zero = jnp.zeros((tr, tl), BF16)
    for ii in range(te // nk):
        for c0 in range(0, tb, tl):
            cs = slice(c0, c0 + tl)
            acc = [None] * (nk // tr)
            for h in range(heads):
                count = jnp.broadcast_to(c1_ref[h, ii, :, cs], (tr, tl)).astype(BF16)
                e1 = jnp.broadcast_to(e1_ref[h, ii, :, cs], (tr, tl)).astype(BF16)
                for s in range(nk // tr):
                    rs = slice(s * tr, (s + 1) * tr)
                    g1 = jnp.minimum(jnp.maximum(count - r2_ref[h, rs, cs], zero), e1)
                    wh = g1 * e2_ref[h, rs, cs]
                    acc[s] = wh if acc[s] is None else acc[s] + wh
            for s in range(nk // tr):
                w_scr[ii * nk + s * tr:ii * nk + (s + 1) * tr, cs] = acc[s]
    x = x_ref[...]
    for sb in range(te // sub):
        rows = slice(sb * sub, (sb + 1) * sub)
        hh = lax.dot_general(u_ref[rows, :], x, (((1,), (1,)), ((), ())), preferred_element_type=F32)
        gelu2 = hh * (1.0 + lax.erf(hh * (2.0 ** -0.5)))
        gw_scr[rows, :] = gelu2.astype(BF16) * w_scr[rows, :]
        o_ref[...] += lax.dot_general(gw_scr[rows, :], v_ref[rows, :], (((0,), (0,)), ((), ())),
                                      preferred_element_type=F32)


def peer_mix(dims, xn, u, v, c1, e1, r2, e2, tb=512, te=512, sub=512):
    t, d = xn.shape
    heads, nk = dims.peer_heads, dims.peer_keys
    tb, te = _pick(t, tb), _pick(dims.n_experts, te)
    sub = min(sub, te)
    assert te % sub == 0 and sub % nk == 0
    ni = te // nk
    row_tab = lambda a: a.reshape(heads, nk, 1, t)
    row_spec = pl.BlockSpec((heads, ni, 1, tb), lambda i, e: (0, e, 0, i))
    col_spec = pl.BlockSpec((heads, nk, tb), lambda i, e: (0, 0, i))
    return pl.pallas_call(
        functools.partial(_peer_mix_kernel, dims=dims, sub=sub),
        grid=(t // tb, dims.n_experts // te),
        in_specs=[pl.BlockSpec((tb, d), lambda i, e: (i, 0)),
                  pl.BlockSpec((te, d), lambda i, e: (e, 0)),
                  pl.BlockSpec((te, d), lambda i, e: (e, 0)),
                  row_spec, row_spec, col_spec, col_spec],
        out_specs=pl.BlockSpec((tb, d), lambda i, e: (i, 0)),
        out_shape=jax.ShapeDtypeStruct((t, d), F32),
        scratch_shapes=[pltpu.VMEM((te, tb), BF16), pltpu.VMEM((te, tb), BF16)],
        compiler_params=_params("parallel", "arbitrary"),
        name="peer_mix",
    )(xn, u, v, row_tab(c1), row_tab(e1), r2, e2)


def _pad_prev(state, width):
    return jnp.pad(state, ((0, 0), (SUBLANES - (width - 1), 0), (0, 0)))


def forward(dims, x_prompt, x_sample, mem_prompt, cache_mem_k, cache_mem_v, state_conv_a,
            state_ssd_conv, state_ssd, norm_mix, norm_mem, norm_ffn, norm_final, w_in,
            a_conv_w, a_out, ssd_conv_w, ssd_conv_b, ssd_dt_bias, ssd_a_log, ssd_d, ssd_norm,
            ssd_out, w_mem_k, w_mem_v, xatt_out, w_o, peer_wq, peer_subkeys, peer_u, peer_v):
    d = dims.d_model
    tp, ts, t = dims.t_prompt, dims.t_sample, dims.tokens
    g, rh, p, n = dims.ssd_groups, dims.heads_per_group, dims.ssd_head_dim, dims.ssd_state
    bf = lambda a: a.astype(BF16)

    xp2, xs2 = x_prompt.reshape(tp, d), x_sample.reshape(ts, d)

    mn = rmsnorm(mem_prompt.reshape(dims.batch * dims.n_mem, d), norm_mem[0], BF16)
    xw = dims.xatt_width
    k_p = matmul(mn, w_mem_k, F32, tn=512, name="mem_k").reshape(dims.batch, dims.n_mem, xw)
    v_p = matmul(mn, w_mem_v, F32, tn=512, name="mem_v").reshape(dims.batch, dims.n_mem, xw)
    mem_k_p = k_p.reshape(1, dims.batch, dims.n_mem, dims.xatt_heads, dims.xatt_head_dim)
    mem_v_p = v_p.reshape(1, dims.batch, dims.n_mem, dims.xatt_heads, dims.xatt_head_dim)

    xn = rmsnorm_parts(xp2, xs2, norm_mix[0], BF16)
    wt = jnp.swapaxes(w_in, 1, 2)[0]
    dt0 = 3 * dims.a_width + dims.ssd_inner + dims.ssd_xbc
    proj = matmul_wt(xn, wt, [(0, dt0), (dt0 + dims.ssd_heads, dims.proj_width - dt0)], tm=512, tn=1024,
                     name="in_proj")
    dt_t = dt_proj_t(xn, wt, dt0, dims.ssd_heads).reshape(g, rh, t)

    a_cols = (dims.off_ain, dims.off_abg, dims.off_acg)
    zeros_a = jnp.zeros((dims.batch, SUBLANES, dims.a_width), F32)
    va_p, st_a_p = causal_conv(proj, zeros_a, a_conv_w[0], row0=0, nseq=dims.batch, length=dims.seq,
                               cols=dims.a_width, col_offs=a_cols, gated=True, out_dtype=BF16)
    prev_a = _pad_prev(state_conv_a[0], dims.a_conv).reshape(dims.dec_batch * SUBLANES, dims.a_width)
    va_s, u_s = short_seq_conv(proj, prev_a, a_conv_w[0], row0=tp, nseq=dims.dec_batch, seq=dims.dec_seq,
                               cols=dims.a_width, col_offs=a_cols, gated=True, out_dtype=BF16)
    na = dims.a_conv - 1
    conv_a_p = st_a_p[None, :, SUBLANES - na:, :]
    conv_a_s = u_s.reshape(dims.dec_batch, dims.dec_seq, dims.a_width)[None, :, dims.dec_seq - na:, :]

    zeros_b = jnp.zeros((dims.batch, SUBLANES, dims.ssd_xbc), F32)
    xbc_p = causal_conv(proj, zeros_b, ssd_conv_w[0], row0=0, nseq=dims.batch, length=dims.seq,
                        cols=dims.ssd_xbc, col_offs=(dims.off_xbc,), gated=False, bias=ssd_conv_b[0])
    prev_b = _pad_prev(state_ssd_conv[0], dims.ssd_conv).reshape(dims.dec_batch * SUBLANES, dims.ssd_xbc)
    xbc_s = short_seq_conv(proj, prev_b, ssd_conv_w[0], row0=tp, nseq=dims.dec_batch, seq=dims.dec_seq,
                           cols=dims.ssd_xbc, col_offs=(dims.off_xbc,), gated=False, bias=ssd_conv_b[0])
    nb = dims.ssd_conv - 1
    x0, x1c = dims.off_xbc, dims.off_xbc + dims.ssd_xbc
    ssd_conv_p = jnp.stack([lax.slice(proj, ((b + 1) * dims.seq - nb, x0), ((b + 1) * dims.seq, x1c))
                            for b in range(dims.batch)])[None]
    ssd_conv_s = lax.slice(proj.reshape(t // dims.dec_seq, dims.dec_seq, dims.proj_width),
                           (tp // dims.dec_seq, dims.dec_seq - nb, x0),
                           (t // dims.dec_seq, dims.dec_seq, x1c))[None]

    bias_g = ssd_dt_bias[0].reshape(g, rh, 1)
    alog_g = ssd_a_log[0].reshape(g, rh, 1)
    d_exp = jnp.repeat(ssd_d[0], p).reshape(1, dims.ssd_inner)
    norm_w = ssd_norm[0].reshape(1, dims.ssd_inner)
    yb_p, h_p = ssd_block(dims, xbc_p, proj, dt_t, bias_g, alog_g, d_exp, norm_w, row0=0, nrows=tp, chained=True)
    yb_s, h_s = ssd_block(dims, xbc_s, proj, dt_t, bias_g, alog_g, d_exp, norm_w, row0=tp, nrows=ts,
                          chained=False, h0=state_ssd[0])

    oc_p = cross_attention(dims, proj, k_p, v_p, row0=0, nseq=dims.batch, length=dims.seq)
    oc_s = cross_attention(dims, proj, cache_mem_k[0], cache_mem_v[0], row0=tp, nseq=dims.dec_batch,
                           length=dims.dec_seq)

    mix = merge_branches(dims, (va_p, yb_p, oc_p), (va_s, yb_s, oc_s), bf(a_out[0]), bf(ssd_out[0]),
                         bf(xatt_out[0]), proj)
    x1 = matmul(mix, bf(w_o[0]), F32, residual_parts=(xp2, xs2), name="out_proj")

    xn2 = rmsnorm(x1, norm_ffn[0], BF16)
    qp = matmul(xn2, bf(peer_wq[0]), F32, name="peer_query")
    c1, e1, r2, e2 = peer_select(dims, qp, peer_subkeys[0])
    ffn = peer_mix(dims, xn2, cast_layer(peer_u), cast_layer(peer_v), c1, e1, r2, e2)
    y_prompt = rmsnorm(x1, norm_final, F32, residual=ffn, row0=0, nrows=tp).reshape(dims.batch, dims.seq, d)
    y_sample = rmsnorm(x1, norm_final, F32, residual=ffn, row0=tp, nrows=ts).reshape(
        dims.dec_batch, dims.dec_seq, d)
    return (y_prompt, y_sample, mem_k_p, mem_v_p, conv_a_p, ssd_conv_p, h_p[None],
            conv_a_s, ssd_conv_s, h_s[None])


def kernel(x_prompt, x_sample, mem_prompt, cache_mem_k, cache_mem_v, state_conv_a, state_ssd_conv, state_ssd, norm_mix, norm_mem, norm_ffn, norm_final, w_in, a_conv_w, a_out, ssd_conv_w, ssd_conv_b, ssd_dt_bias, ssd_a_log, ssd_d, ssd_norm, ssd_out, w_mem_k, w_mem_v, xatt_out, w_o, peer_wq, peer_subkeys, peer_u, peer_v):
    return forward(FULL, x_prompt, x_sample, mem_prompt, cache_mem_k, cache_mem_v, state_conv_a,
                   state_ssd_conv, state_ssd, norm_mix, norm_mem, norm_ffn, norm_final, w_in,
                   a_conv_w, a_out, ssd_conv_w, ssd_conv_b, ssd_dt_bias, ssd_a_log, ssd_d, ssd_norm,
                   ssd_out, w_mem_k, w_mem_v, xatt_out, w_o, peer_wq, peer_subkeys, peer_u, peer_v)
```

```python
import dataclasses
import functools

import jax
import jax.numpy as jnp
import numpy as np
from jax import lax
from jax.experimental import pallas as pl
from jax.experimental.pallas import tpu as pltpu

F32 = jnp.float32
BF16 = jnp.bfloat16
EPS = 1e-6
HIGHEST = lax.Precision.HIGHEST
NEG_INF = float("-inf")

LANES = 128
SUBLANES = 8
VMEM_LIMIT_BYTES = 56 * 1024 * 1024


@dataclasses.dataclass(frozen=True)
class Dims:
    d_model: int = 4096
    batch: int = 4
    seq: int = 2048
    dec_batch: int = 128
    dec_seq: int = 8
    a_width: int = 2048
    a_conv: int = 3
    ssd_inner: int = 4096
    ssd_head_dim: int = 64
    ssd_groups: int = 8
    ssd_state: int = 128
    ssd_conv: int = 4
    ssd_chunk: int = 128
    n_mem: int = 256
    xatt_heads: int = 4
    xatt_head_dim: int = 512
    peer_heads: int = 8
    peer_keys: int = 128
    peer_topk: int = 16
    peer_qdim: int = 256

    @property
    def ssd_heads(self):
        return self.ssd_inner // self.ssd_head_dim

    @property
    def heads_per_group(self):
        return self.ssd_heads // self.ssd_groups

    @property
    def group_width(self):
        return self.ssd_inner // self.ssd_groups

    @property
    def ssd_xbc(self):
        return self.ssd_inner + 2 * self.ssd_groups * self.ssd_state

    @property
    def xatt_width(self):
        return self.xatt_heads * self.xatt_head_dim

    @property
    def n_experts(self):
        return self.peer_keys * self.peer_keys

    @property
    def t_prompt(self):
        return self.batch * self.seq

    @property
    def t_sample(self):
        return self.dec_batch * self.dec_seq

    @property
    def tokens(self):
        return self.t_prompt + self.t_sample

    @property
    def off_ain(self):
        return 0

    @property
    def off_abg(self):
        return self.a_width

    @property
    def off_acg(self):
        return 2 * self.a_width

    @property
    def off_z(self):
        return 3 * self.a_width

    @property
    def off_xbc(self):
        return self.off_z + self.ssd_inner

    @property
    def off_q(self):
        return self.off_xbc + self.ssd_xbc

    @property
    def off_gates(self):
        return self.off_q + self.xatt_width

    @property
    def proj_width(self):
        return self.off_gates + 3 * self.d_model


FULL = Dims()


def _params(*sem):
    return pltpu.CompilerParams(dimension_semantics=sem, vmem_limit_bytes=VMEM_LIMIT_BYTES)


def _pick(n, pref):
    t = min(n, pref)
    while n % t:
        t //= 2
    return t


def _rmsnorm_rows(x, g_ref, o_ref):
    inv = lax.rsqrt(jnp.mean(x * x, axis=-1, keepdims=True) + EPS)
    o_ref[...] = ((x * inv) * g_ref[...]).astype(o_ref.dtype)


def _rmsnorm_kernel(x_ref, g_ref, o_ref):
    _rmsnorm_rows(x_ref[...], g_ref, o_ref)


def _add_rmsnorm_kernel(x_ref, r_ref, g_ref, o_ref):
    _rmsnorm_rows(x_ref[...] + r_ref[...], g_ref, o_ref)


def _rmsnorm_parts_kernel(xp_ref, xs_ref, g_ref, o_ref, *, npb):
    i = pl.program_id(0)

    @pl.when(i < npb)
    def _():
        _rmsnorm_rows(xp_ref[...], g_ref, o_ref)

    @pl.when(i >= npb)
    def _():
        _rmsnorm_rows(xs_ref[...], g_ref, o_ref)


def rmsnorm(x, g, out_dtype, residual=None, row0=0, nrows=None, tm=256):
    m, d = x.shape
    nrows = m - row0 if nrows is None else nrows
    tm = _pick(nrows, tm)
    rb0 = row0 // tm
    assert row0 % tm == 0
    row = pl.BlockSpec((tm, d), lambda i: (rb0 + i, 0))
    gspec = pl.BlockSpec((1, d), lambda i: (0, 0))
    args = (x,) if residual is None else (x, residual)
    return pl.pallas_call(
        _rmsnorm_kernel if residual is None else _add_rmsnorm_kernel,
        grid=(nrows // tm,),
        in_specs=[row] * len(args) + [gspec],
        out_specs=pl.BlockSpec((tm, d), lambda i: (i, 0)),
        out_shape=jax.ShapeDtypeStruct((nrows, d), out_dtype),
        compiler_params=_params("parallel"),
        name="rmsnorm" if residual is None else "add_rmsnorm",
    )(*args, g.reshape(1, d))


def rmsnorm_parts(xp, xs, g, out_dtype, tm=256):
    d = xp.shape[1]
    tm = _pick(xs.shape[0], _pick(xp.shape[0], tm))
    npb, nsb = xp.shape[0] // tm, xs.shape[0] // tm
    return pl.pallas_call(
        functools.partial(_rmsnorm_parts_kernel, npb=npb),
        grid=(npb + nsb,),
        in_specs=[pl.BlockSpec((tm, d), lambda i: (jnp.minimum(i, npb - 1), 0)),
                  pl.BlockSpec((tm, d), lambda i: (jnp.maximum(i - npb, 0), 0)),
                  pl.BlockSpec((1, d), lambda i: (0, 0))],
        out_specs=pl.BlockSpec((tm, d), lambda i: (i, 0)),
        out_shape=jax.ShapeDtypeStruct((xp.shape[0] + xs.shape[0], d), out_dtype),
        compiler_params=_params("arbitrary"),
        name="rmsnorm_parts",
    )(xp, xs, g.reshape(1, d))


def _cast_kernel(x_ref, o_ref):
    o_ref[...] = x_ref[0].astype(o_ref.dtype)


def cast_layer(w, dtype=BF16, tr=512):
    _, r, c = w.shape
    tr = _pick(r, tr)
    return pl.pallas_call(
        _cast_kernel,
        grid=(r // tr,),
        in_specs=[pl.BlockSpec((1, tr, c), lambda i: (0, i, 0))],
        out_specs=pl.BlockSpec((tr, c), lambda i: (i, 0)),
        out_shape=jax.ShapeDtypeStruct((r, c), dtype),
        compiler_params=_params("parallel"),
        name="cast_layer",
    )(w)


def _mm_kernel(*refs, npb, layer_weight):
    a_ref, b_ref = refs[:2]
    refs = refs[2:]
    if npb is not None:
        rp_ref, rs_ref = refs[:2]
        refs = refs[2:]
    o_ref = refs[0]
    i = pl.program_id(1)
    if layer_weight:
        w_scr = refs[1]

        @pl.when(i == 0)
        def _():
            w_scr[...] = b_ref[0].astype(BF16)

        w = w_scr[...]
    else:
        w = b_ref[...]
    acc = jnp.dot(a_ref[...], w, preferred_element_type=F32)
    if npb is None:
        o_ref[...] = acc.astype(o_ref.dtype)
    else:
        @pl.when(i < npb)
        def _():
            o_ref[...] = (rp_ref[...] + acc).astype(o_ref.dtype)

        @pl.when(i >= npb)
        def _():
            o_ref[...] = (rs_ref[...] + acc).astype(o_ref.dtype)


def matmul(a, b, out_dtype, residual_parts=None, tm=512, tn=1024, name="matmul"):
    m, k = a.shape
    layer_weight = b.ndim == 3
    n = b.shape[-1]
    tm, tn = _pick(m, tm), _pick(n, tn)
    npb = None
    extra_specs, extra_args = [], []
    if residual_parts is not None:
        rp, rs = residual_parts
        tm = _pick(rs.shape[0], _pick(rp.shape[0], tm))
        npb = rp.shape[0] // tm
        extra_specs = [pl.BlockSpec((tm, tn), lambda j, i: (jnp.minimum(i, npb - 1), j)),
                       pl.BlockSpec((tm, tn), lambda j, i: (jnp.maximum(i - npb, 0), j))]
        extra_args = [rp, rs]
    if layer_weight:
        b_spec = pl.BlockSpec((1, k, tn), lambda j, i: (0, 0, j))
        scratch = [pltpu.VMEM((k, tn), BF16)]
    else:
        b_spec = pl.BlockSpec((k, tn), lambda j, i: (0, j))
        scratch = []
    sequential = layer_weight or npb is not None
    return pl.pallas_call(
        functools.partial(_mm_kernel, npb=npb, layer_weight=layer_weight),
        grid=(n // tn, m // tm),
        in_specs=[pl.BlockSpec((tm, k), lambda j, i: (i, 0)), b_spec] + extra_specs,
        out_specs=pl.BlockSpec((tm, tn), lambda j, i: (i, j)),
        out_shape=jax.ShapeDtypeStruct((m, n), out_dtype),
        scratch_shapes=scratch,
        compiler_params=_params(*(("arbitrary", "arbitrary") if sequential else ("parallel", "parallel"))),
        name=name,
    )(a, b, *extra_args)


def _mm_wt_kernel(a_ref, wt_ref, o_ref, w_scr):
    @pl.when(pl.program_id(1) == 0)
    def _():
        w_scr[...] = wt_ref[...].astype(BF16)

    o_ref[...] = lax.dot_general(a_ref[...], w_scr[...], (((1,), (1,)), ((), ())),
                                 preferred_element_type=F32).astype(o_ref.dtype)


def matmul_wt(a, wt, segments, out_dtype=F32, tm=1024, tn=512, name="matmul_wt"):
    m, k = a.shape
    tm = _pick(m, tm)
    for start, length in segments:
        tn = _pick(length, tn)
        assert start % SUBLANES == 0
    assert all(length % tn == 0 for _, length in segments)
    starts = np.concatenate([np.arange(s, s + l, tn) for s, l in segments]).astype(np.int32)
    nblk = len(starts)
    bounds = np.cumsum([l // tn for _, l in segments])[:-1]
    shifts = [segments[i + 1][0] - (segments[i][0] + segments[i][1]) for i in range(len(segments) - 1)]

    def row_start(j):
        r = segments[0][0] + j * tn
        for b, sh in zip(bounds, shifts):
            r = r + jnp.where(j >= b, sh, 0)
        return pl.multiple_of(r, SUBLANES)

    return pl.pallas_call(
        _mm_wt_kernel,
        grid=(nblk, m // tm),
        in_specs=[pl.BlockSpec((tm, k), lambda j, i: (i, 0)),
                  pl.BlockSpec((pl.Element(tn), pl.Element(k)), lambda j, i: (row_start(j), 0))],
        out_specs=pl.BlockSpec((tm, tn), lambda j, i: (i, j)),
        out_shape=jax.ShapeDtypeStruct((m, nblk * tn), out_dtype),
        scratch_shapes=[pltpu.VMEM((tn, k), BF16)],
        compiler_params=_params("arbitrary", "arbitrary"),
        name=name,
    )(a, wt)


def _dt_proj_kernel(a_ref, wt_ref, o_ref):
    o_ref[...] = lax.dot_general(wt_ref[...].astype(BF16), a_ref[...], (((1,), (1,)), ((), ())),
                                 preferred_element_type=F32)


def dt_proj_t(a, wt, start, nrows, tm=1024):
    m, k = a.shape
    tm = _pick(m, tm)
    assert start % SUBLANES == 0
    return pl.pallas_call(
        _dt_proj_kernel,
        grid=(m // tm,),
        in_specs=[pl.BlockSpec((tm, k), lambda i: (i, 0)),
                  pl.BlockSpec((pl.Element(nrows), pl.Element(k)), lambda i: (start, 0))],
        out_specs=pl.BlockSpec((nrows, tm), lambda i: (0, i)),
        out_shape=jax.ShapeDtypeStruct((nrows, m), F32),
        compiler_params=_params("parallel"),
        name="dt_proj",
    )(a, wt)


def _conv_kernel(*refs, width, tl, gated):
    if gated:
        in_ref, bg_ref, cg_ref, prev_ref, w_ref, o_ref, st_ref, scr = refs
    else:
        in_ref, prev_ref, w_ref, bias_ref, o_ref, scr = refs
    lt = pl.program_id(2)

    @pl.when(lt == 0)
    def _():
        scr[0:SUBLANES, :] = prev_ref[0]

    u = cg_ref[...] * in_ref[...] if gated else in_ref[...]
    scr[SUBLANES:SUBLANES + tl, :] = u
    acc = w_ref[width - 1:width, :] * u
    for k in range(width - 1):
        lo = SUBLANES - (width - 1 - k)
        acc = acc + w_ref[k:k + 1, :] * scr[lo:lo + tl, :]
    tail = scr[tl:tl + SUBLANES, :]
    scr[0:SUBLANES, :] = tail
    if gated:
        o_ref[...] = (bg_ref[...] * acc).astype(o_ref.dtype)
        st_ref[0] = tail
    else:
        y = acc + bias_ref[...]
        o_ref[...] = (y * jax.nn.sigmoid(y)).astype(o_ref.dtype)


def causal_conv(proj, prev8, w, *, row0, nseq, length, cols, col_offs, gated, bias=None, out_dtype=F32,
                tl=1024, tc=512):
    width = w.shape[0]
    tl, tc = _pick(length, tl), _pick(cols, tc)
    while any(o % tc for o in col_offs):
        tc //= 2
    nl = length // tl
    rb0 = row0 // tl
    assert row0 % tl == 0 and all(o % tc == 0 for o in col_offs)

    def blk(off):
        return pl.BlockSpec((tl, tc), lambda s, c, l, off=off: (rb0 + s * nl + l, off // tc + c))

    prev_spec = pl.BlockSpec((1, SUBLANES, tc), lambda s, c, l: (s, 0, c))
    w_spec = pl.BlockSpec((width, tc), lambda s, c, l: (0, c))
    out_spec = pl.BlockSpec((tl, tc), lambda s, c, l: (s * nl + l, c))
    out_shape = jax.ShapeDtypeStruct((nseq * length, cols), out_dtype)
    if gated:
        in_specs = [blk(col_offs[0]), blk(col_offs[1]), blk(col_offs[2]), prev_spec, w_spec]
        args = (proj, proj, proj, prev8, w)
        out_specs = [out_spec, pl.BlockSpec((1, SUBLANES, tc), lambda s, c, l: (s, 0, c))]
        out_shape = [out_shape, jax.ShapeDtypeStruct((nseq, SUBLANES, cols), F32)]
    else:
        in_specs = [blk(col_offs[0]), prev_spec, w_spec, pl.BlockSpec((1, tc), lambda s, c, l: (0, c))]
        args = (proj, prev8, w, bias.reshape(1, cols))
        out_specs = out_spec
    return pl.pallas_call(
        functools.partial(_conv_kernel, width=width, tl=tl, gated=gated),
        grid=(nseq, cols // tc, nl),
        in_specs=in_specs,
        out_specs=out_specs,
        out_shape=out_shape,
        scratch_shapes=[pltpu.VMEM((tl + SUBLANES, tc), F32)],
        compiler_params=_params("parallel", "parallel", "arbitrary"),
        name="gated_conv" if gated else "ssd_conv",
    )(*args)


def _short_conv_kernel(*refs, width, seq, gated):
    if gated:
        in_ref, bg_ref, cg_ref, prev_ref, w_ref, o_ref, u_ref, scr_u, scr_p = refs
    else:
        in_ref, prev_ref, w_ref, bias_ref, o_ref, scr_u, scr_p = refs
    rows = in_ref.shape[0]
    u = cg_ref[...] * in_ref[...] if gated else in_ref[...]
    zeros = jnp.zeros((SUBLANES, u.shape[1]), F32)
    scr_u[0:SUBLANES, :] = zeros
    scr_u[SUBLANES:SUBLANES + rows, :] = u
    scr_p[0:rows, :] = prev_ref[...]
    scr_p[rows:rows + SUBLANES, :] = zeros
    pos = lax.broadcasted_iota(jnp.int32, u.shape, 0) % seq
    acc = w_ref[width - 1:width, :] * u
    for k in range(width - 1):
        shift = width - 1 - k
        lo = SUBLANES - shift
        operand = jnp.where(pos >= shift, scr_u[lo:lo + rows, :], scr_p[lo:lo + rows, :])
        acc = acc + w_ref[k:k + 1, :] * operand
    if gated:
        o_ref[...] = (bg_ref[...] * acc).astype(o_ref.dtype)
        u_ref[...] = u
    else:
        y = acc + bias_ref[...]
        o_ref[...] = (y * jax.nn.sigmoid(y)).astype(o_ref.dtype)


def short_seq_conv(proj, prev8, w, *, row0, nseq, seq, cols, col_offs, gated, bias=None, out_dtype=F32,
                   rows=128, tc=2048):
    width = w.shape[0]
    assert seq == SUBLANES and width - 1 <= seq
    total = nseq * seq
    rows, tc = _pick(total, rows), _pick(cols, tc)
    while any(o % tc for o in col_offs):
        tc //= 2
    rb0 = row0 // rows
    assert row0 % rows == 0

    def blk(off):
        return pl.BlockSpec((rows, tc), lambda r, c, off=off: (rb0 + r, off // tc + c))

    own = pl.BlockSpec((rows, tc), lambda r, c: (r, c))
    w_spec = pl.BlockSpec((width, tc), lambda r, c: (0, c))
    out_shape = jax.ShapeDtypeStruct((total, cols), out_dtype)
    if gated:
        in_specs = [blk(col_offs[0]), blk(col_offs[1]), blk(col_offs[2]), own, w_spec]
        args = (proj, proj, proj, prev8, w)
        out_specs = [own, own]
        out_shape = [out_shape, jax.ShapeDtypeStruct((total, cols), F32)]
    else:
        in_specs = [blk(col_offs[0]), own, w_spec, pl.BlockSpec((1, tc), lambda r, c: (0, c))]
        args = (proj, prev8, w, bias.reshape(1, cols))
        out_specs = own
    return pl.pallas_call(
        functools.partial(_short_conv_kernel, width=width, seq=seq, gated=gated),
        grid=(total // rows, cols // tc),
        in_specs=in_specs,
        out_specs=out_specs,
        out_shape=out_shape,
        scratch_shapes=[pltpu.VMEM((rows + SUBLANES, tc), F32), pltpu.VMEM((rows + SUBLANES, tc), F32)],
        compiler_params=_params("parallel", "parallel"),
        name="gated_conv_short" if gated else "ssd_conv_short",
    )(*args)


def _softplus(x):
    return jnp.maximum(x, 0.0) + jnp.log1p(jnp.exp(-jnp.abs(x)))


def _ssd_kernel(*refs, nseq, dims, chained, gpb):
    q = dims.ssd_chunk
    r_heads = dims.heads_per_group
    p = dims.ssd_head_dim
    gw = dims.group_width
    n = dims.ssd_state
    seg = q // nseq
    if chained:
        (xs_raw, b_raw, c_raw, z_ref, dt_ref, bias_ref, alog_ref, d_ref, nw_ref, sel_ref,
         wx_ref, wb_ref, wc_ref, cbx_ref, cbb_ref, cbc_ref,
         y_ref, hout_ref, h_scr, xs_ref, b_ref, c_ref, tail_x, tail_b, tail_c) = refs
        first = pl.program_id(2) == 0

        @pl.when(first)
        def _():
            h_scr[...] = jnp.zeros_like(h_scr)
            for tail in (tail_x, tail_b, tail_c):
                tail[0:SUBLANES, :] = jnp.zeros((SUBLANES, tail.shape[1]), F32)

        width = wx_ref.shape[0]
        for raw, w_ref, cb_ref, tail, out in ((xs_raw, wx_ref, cbx_ref, tail_x, xs_ref),
                                              (b_raw, wb_ref, cbb_ref, tail_b, b_ref),
                                              (c_raw, wc_ref, cbc_ref, tail_c, c_ref)):
            u = raw[...]
            tail[SUBLANES:SUBLANES + q, :] = u
            acc = w_ref[width - 1:width, :] * u
            for k in range(width - 1):
                lo = SUBLANES - (width - 1 - k)
                acc = acc + w_ref[k:k + 1, :] * tail[lo:lo + q, :]
            tail[0:SUBLANES, :] = tail[q:q + SUBLANES, :]
            yv = acc + cb_ref[...]
            out[...] = yv * jax.nn.sigmoid(yv)
    else:
        (xs_ref, b_ref, c_ref, z_ref, dt_ref, bias_ref, alog_ref, d_ref, nw_ref, sel_ref, h0_ref,
         y_ref, hout_ref) = refs

    for gi in range(gpb):
        gcols = slice(gi * gw, (gi + 1) * gw)
        ncols = slice(gi * n, (gi + 1) * n)
        _ssd_group(
            xs_ref.at[:, gcols], b_ref.at[:, ncols], c_ref.at[:, ncols], z_ref.at[:, gcols],
            dt_ref.at[gi], bias_ref.at[gi], alog_ref.at[gi], d_ref.at[:, gcols], nw_ref.at[:, gcols], sel_ref,
            y_ref.at[:, gcols],
            hout_ref.at[:, gi * r_heads:(gi + 1) * r_heads],
            h_scr.at[gi * r_heads * p:(gi + 1) * r_heads * p, :] if chained else None,
            None if chained else h0_ref.at[:, gi * r_heads:(gi + 1) * r_heads],
            nseq=nseq, dims=dims, chained=chained)


def _ssd_group(xs_ref, b_ref, c_ref, z_ref, dt_ref, bias_ref, alog_ref, d_ref, nw_ref, sel_ref, y_ref,
               hout_ref, h_scr, h0_ref, *, nseq, dims, chained):
    q = dims.ssd_chunk
    r_heads = dims.heads_per_group
    p = dims.ssd_head_dim
    gw = dims.group_width
    seg = q // nseq
    row = lax.broadcasted_iota(jnp.int32, (q, q), 0)
    col = lax.broadcasted_iota(jnp.int32, (q, q), 1)
    same = (row // seg) == (col // seg)
    dt = _softplus(dt_ref[...] + bias_ref[...])
    adt = dt * (-jnp.exp(alog_ref[...]))
    cum_mask = jnp.where(same & (row <= col), 1.0, 0.0).astype(F32)
    acum_t = jnp.dot(adt, cum_mask, precision=HIGHEST, preferred_element_type=F32)
    if nseq == 1:
        atot_t = jnp.broadcast_to(acum_t[:, q - 1:q], (r_heads, q))
    else:
        atot_t = jnp.dot(adt, jnp.where(same, 1.0, 0.0).astype(F32), precision=HIGHEST,
                         preferred_element_type=F32)
    stack = jnp.concatenate(
        [acum_t, dt, dt * jnp.exp(atot_t - acum_t), jnp.exp(acum_t),
         jnp.zeros((q - 4 * r_heads, q), F32)], axis=0)
    cols_form = stack.T
    sel = sel_ref[...]
    hi = cols_form.astype(BF16)
    rest = cols_form - hi.astype(F32)
    mid = rest.astype(BF16)
    low = (rest - mid.astype(F32)).astype(BF16)
    expand = (jnp.dot(hi, sel, preferred_element_type=F32) + jnp.dot(mid, sel, preferred_element_type=F32)
              + jnp.dot(low, sel, preferred_element_type=F32))
    e_dt, e_st, e_ac = expand[:, :gw], expand[:, gw:2 * gw], expand[:, 2 * gw:]

    x = xs_ref[...]
    bb = b_ref[...].astype(BF16)
    cb_ = c_ref[...].astype(BF16)
    cb = lax.dot_general(cb_, bb, (((1,), (1,)), ((), ())), preferred_element_type=F32)
    x_dt = x * e_dt
    x_dt_b = x_dt.astype(BF16)
    causal = same & (row >= col)
    lane = lax.broadcasted_iota(jnp.int32, (q, LANES), 1)
    heads_per_tile = LANES // p

    y_tiles = []
    for tile in range(gw // LANES):
        xt = x_dt_b[:, tile * LANES:(tile + 1) * LANES]
        acc = None
        for k in range(heads_per_tile):
            r = tile * heads_per_tile + k
            segm = cols_form[:, r:r + 1] - acum_t[r:r + 1, :]
            decay = jnp.exp(jnp.where(causal, segm, NEG_INF))
            m_r = (cb * decay).astype(BF16)
            x_r = jnp.where((lane >= k * p) & (lane < (k + 1) * p), xt, jnp.zeros_like(xt))
            part = jnp.dot(m_r, x_r, preferred_element_type=F32)
            acc = part if acc is None else acc + part
        y_tiles.append(acc)
    y = jnp.concatenate(y_tiles, axis=1)

    x_st = x * e_st
    eac_t = jnp.exp(atot_t)

    if chained:
        h = h_scr[...]
        y_off = lax.dot_general(cb_, h.astype(BF16), (((1,), (1,)), ((), ())),
                                preferred_element_type=F32)
        y = y + y_off * e_ac
        s_new = lax.dot_general(x_st.astype(BF16), bb, (((0,), (0,)), ((), ())),
                                preferred_element_type=F32)
        scale = jnp.broadcast_to(eac_t[:, q - 1:q], (r_heads, LANES))
        for r in range(r_heads):
            rows = slice(r * p, (r + 1) * p)
            h_scr[rows, :] = h[rows, :] * jnp.broadcast_to(scale[r:r + 1, :], (p, LANES)) + s_new[rows, :]
        for r in range(r_heads):
            hout_ref[0, r] = h_scr[r * p:(r + 1) * p, :]
    else:
        x_st_t = x_st.T
        rows_q = lax.broadcasted_iota(jnp.int32, (q, gw), 0)
        lanes_q = lax.broadcasted_iota(jnp.int32, (gw, q), 1)
        y_off = jnp.zeros((q, gw), F32)
        for s in range(nseq):
            h_s = h0_ref[s].reshape(r_heads * p, dims.ssd_state)
            y_s = lax.dot_general(cb_, h_s.astype(BF16), (((1,), (1,)), ((), ())),
                                  preferred_element_type=F32)
            y_off = jnp.where(rows_q // seg == s, y_s, y_off)
            xs_s = jnp.where(lanes_q // seg == s, x_st_t, 0.0).astype(BF16)
            s_new = jnp.dot(xs_s, bb, preferred_element_type=F32)
            scale = jnp.broadcast_to(eac_t[:, s * seg:s * seg + 1], (r_heads, LANES))
            for r in range(r_heads):
                rows = slice(r * p, (r + 1) * p)
                hout_ref[s, r] = (h_s[rows, :] * jnp.broadcast_to(scale[r:r + 1, :], (p, LANES))
                                  + s_new[rows, :])
        y = y + y_off * e_ac

    y = y + d_ref[...] * x
    z = z_ref[...]
    y = y * (z * jax.nn.sigmoid(z))
    inv = lax.rsqrt(jnp.mean(y * y, axis=-1, keepdims=True) + EPS)
    y_ref[...] = ((y * inv) * nw_ref[...]).astype(y_ref.dtype)


def _ssd_selector(dims):
    r_heads, p, gw = dims.heads_per_group, dims.ssd_head_dim, dims.group_width
    sel = np.zeros((dims.ssd_chunk, 3 * gw), np.float32)
    for part in range(3):
        for r in range(r_heads):
            sel[(part + 1) * r_heads + r, part * gw + r * p:part * gw + (r + 1) * p] = 1.0
    return jnp.asarray(sel, BF16)


def ssd_block(dims, xbc, proj, dt_t, bias_g, alog_g, d_exp, norm_w, *, row0, nrows, chained, h0=None,
              conv=None):
    q = dims.ssd_chunk
    g = dims.ssd_groups
    gw = dims.group_width
    n = dims.ssd_state
    r_heads, p = dims.heads_per_group, dims.ssd_head_dim
    assert LANES % p == 0 and q == LANES and 4 * r_heads <= q
    sel = _ssd_selector(dims)
    rb0 = row0 // q
    zoff = dims.off_z // gw
    boff = dims.ssd_inner // n
    coff = boff + g

    gpb = 1
    if chained:
        for cand_gpb in (4, 2):
            if all(v % cand_gpb == 0 for v in (g, zoff, boff)):
                gpb = cand_gpb
                break
    gblocks = g // gpb
    if chained:
        nb, nc = dims.batch, dims.seq // q
        grid = (nb, gblocks, nc)
        rowblk = lambda b, gi, c: b * nc + c
        sems = ("parallel", "parallel", "arbitrary")
        nseq = 1
    else:
        nseq = q // dims.dec_seq
        nb = dims.dec_batch // nseq
        grid = (nb, gblocks)
        rowblk = lambda b, gi: b
        sems = ("parallel", "parallel")

    def spec(shape, fn):
        return pl.BlockSpec(shape, fn)

    if chained:
        ix = lambda f: (lambda b, gi, c: f(rowblk(b, gi, c), gi))
    else:
        ix = lambda f: (lambda b, gi: f(rowblk(b, gi), gi))

    in_specs = [
        spec((q, gpb * gw), ix(lambda rb, gi: (rb, gi))),
        spec((q, gpb * n), ix(lambda rb, gi: (rb, boff // gpb + gi))),
        spec((q, gpb * n), ix(lambda rb, gi: (rb, coff // gpb + gi))),
        spec((q, gpb * gw), ix(lambda rb, gi: (rb0 + rb, zoff // gpb + gi))),
        spec((gpb, r_heads, q), ix(lambda rb, gi: (gi, 0, rb0 + rb))),
        spec((gpb, r_heads, 1), ix(lambda rb, gi: (gi, 0, 0))),
        spec((gpb, r_heads, 1), ix(lambda rb, gi: (gi, 0, 0))),
        spec((1, gpb * gw), ix(lambda rb, gi: (0, gi))),
        spec((1, gpb * gw), ix(lambda rb, gi: (0, gi))),
        spec((q, 3 * gw), ix(lambda rb, gi: (0, 0))),
    ]
    args = [xbc, xbc, xbc, proj, dt_t, bias_g, alog_g, d_exp, norm_w, sel]
    y_spec = spec((q, gpb * gw), ix(lambda rb, gi: (rb, gi)))
    if chained:
        conv_w, conv_b = conv
        xw, bw = gpb * gw, gpb * n
        assert dims.off_xbc % xw == 0 and (dims.off_xbc + dims.ssd_inner) % bw == 0 and (g * n) % bw == 0
        x0, b0 = dims.off_xbc // xw, (dims.off_xbc + dims.ssd_inner) // bw
        c0 = b0 + (g * n) // bw
        in_specs[0] = spec((q, xw), ix(lambda rb, gi: (rb0 + rb, x0 + gi)))
        in_specs[1] = spec((q, bw), ix(lambda rb, gi: (rb0 + rb, b0 + gi)))
        in_specs[2] = spec((q, bw), ix(lambda rb, gi: (rb0 + rb, c0 + gi)))
        args[0:3] = [proj, proj, proj]
        width = conv_w.shape[0]
        wb0 = dims.ssd_inner // bw
        wc0 = wb0 + (g * n) // bw
        in_specs += [spec((width, xw), ix(lambda rb, gi: (0, gi))),
                     spec((width, bw), ix(lambda rb, gi: (0, wb0 + gi))),
                     spec((width, bw), ix(lambda rb, gi: (0, wc0 + gi))),
                     spec((1, xw), ix(lambda rb, gi: (0, gi))),
                     spec((1, bw), ix(lambda rb, gi: (0, wb0 + gi))),
                     spec((1, bw), ix(lambda rb, gi: (0, wc0 + gi)))]
        conv_b2 = conv_b.reshape(1, dims.ssd_xbc)
        args += [conv_w, conv_w, conv_w, conv_b2, conv_b2, conv_b2]
        h_spec = pl.BlockSpec((1, gpb * r_heads, p, n), lambda b, gi, c: (b, gi, 0, 0))
        h_shape = jax.ShapeDtypeStruct((nb, dims.ssd_heads, p, n), F32)
        scratch = [pltpu.VMEM((gpb * r_heads * p, n), F32),
                   pltpu.VMEM((q, xw), F32), pltpu.VMEM((q, bw), F32), pltpu.VMEM((q, bw), F32),
                   pltpu.VMEM((q + SUBLANES, xw), F32), pltpu.VMEM((q + SUBLANES, bw), F32),
                   pltpu.VMEM((q + SUBLANES, bw), F32)]
    else:
        in_specs.append(pl.BlockSpec((nseq, gpb * r_heads, p, n), lambda b, gi: (b, gi, 0, 0)))
        args.append(h0)
        h_spec = pl.BlockSpec((nseq, gpb * r_heads, p, n), lambda b, gi: (b, gi, 0, 0))
        h_shape = jax.ShapeDtypeStruct((dims.dec_batch, dims.ssd_heads, p, n), F32)
        scratch = []
    return pl.pallas_call(
        functools.partial(_ssd_kernel, nseq=nseq, dims=dims, chained=chained, gpb=gpb),
        grid=grid,
        in_specs=in_specs,
        out_specs=[y_spec, h_spec],
        out_shape=[jax.ShapeDtypeStruct((nrows, dims.ssd_inner), BF16), h_shape],
        scratch_shapes=scratch,
        compiler_params=_params(*sems),
        name="ssd_prompt" if chained else "ssd_sample",
    )(*args)


def _attend(q, k, v, scale):
    s = lax.dot_general(q.astype(BF16), k.astype(BF16), (((1,), (1,)), ((), ())),
                        preferred_element_type=F32) * scale
    s = s - jnp.max(s, axis=-1, keepdims=True)
    e = jnp.exp(s)
    pr = (e / jnp.sum(e, axis=-1, keepdims=True)).astype(BF16)
    return jnp.dot(pr, v.astype(BF16), preferred_element_type=F32)


def _xattn_kernel(q_ref, k_ref, v_ref, o_ref, *, heads, head_dim):
    for h in range(heads):
        cs = slice(h * head_dim, (h + 1) * head_dim)
        o_ref[:, cs] = _attend(q_ref[:, cs], k_ref[0, :, cs], v_ref[0, :, cs], head_dim ** -0.5).astype(o_ref.dtype)


def _xattn_cached_kernel(q_ref, k_ref, v_ref, o_ref, *, heads, head_dim):
    tq = q_ref.shape[0]
    m = k_ref.shape[1]
    k = k_ref[0].reshape(m * heads, head_dim).astype(BF16)
    v = v_ref[0].reshape(m * heads, head_dim).astype(BF16)
    q = jnp.concatenate([q_ref[:, h * head_dim:(h + 1) * head_dim] for h in range(heads)], axis=0)
    s = lax.dot_general(q.astype(BF16), k, (((1,), (1,)), ((), ())), preferred_element_type=F32)
    s = s * (head_dim ** -0.5)
    row_head = lax.broadcasted_iota(jnp.int32, s.shape, 0) // tq
    col_head = lax.broadcasted_iota(jnp.int32, s.shape, 1) % heads
    s = jnp.where(row_head == col_head, s, NEG_INF)
    s = s - jnp.max(s, axis=-1, keepdims=True)
    e = jnp.exp(s)
    pr = (e / jnp.sum(e, axis=-1, keepdims=True)).astype(BF16)
    o = jnp.dot(pr, v, preferred_element_type=F32)
    for h in range(heads):
        o_ref[:, h * head_dim:(h + 1) * head_dim] = o[h * tq:(h + 1) * tq, :].astype(o_ref.dtype)


def cross_attention(dims, proj, k, v, *, row0, nseq, length, tq=512):
    w = dims.xatt_width
    tq = _pick(length, tq)
    nq = length // tq
    rb0 = row0 // tq
    assert row0 % tq == 0 and dims.off_q % w == 0
    qoff = dims.off_q // w
    hd, nh = dims.xatt_head_dim, dims.xatt_heads
    if k.ndim == 4:
        assert nq == 1 and tq % SUBLANES == 0
        kv_spec = pl.BlockSpec((1, dims.n_mem, nh, hd), lambda b: (b, 0, 0, 0))
        return pl.pallas_call(
            functools.partial(_xattn_cached_kernel, heads=nh, head_dim=hd),
            grid=(nseq,),
            in_specs=[pl.BlockSpec((tq, w), lambda b: (rb0 + b, qoff)), kv_spec, kv_spec],
            out_specs=pl.BlockSpec((tq, w), lambda b: (b, 0)),
            out_shape=jax.ShapeDtypeStruct((nseq * length, w), BF16),
            compiler_params=_params("parallel"),
            name="cross_attention_cached",
        )(proj, k, v)
    kv_spec = pl.BlockSpec((1, dims.n_mem, w), lambda b, i: (b, 0, 0))
    return pl.pallas_call(
        functools.partial(_xattn_kernel, heads=nh, head_dim=hd),
        grid=(nseq, nq),
        in_specs=[pl.BlockSpec((tq, w), lambda b, i: (rb0 + b * nq + i, qoff)), kv_spec, kv_spec],
        out_specs=pl.BlockSpec((tq, w), lambda b, i: (b * nq + i, 0)),
        out_shape=jax.ShapeDtypeStruct((nseq * length, w), BF16),
        compiler_params=_params("parallel", "parallel"),
        name="cross_attention",
    )(proj, k, v)


def _merge_kernel(vap_ref, ybp_ref, ocp_ref, vas_ref, ybs_ref, ocs_ref, wa_ref, wb_ref, wc_ref,
                  ga_ref, gb_ref, gc_ref, o_ref, *, npb):
    i = pl.program_id(1)

    def body(va_ref, yb_ref, oc_ref):
        ha = jnp.dot(va_ref[...], wa_ref[...], preferred_element_type=F32)
        hb = jnp.dot(yb_ref[...], wb_ref[...], preferred_element_type=F32)
        hc = jnp.dot(oc_ref[...], wc_ref[...], preferred_element_type=F32)
        mix = (jax.nn.sigmoid(ga_ref[...]) * ha + jax.nn.sigmoid(gb_ref[...]) * hb
               + jax.nn.sigmoid(gc_ref[...]) * hc)
        o_ref[...] = mix.astype(o_ref.dtype)

    @pl.when(i < npb)
    def _():
        body(vap_ref, ybp_ref, ocp_ref)

    @pl.when(i >= npb)
    def _():
        body(vas_ref, ybs_ref, ocs_ref)


def merge_branches(dims, prompt, sample, wa, wb, wc, proj, tm=512, tn=512):
    d = dims.d_model
    mp, ms = prompt[0].shape[0], sample[0].shape[0]
    tm, tn = _pick(ms, _pick(mp, tm)), _pick(d, tn)
    npb = mp // tm
    g0 = dims.off_gates // tn
    gd = d // tn
    assert dims.off_gates % tn == 0

    def lhs_p(a):
        return pl.BlockSpec((tm, a.shape[1]), lambda j, i: (jnp.minimum(i, npb - 1), 0))

    def lhs_s(a):
        return pl.BlockSpec((tm, a.shape[1]), lambda j, i: (jnp.maximum(i - npb, 0), 0))

    def rhs(width):
        return pl.BlockSpec((width, tn), lambda j, i: (0, j), pipeline_mode=pl.Buffered(1))

    def gate(k):
        return pl.BlockSpec((tm, tn), lambda j, i, k=k: (i, g0 + k * gd + j))

    return pl.pallas_call(
        functools.partial(_merge_kernel, npb=npb),
        grid=(d // tn, (mp + ms) // tm),
        in_specs=[lhs_p(a) for a in prompt] + [lhs_s(a) for a in sample]
        + [rhs(wa.shape[0]), rhs(wb.shape[0]), rhs(wc.shape[0]), gate(0), gate(1), gate(2)],
        out_specs=pl.BlockSpec((tm, tn), lambda j, i: (i, j)),
        out_shape=jax.ShapeDtypeStruct((mp + ms, d), BF16),
        compiler_params=_params("arbitrary", "arbitrary"),
        name="merge_branches",
    )(*prompt, *sample, wa, wb, wc, proj, proj, proj)


def _peer_select_kernel(q_ref, keys_ref, c1_ref, e1_ref, r2_ref, e2_ref, scores, work, tops, cand, ranks, *,
                        dims):
    heads, nk, topk = dims.peer_heads, dims.peer_keys, dims.peer_topk
    half = dims.peer_qdim // 2
    tb = q_ref.shape[0]
    rank = lax.broadcasted_iota(jnp.int32, (topk, tb), 0)

    for k in range(2 * heads):
        qh = q_ref[:, k * half:(k + 1) * half].astype(BF16)
        s = lax.dot_general(keys_ref[k].astype(BF16), qh, (((1,), (1,)), ((), ())),
                            preferred_element_type=F32)
        scores[k] = s
        work[k] = s
        tops[k] = jnp.full((topk, tb), NEG_INF, F32)

    for h in range(heads):
        ranks[h] = jnp.full((nk, tb), float(topk), F32)

    def extract(r, carry):
        rf = jnp.asarray(r, F32)
        for k in range(2 * heads):
            s = work[k]
            m = jnp.max(s, axis=0, keepdims=True)
            hit = s == m
            tops[k] = jnp.where(rank == r, m, tops[k])
            work[k] = jnp.where(hit, NEG_INF, s)
            if k % 2:
                ranks[k // 2] = jnp.where(hit, rf, ranks[k // 2])
        return carry
    lax.fori_loop(0, topk, extract, 0)

    for h in range(heads):
        a, b = tops[2 * h], tops[2 * h + 1]
        cand[h, 0:topk, :] = a[0:1, :] + b
        for pi in range(1, topk):
            lo = topk + (pi - 1) * SUBLANES
            cand[h, lo:lo + SUBLANES, :] = a[pi:pi + 1, :] + b[0:SUBLANES, :]

    def threshold(r, taus):
        new = []
        for h in range(heads):
            c = cand[h]
            m = jnp.max(c, axis=0, keepdims=True)
            cand[h] = jnp.where(c == m, NEG_INF, c)
            new.append(m)
        return tuple(new)
    taus = lax.fori_loop(0, topk, threshold, tuple(jnp.zeros((1, tb), F32) for _ in range(heads)))

    for h in range(heads):
        a, b = tops[2 * h], tops[2 * h + 1]
        tau = taus[h]
        z = None
        for pi in range(topk):
            rows = topk if pi == 0 else SUBLANES
            c = a[pi:pi + 1, :] + b[0:rows, :]
            part = jnp.sum(jnp.where(c >= tau, jnp.exp(c - (a[0:1, :] + b[0:1, :])), 0.0), axis=0, keepdims=True)
            z = part if z is None else z + part
        s1, s2 = scores[2 * h], scores[2 * h + 1]
        count = jnp.zeros_like(s1)
        for qi in range(topk):
            reach = a + b[qi:qi + 1, :] >= tau
            alpha = jnp.min(jnp.where(reach, a, float("inf")), axis=0, keepdims=True)
            count = count + jnp.where(s1 >= alpha, 1.0, 0.0)
        c1_ref[h] = count
        r2_ref[h] = ranks[h].astype(r2_ref.dtype)
        e1_ref[h] = jnp.exp(s1 - a[0:1, :]) * (0.5 / z)
        e2_ref[h] = jnp.exp(s2 - b[0:1, :]).astype(e2_ref.dtype)


def peer_select(dims, q, subkeys, tb=128):
    t = q.shape[0]
    heads, nk = dims.peer_heads, dims.peer_keys
    half = dims.peer_qdim // 2
    assert dims.peer_topk >= SUBLANES and heads == SUBLANES
    tab = jax.ShapeDtypeStruct((heads, nk, t), F32)
    tab16 = jax.ShapeDtypeStruct((heads, nk, t), BF16)
    tab_spec = pl.BlockSpec((heads, nk, tb), lambda i: (0, 0, i))
    return pl.pallas_call(
        functools.partial(_peer_select_kernel, dims=dims),
        grid=(t // tb,),
        in_specs=[pl.BlockSpec((tb, q.shape[1]), lambda i: (i, 0)),
                  pl.BlockSpec((2 * heads, nk, half), lambda i: (0, 0, 0))],
        out_specs=[tab_spec, tab_spec, tab_spec, tab_spec],
        out_shape=[tab, tab, tab16, tab16],
        scratch_shapes=[pltpu.VMEM((2 * heads, nk, tb), F32),
                        pltpu.VMEM((2 * heads, nk, tb), F32),
                        pltpu.VMEM((2 * heads, dims.peer_topk, tb), F32),
                        pltpu.VMEM((heads, dims.peer_topk + (dims.peer_topk - 1) * SUBLANES, tb), F32),
                        pltpu.VMEM((heads, nk, tb), F32)],
        compiler_params=_params("parallel"),
        name="peer_select",
    )(q, subkeys.reshape(2 * heads, nk, half))


def _peer_mix_kernel(x_ref, u_ref, v_ref, c1_ref, e1_ref, r2_ref, e2_ref, o_ref, gw_scr, w_scr, *, dims, sub):
    heads, nk = dims.peer_heads, dims.peer_keys
    eb = pl.program_id(1)
    te = u_ref.shape[0]

    @pl.when(eb == 0)
    def _():
        o_ref[...] = jnp.zeros_like(o_ref)

    tb = x_ref.shape[0]
    tr, tl = 2 * SUBLANES, LANES
    zero = jnp.zeros((tr, tl), BF16)
    for ii in range(te // nk):
        for c0 in range(0, tb, tl):
            cs = slice(c0, c0 + tl)
            acc = [None] * (nk // tr)
            for h in range(heads):
                count = jnp.broadcast_to(c1_ref[h, ii, :, cs], (tr, tl)).astype(BF16)
                e1 = jnp.broadcast_to(e1_ref[h, ii, :, cs], (tr, tl)).astype(BF16)
                for s in range(nk // tr):
                    rs = slice(s * tr, (s + 1) * tr)
                    g1 = jnp.minimum(jnp.maximum(count - r2_ref[h, rs, cs], zero), e1)
                    wh = g1 * e2_ref[h, rs, cs]
                    acc[s] = wh if acc[s] is None else acc[s] + wh
            for s in range(nk // tr):
                w_scr[ii * nk + s * tr:ii * nk + (s + 1) * tr, cs] = acc[s]
    x = x_ref[...]
    for sb in range(te // sub):
        rows = slice(sb * sub, (sb + 1) * sub)
        hh = lax.dot_general(u_ref[rows, :], x, (((1,), (1,)), ((), ())), preferred_element_type=F32)
        gelu2 = hh * (1.0 + lax.erf(hh * (2.0 ** -0.5)))
        gw_scr[rows, :] = gelu2.astype(BF16) * w_scr[rows, :]
        o_ref[...] += lax.dot_general(gw_scr[rows, :], v_ref[rows, :], (((0,), (0,)), ((), ())),
                                      preferred_element_type=F32)


def peer_mix(dims, xn, u, v, c1, e1, r2, e2, tb=512, te=512, sub=512):
    t, d = xn.shape
    heads, nk = dims.peer_heads, dims.peer_keys
    tb, te = _pick(t, tb), _pick(dims.n_experts, te)
    sub = min(sub, te)
    assert te % sub == 0 and sub % nk == 0
    ni = te // nk
    row_tab = lambda a: a.reshape(heads, nk, 1, t)
    row_spec = pl.BlockSpec((heads, ni, 1, tb), lambda i, e: (0, e, 0, i))
    col_spec = pl.BlockSpec((heads, nk, tb), lambda i, e: (0, 0, i))
    return pl.pallas_call(
        functools.partial(_peer_mix_kernel, dims=dims, sub=sub),
        grid=(t // tb, dims.n_experts // te),
        in_specs=[pl.BlockSpec((tb, d), lambda i, e: (i, 0)),
                  pl.BlockSpec((te, d), lambda i, e: (e, 0)),
                  pl.BlockSpec((te, d), lambda i, e: (e, 0)),
                  row_spec, row_spec, col_spec, col_spec],
        out_specs=pl.BlockSpec((tb, d), lambda i, e: (i, 0)),
        out_shape=jax.ShapeDtypeStruct((t, d), F32),
        scratch_shapes=[pltpu.VMEM((te, tb), BF16), pltpu.VMEM((te, tb), BF16)],
        compiler_params=_params("parallel", "arbitrary"),
        name="peer_mix",
    )(xn, u, v, row_tab(c1), row_tab(e1), r2, e2)


def _pad_prev(state, width):
    return jnp.pad(state, ((0, 0), (SUBLANES - (width - 1), 0), (0, 0)))


def forward(dims, x_prompt, x_sample, mem_prompt, cache_mem_k, cache_mem_v, state_conv_a,
            state_ssd_conv, state_ssd, norm_mix, norm_mem, norm_ffn, norm_final, w_in,
            a_conv_w, a_out, ssd_conv_w, ssd_conv_b, ssd_dt_bias, ssd_a_log, ssd_d, ssd_norm,
            ssd_out, w_mem_k, w_mem_v, xatt_out, w_o, peer_wq, peer_subkeys, peer_u, peer_v):
    d = dims.d_model
    tp, ts, t = dims.t_prompt, dims.t_sample, dims.tokens
    g, rh, p, n = dims.ssd_groups, dims.heads_per_group, dims.ssd_head_dim, dims.ssd_state
    bf = lambda a: a.astype(BF16)

    xp2, xs2 = x_prompt.reshape(tp, d), x_sample.reshape(ts, d)

    mn = rmsnorm(mem_prompt.reshape(dims.batch * dims.n_mem, d), norm_mem[0], BF16)
    xw = dims.xatt_width
    k_p = matmul(mn, w_mem_k, F32, tn=512, name="mem_k").reshape(dims.batch, dims.n_mem, xw)
    v_p = matmul(mn, w_mem_v, F32, tn=512, name="mem_v").reshape(dims.batch, dims.n_mem, xw)
    mem_k_p = k_p.reshape(1, dims.batch, dims.n_mem, dims.xatt_heads, dims.xatt_head_dim)
    mem_v_p = v_p.reshape(1, dims.batch, dims.n_mem, dims.xatt_heads, dims.xatt_head_dim)

    xn = rmsnorm_parts(xp2, xs2, norm_mix[0], BF16)
    wt = jnp.swapaxes(w_in, 1, 2)[0]
    dt0 = 3 * dims.a_width + dims.ssd_inner + dims.ssd_xbc
    proj = matmul_wt(xn, wt, [(0, dt0), (dt0 + dims.ssd_heads, dims.proj_width - dt0)], tm=512, tn=1024,
                     name="in_proj")
    dt_t = dt_proj_t(xn, wt, dt0, dims.ssd_heads).reshape(g, rh, t)

    a_cols = (dims.off_ain, dims.off_abg, dims.off_acg)
    zeros_a = jnp.zeros((dims.batch, SUBLANES, dims.a_width), F32)
    va_p, st_a_p = causal_conv(proj, zeros_a, a_conv_w[0], row0=0, nseq=dims.batch, length=dims.seq,
                               cols=dims.a_width, col_offs=a_cols, gated=True, out_dtype=BF16)
    prev_a = _pad_prev(state_conv_a[0], dims.a_conv).reshape(dims.dec_batch * SUBLANES, dims.a_width)
    va_s, u_s = short_seq_conv(proj, prev_a, a_conv_w[0], row0=tp, nseq=dims.dec_batch, seq=dims.dec_seq,
                               cols=dims.a_width, col_offs=a_cols, gated=True, out_dtype=BF16)
    na = dims.a_conv - 1
    conv_a_p = st_a_p[None, :, SUBLANES - na:, :]
    conv_a_s = u_s.reshape(dims.dec_batch, dims.dec_seq, dims.a_width)[None, :, dims.dec_seq - na:, :]

    prev_b = _pad_prev(state_ssd_conv[0], dims.ssd_conv).reshape(dims.dec_batch * SUBLANES, dims.ssd_xbc)
    xbc_s = short_seq_conv(proj, prev_b, ssd_conv_w[0], row0=tp, nseq=dims.dec_batch, seq=dims.dec_seq,
                           cols=dims.ssd_xbc, col_offs=(dims.off_xbc,), gated=False, bias=ssd_conv_b[0])
    nb = dims.ssd_conv - 1
    x0, x1c = dims.off_xbc, dims.off_xbc + dims.ssd_xbc
    ssd_conv_p = jnp.stack([lax.slice(proj, ((b + 1) * dims.seq - nb, x0), ((b + 1) * dims.seq, x1c))
                            for b in range(dims.batch)])[None]
    ssd_conv_s = lax.slice(proj.reshape(t // dims.dec_seq, dims.dec_seq, dims.proj_width),
                           (tp // dims.dec_seq, dims.dec_seq - nb, x0),
                           (t // dims.dec_seq, dims.dec_seq, x1c))[None]

    bias_g = ssd_dt_bias[0].reshape(g, rh, 1)
    alog_g = ssd_a_log[0].reshape(g, rh, 1)
    d_exp = jnp.repeat(ssd_d[0], p).reshape(1, dims.ssd_inner)
    norm_w = ssd_norm[0].reshape(1, dims.ssd_inner)
    yb_p, h_p = ssd_block(dims, None, proj, dt_t, bias_g, alog_g, d_exp, norm_w, row0=0, nrows=tp, chained=True,
                          conv=(ssd_conv_w[0], ssd_conv_b[0]))
    yb_s, h_s = ssd_block(dims, xbc_s, proj, dt_t, bias_g, alog_g, d_exp, norm_w, row0=tp, nrows=ts,
                          chained=False, h0=state_ssd[0])

    oc_p = cross_attention(dims, proj, k_p, v_p, row0=0, nseq=dims.batch, length=dims.seq)
    oc_s = cross_attention(dims, proj, cache_mem_k[0], cache_mem_v[0], row0=tp, nseq=dims.dec_batch,
                           length=dims.dec_seq)

    mix = merge_branches(dims, (va_p, yb_p, oc_p), (va_s, yb_s, oc_s), bf(a_out[0]), bf(ssd_out[0]),
                         bf(xatt_out[0]), proj)
    x1 = matmul(mix, bf(w_o[0]), F32, residual_parts=(xp2, xs2), name="out_proj")

    xn2 = rmsnorm(x1, norm_ffn[0], BF16)
    qp = matmul(xn2, bf(peer_wq[0]), F32, name="peer_query")
    c1, e1, r2, e2 = peer_select(dims, qp, peer_subkeys[0])
    ffn = peer_mix(dims, xn2, cast_layer(peer_u), cast_layer(peer_v), c1, e1, r2, e2)
    y_prompt = rmsnorm(x1, norm_final, F32, residual=ffn, row0=0, nrows=tp).reshape(dims.batch, dims.seq, d)
    y_sample = rmsnorm(x1, norm_final, F32, residual=ffn, row0=tp, nrows=ts).reshape(
        dims.dec_batch, dims.dec_seq, d)
    return (y_prompt, y_sample, mem_k_p, mem_v_p, conv_a_p, ssd_conv_p, h_p[None],
            conv_a_s, ssd_conv_s, h_s[None])


def kernel(x_prompt, x_sample, mem_prompt, cache_mem_k, cache_mem_v, state_conv_a, state_ssd_conv, state_ssd, norm_mix, norm_mem, norm_ffn, norm_final, w_in, a_conv_w, a_out, ssd_conv_w, ssd_conv_b, ssd_dt_bias, ssd_a_log, ssd_d, ssd_norm, ssd_out, w_mem_k, w_mem_v, xatt_out, w_o, peer_wq, peer_subkeys, peer_u, peer_v):
    return forward(FULL, x_prompt, x_sample, mem_prompt, cache_mem_k, cache_mem_v, state_conv_a,
                   state_ssd_conv, state_ssd, norm_mix, norm_mem, norm_ffn, norm_final, w_in,
                   a_conv_w, a_out, ssd_conv_w, ssd_conv_b, ssd_dt_bias, ssd_a_log, ssd_d, ssd_norm,
                   ssd_out, w_mem_k, w_mem_v, xatt_out, w_o, peer_wq, peer_subkeys, peer_u, peer_v)
```

```python
import dataclasses
import functools

import jax
import jax.numpy as jnp
import numpy as np
from jax import lax
from jax.experimental import pallas as pl
from jax.experimental.pallas import tpu as pltpu

F32 = jnp.float32
BF16 = jnp.bfloat16
EPS = 1e-6
HIGHEST = lax.Precision.HIGHEST
NEG_INF = float("-inf")

LANES = 128
SUBLANES = 8
VMEM_LIMIT_BYTES = 56 * 1024 * 1024


@dataclasses.dataclass(frozen=True)
class Dims:
    d_model: int = 4096
    batch: int = 4
    seq: int = 2048
    dec_batch: int = 128
    dec_seq: int = 8
    a_width: int = 2048
    a_conv: int = 3
    ssd_inner: int = 4096
    ssd_head_dim: int = 64
    ssd_groups: int = 8
    ssd_state: int = 128
    ssd_conv: int = 4
    ssd_chunk: int = 128
    n_mem: int = 256
    xatt_heads: int = 4
    xatt_head_dim: int = 512
    peer_heads: int = 8
    peer_keys: int = 128
    peer_topk: int = 16
    peer_qdim: int = 256

    @property
    def ssd_heads(self):
        return self.ssd_inner // self.ssd_head_dim

    @property
    def heads_per_group(self):
        return self.ssd_heads // self.ssd_groups

    @property
    def group_width(self):
        return self.ssd_inner // self.ssd_groups

    @property
    def ssd_xbc(self):
        return self.ssd_inner + 2 * self.ssd_groups * self.ssd_state

    @property
    def xatt_width(self):
        return self.xatt_heads * self.xatt_head_dim

    @property
    def n_experts(self):
        return self.peer_keys * self.peer_keys

    @property
    def t_prompt(self):
        return self.batch * self.seq

    @property
    def t_sample(self):
        return self.dec_batch * self.dec_seq

    @property
    def tokens(self):
        return self.t_prompt + self.t_sample

    @property
    def off_ain(self):
        return 0

    @property
    def off_abg(self):
        return self.a_width

    @property
    def off_acg(self):
        return 2 * self.a_width

    @property
    def off_z(self):
        return 3 * self.a_width

    @property
    def off_xbc(self):
        return self.off_z + self.ssd_inner

    @property
    def off_q(self):
        return self.off_xbc + self.ssd_xbc

    @property
    def off_gates(self):
        return self.off_q + self.xatt_width

    @property
    def proj_width(self):
        return self.off_gates + 3 * self.d_model


FULL = Dims()


def _params(*sem):
    return pltpu.CompilerParams(dimension_semantics=sem, vmem_limit_bytes=VMEM_LIMIT_BYTES)


def _pick(n, pref):
    t = min(n, pref)
    while n % t:
        t //= 2
    return t


def _rmsnorm_rows(x, g_ref, o_ref):
    inv = lax.rsqrt(jnp.mean(x * x, axis=-1, keepdims=True) + EPS)
    o_ref[...] = ((x * inv) * g_ref[...]).astype(o_ref.dtype)


def _rmsnorm_kernel(x_ref, g_ref, o_ref):
    _rmsnorm_rows(x_ref[...], g_ref, o_ref)


def _add_rmsnorm_kernel(x_ref, r_ref, g_ref, o_ref):
    _rmsnorm_rows(x_ref[...] + r_ref[...], g_ref, o_ref)


def _rmsnorm_parts_kernel(xp_ref, xs_ref, g_ref, o_ref, *, npb):
    i = pl.program_id(0)

    @pl.when(i < npb)
    def _():
        _rmsnorm_rows(xp_ref[...], g_ref, o_ref)

    @pl.when(i >= npb)
    def _():
        _rmsnorm_rows(xs_ref[...], g_ref, o_ref)


def rmsnorm(x, g, out_dtype, residual=None, row0=0, nrows=None, tm=256):
    m, d = x.shape
    nrows = m - row0 if nrows is None else nrows
    tm = _pick(nrows, tm)
    rb0 = row0 // tm
    assert row0 % tm == 0
    row = pl.BlockSpec((tm, d), lambda i: (rb0 + i, 0))
    gspec = pl.BlockSpec((1, d), lambda i: (0, 0))
    args = (x,) if residual is None else (x, residual)
    return pl.pallas_call(
        _rmsnorm_kernel if residual is None else _add_rmsnorm_kernel,
        grid=(nrows // tm,),
        in_specs=[row] * len(args) + [gspec],
        out_specs=pl.BlockSpec((tm, d), lambda i: (i, 0)),
        out_shape=jax.ShapeDtypeStruct((nrows, d), out_dtype),
        compiler_params=_params("parallel"),
        name="rmsnorm" if residual is None else "add_rmsnorm",
    )(*args, g.reshape(1, d))


def rmsnorm_parts(xp, xs, g, out_dtype, tm=256):
    d = xp.shape[1]
    tm = _pick(xs.shape[0], _pick(xp.shape[0], tm))
    npb, nsb = xp.shape[0] // tm, xs.shape[0] // tm
    return pl.pallas_call(
        functools.partial(_rmsnorm_parts_kernel, npb=npb),
        grid=(npb + nsb,),
        in_specs=[pl.BlockSpec((tm, d), lambda i: (jnp.minimum(i, npb - 1), 0)),
                  pl.BlockSpec((tm, d), lambda i: (jnp.maximum(i - npb, 0), 0)),
                  pl.BlockSpec((1, d), lambda i: (0, 0))],
        out_specs=pl.BlockSpec((tm, d), lambda i: (i, 0)),
        out_shape=jax.ShapeDtypeStruct((xp.shape[0] + xs.shape[0], d), out_dtype),
        compiler_params=_params("arbitrary"),
        name="rmsnorm_parts",
    )(xp, xs, g.reshape(1, d))


def _cast_kernel(x_ref, o_ref):
    o_ref[...] = x_ref[0].astype(o_ref.dtype)


def cast_layer(w, dtype=BF16, tr=512):
    _, r, c = w.shape
    tr = _pick(r, tr)
    return pl.pallas_call(
        _cast_kernel,
        grid=(r // tr,),
        in_specs=[pl.BlockSpec((1, tr, c), lambda i: (0, i, 0))],
        out_specs=pl.BlockSpec((tr, c), lambda i: (i, 0)),
        out_shape=jax.ShapeDtypeStruct((r, c), dtype),
        compiler_params=_params("parallel"),
        name="cast_layer",
    )(w)


def _mm_kernel(*refs, npb, layer_weight):
    a_ref, b_ref = refs[:2]
    refs = refs[2:]
    if npb is not None:
        rp_ref, rs_ref = refs[:2]
        refs = refs[2:]
    o_ref = refs[0]
    i = pl.program_id(1)
    if layer_weight:
        w_scr = refs[1]

        @pl.when(i == 0)
        def _():
            w_scr[...] = b_ref[0].astype(BF16)

        w = w_scr[...]
    else:
        w = b_ref[...]
    acc = jnp.dot(a_ref[...], w, preferred_element_type=F32)
    if npb is None:
        o_ref[...] = acc.astype(o_ref.dtype)
    else:
        @pl.when(i < npb)
        def _():
            o_ref[...] = (rp_ref[...] + acc).astype(o_ref.dtype)

        @pl.when(i >= npb)
        def _():
            o_ref[...] = (rs_ref[...] + acc).astype(o_ref.dtype)


def matmul(a, b, out_dtype, residual_parts=None, tm=512, tn=1024, name="matmul"):
    m, k = a.shape
    layer_weight = b.ndim == 3
    n = b.shape[-1]
    tm, tn = _pick(m, tm), _pick(n, tn)
    npb = None
    extra_specs, extra_args = [], []
    if residual_parts is not None:
        rp, rs = residual_parts
        tm = _pick(rs.shape[0], _pick(rp.shape[0], tm))
        npb = rp.shape[0] // tm
        extra_specs = [pl.BlockSpec((tm, tn), lambda j, i: (jnp.minimum(i, npb - 1), j)),
                       pl.BlockSpec((tm, tn), lambda j, i: (jnp.maximum(i - npb, 0), j))]
        extra_args = [rp, rs]
    if layer_weight:
        b_spec = pl.BlockSpec((1, k, tn), lambda j, i: (0, 0, j))
        scratch = [pltpu.VMEM((k, tn), BF16)]
    else:
        b_spec = pl.BlockSpec((k, tn), lambda j, i: (0, j))
        scratch = []
    sequential = layer_weight or npb is not None
    return pl.pallas_call(
        functools.partial(_mm_kernel, npb=npb, layer_weight=layer_weight),
        grid=(n // tn, m // tm),
        in_specs=[pl.BlockSpec((tm, k), lambda j, i: (i, 0)), b_spec] + extra_specs,
        out_specs=pl.BlockSpec((tm, tn), lambda j, i: (i, j)),
        out_shape=jax.ShapeDtypeStruct((m, n), out_dtype),
        scratch_shapes=scratch,
        compiler_params=_params(*(("arbitrary", "arbitrary") if sequential else ("parallel", "parallel"))),
        name=name,
    )(a, b, *extra_args)


def _mm_wt_kernel(a_ref, wt_ref, o_ref, w_scr):
    @pl.when(pl.program_id(1) == 0)
    def _():
        w_scr[...] = wt_ref[...].astype(BF16)

    o_ref[...] = lax.dot_general(a_ref[...], w_scr[...], (((1,), (1,)), ((), ())),
                                 preferred_element_type=F32).astype(o_ref.dtype)


def matmul_wt(a, wt, segments, out_dtype=F32, tm=1024, tn=512, name="matmul_wt"):
    m, k = a.shape
    tm = _pick(m, tm)
    for start, length in segments:
        tn = _pick(length, tn)
        assert start % SUBLANES == 0
    assert all(length % tn == 0 for _, length in segments)
    starts = np.concatenate([np.arange(s, s + l, tn) for s, l in segments]).astype(np.int32)
    nblk = len(starts)
    bounds = np.cumsum([l // tn for _, l in segments])[:-1]
    shifts = [segments[i + 1][0] - (segments[i][0] + segments[i][1]) for i in range(len(segments) - 1)]

    def row_start(j):
        r = segments[0][0] + j * tn
        for b, sh in zip(bounds, shifts):
            r = r + jnp.where(j >= b, sh, 0)
        return pl.multiple_of(r, SUBLANES)

    return pl.pallas_call(
        _mm_wt_kernel,
        grid=(nblk, m // tm),
        in_specs=[pl.BlockSpec((tm, k), lambda j, i: (i, 0)),
                  pl.BlockSpec((pl.Element(tn), pl.Element(k)), lambda j, i: (row_start(j), 0))],
        out_specs=pl.BlockSpec((tm, tn), lambda j, i: (i, j)),
        out_shape=jax.ShapeDtypeStruct((m, nblk * tn), out_dtype),
        scratch_shapes=[pltpu.VMEM((tn, k), BF16)],
        compiler_params=_params("arbitrary", "arbitrary"),
        name=name,
    )(a, wt)


def _dt_proj_kernel(a_ref, wt_ref, o_ref):
    o_ref[...] = lax.dot_general(wt_ref[...].astype(BF16), a_ref[...], (((1,), (1,)), ((), ())),
                                 preferred_element_type=F32)


def dt_proj_t(a, wt, start, nrows, tm=1024):
    m, k = a.shape
    tm = _pick(m, tm)
    assert start % SUBLANES == 0
    return pl.pallas_call(
        _dt_proj_kernel,
        grid=(m // tm,),
        in_specs=[pl.BlockSpec((tm, k), lambda i: (i, 0)),
                  pl.BlockSpec((pl.Element(nrows), pl.Element(k)), lambda i: (start, 0))],
        out_specs=pl.BlockSpec((nrows, tm), lambda i: (0, i)),
        out_shape=jax.ShapeDtypeStruct((nrows, m), F32),
        compiler_params=_params("parallel"),
        name="dt_proj",
    )(a, wt)


def _gated_conv_kernel(in_ref, bg_ref, cg_ref, prev_ref, w_ref, o_ref, st_ref, scr, *, width, tl):
    lt = pl.program_id(2)

    @pl.when(lt == 0)
    def _():
        scr[0:SUBLANES, :] = prev_ref[0]

    u = cg_ref[...] * in_ref[...]
    scr[SUBLANES:SUBLANES + tl, :] = u
    acc = w_ref[width - 1:width, :] * u
    for k in range(width - 1):
        lo = SUBLANES - (width - 1 - k)
        acc = acc + w_ref[k:k + 1, :] * scr[lo:lo + tl, :]
    tail = scr[tl:tl + SUBLANES, :]
    scr[0:SUBLANES, :] = tail
    o_ref[...] = (bg_ref[...] * acc).astype(o_ref.dtype)
    st_ref[0] = tail


def gated_conv(proj, prev8, w, *, row0, nseq, length, cols, col_offs, out_dtype, tl=1024, tc=512):
    width = w.shape[0]
    tl, tc = _pick(length, tl), _pick(cols, tc)
    while any(o % tc for o in col_offs):
        tc //= 2
    nl = length // tl
    rb0 = row0 // tl
    assert row0 % tl == 0

    def blk(off):
        return pl.BlockSpec((tl, tc), lambda s, c, l, off=off: (rb0 + s * nl + l, off // tc + c))

    carry_spec = pl.BlockSpec((1, SUBLANES, tc), lambda s, c, l: (s, 0, c))
    return pl.pallas_call(
        functools.partial(_gated_conv_kernel, width=width, tl=tl),
        grid=(nseq, cols // tc, nl),
        in_specs=[blk(col_offs[0]), blk(col_offs[1]), blk(col_offs[2]), carry_spec,
                  pl.BlockSpec((width, tc), lambda s, c, l: (0, c))],
        out_specs=[pl.BlockSpec((tl, tc), lambda s, c, l: (s * nl + l, c)), carry_spec],
        out_shape=[jax.ShapeDtypeStruct((nseq * length, cols), out_dtype),
                   jax.ShapeDtypeStruct((nseq, SUBLANES, cols), F32)],
        scratch_shapes=[pltpu.VMEM((tl + SUBLANES, tc), F32)],
        compiler_params=_params("parallel", "parallel", "arbitrary"),
        name="gated_conv",
    )(proj, proj, proj, prev8, w)


def _short_conv_kernel(*refs, width, seq, gated):
    if gated:
        in_ref, bg_ref, cg_ref, prev_ref, w_ref, o_ref, u_ref, scr_u, scr_p = refs
    else:
        in_ref, prev_ref, w_ref, bias_ref, o_ref, scr_u, scr_p = refs
    rows = in_ref.shape[0]
    u = cg_ref[...] * in_ref[...] if gated else in_ref[...]
    zeros = jnp.zeros((SUBLANES, u.shape[1]), F32)
    scr_u[0:SUBLANES, :] = zeros
    scr_u[SUBLANES:SUBLANES + rows, :] = u
    scr_p[0:rows, :] = prev_ref[...]
    scr_p[rows:rows + SUBLANES, :] = zeros
    pos = lax.broadcasted_iota(jnp.int32, u.shape, 0) % seq
    acc = w_ref[width - 1:width, :] * u
    for k in range(width - 1):
        shift = width - 1 - k
        lo = SUBLANES - shift
        operand = jnp.where(pos >= shift, scr_u[lo:lo + rows, :], scr_p[lo:lo + rows, :])
        acc = acc + w_ref[k:k + 1, :] * operand
    if gated:
        o_ref[...] = (bg_ref[...] * acc).astype(o_ref.dtype)
        u_ref[...] = u
    else:
        y = acc + bias_ref[...]
        o_ref[...] = (y * jax.nn.sigmoid(y)).astype(o_ref.dtype)


def short_seq_conv(proj, prev8, w, *, row0, nseq, seq, cols, col_offs, gated, bias=None, out_dtype=F32,
                   rows=128, tc=2048):
    width = w.shape[0]
    assert seq == SUBLANES and width - 1 <= seq
    total = nseq * seq
    rows, tc = _pick(total, rows), _pick(cols, tc)
    while any(o % tc for o in col_offs):
        tc //= 2
    rb0 = row0 // rows
    assert row0 % rows == 0

    def blk(off):
        return pl.BlockSpec((rows, tc), lambda r, c, off=off: (rb0 + r, off // tc + c))

    own = pl.BlockSpec((rows, tc), lambda r, c: (r, c))
    w_spec = pl.BlockSpec((width, tc), lambda r, c: (0, c))
    out_shape = jax.ShapeDtypeStruct((total, cols), out_dtype)
    if gated:
        in_specs = [blk(col_offs[0]), blk(col_offs[1]), blk(col_offs[2]), own, w_spec]
        args = (proj, proj, proj, prev8, w)
        out_specs = [own, own]
        out_shape = [out_shape, jax.ShapeDtypeStruct((total, cols), F32)]
    else:
        in_specs = [blk(col_offs[0]), own, w_spec, pl.BlockSpec((1, tc), lambda r, c: (0, c))]
        args = (proj, prev8, w, bias.reshape(1, cols))
        out_specs = own
    return pl.pallas_call(
        functools.partial(_short_conv_kernel, width=width, seq=seq, gated=gated),
        grid=(total // rows, cols // tc),
        in_specs=in_specs,
        out_specs=out_specs,
        out_shape=out_shape,
        scratch_shapes=[pltpu.VMEM((rows + SUBLANES, tc), F32), pltpu.VMEM((rows + SUBLANES, tc), F32)],
        compiler_params=_params("parallel", "parallel"),
        name="gated_conv_short" if gated else "ssd_conv_short",
    )(*args)


def _softplus(x):
    return jnp.maximum(x, 0.0) + jnp.log1p(jnp.exp(-jnp.abs(x)))


def _ssd_kernel(*refs, nseq, dims, chained, gpb):
    q = dims.ssd_chunk
    r_heads = dims.heads_per_group
    p = dims.ssd_head_dim
    gw = dims.group_width
    n = dims.ssd_state
    seg = q // nseq
    if chained:
        (xs_raw, b_raw, c_raw, z_ref, dt_ref, bias_ref, alog_ref, d_ref, nw_ref, sel_ref,
         wx_ref, wb_ref, wc_ref, cbx_ref, cbb_ref, cbc_ref,
         y_ref, hout_ref, h_scr, xs_ref, b_ref, c_ref, tail_x, tail_b, tail_c) = refs
        first = pl.program_id(2) == 0

        @pl.when(first)
        def _():
            h_scr[...] = jnp.zeros_like(h_scr)
            for tail in (tail_x, tail_b, tail_c):
                tail[0:SUBLANES, :] = jnp.zeros((SUBLANES, tail.shape[1]), F32)

        width = wx_ref.shape[0]
        for raw, w_ref, cb_ref, tail, out in ((xs_raw, wx_ref, cbx_ref, tail_x, xs_ref),
                                              (b_raw, wb_ref, cbb_ref, tail_b, b_ref),
                                              (c_raw, wc_ref, cbc_ref, tail_c, c_ref)):
            u = raw[...]
            tail[SUBLANES:SUBLANES + q, :] = u
            acc = w_ref[width - 1:width, :] * u
            for k in range(width - 1):
                lo = SUBLANES - (width - 1 - k)
                acc = acc + w_ref[k:k + 1, :] * tail[lo:lo + q, :]
            tail[0:SUBLANES, :] = tail[q:q + SUBLANES, :]
            yv = acc + cb_ref[...]
            out[...] = yv * jax.nn.sigmoid(yv)
    else:
        (xs_ref, b_ref, c_ref, z_ref, dt_ref, bias_ref, alog_ref, d_ref, nw_ref, sel_ref, h0_ref,
         y_ref, hout_ref) = refs

    for gi in range(gpb):
        gcols = slice(gi * gw, (gi + 1) * gw)
        ncols = slice(gi * n, (gi + 1) * n)
        _ssd_group(
            xs_ref.at[:, gcols], b_ref.at[:, ncols], c_ref.at[:, ncols], z_ref.at[:, gcols],
            dt_ref.at[gi], bias_ref.at[gi], alog_ref.at[gi], d_ref.at[:, gcols], nw_ref.at[:, gcols], sel_ref,
            y_ref.at[:, gcols],
            hout_ref.at[:, gi * r_heads:(gi + 1) * r_heads],
            h_scr.at[gi * r_heads * p:(gi + 1) * r_heads * p, :] if chained else None,
            None if chained else h0_ref.at[:, gi * r_heads:(gi + 1) * r_heads],
            nseq=nseq, dims=dims, chained=chained)


def _ssd_group(xs_ref, b_ref, c_ref, z_ref, dt_ref, bias_ref, alog_ref, d_ref, nw_ref, sel_ref, y_ref,
               hout_ref, h_scr, h0_ref, *, nseq, dims, chained):
    q = dims.ssd_chunk
    r_heads = dims.heads_per_group
    p = dims.ssd_head_dim
    gw = dims.group_width
    seg = q // nseq
    row = lax.broadcasted_iota(jnp.int32, (q, q), 0)
    col = lax.broadcasted_iota(jnp.int32, (q, q), 1)
    same = (row // seg) == (col // seg)
    dt = _softplus(dt_ref[...] + bias_ref[...])
    adt = dt * (-jnp.exp(alog_ref[...]))
    cum_mask = jnp.where(same & (row <= col), 1.0, 0.0).astype(F32)
    acum_t = jnp.dot(adt, cum_mask, precision=HIGHEST, preferred_element_type=F32)
    if nseq == 1:
        atot_t = jnp.broadcast_to(acum_t[:, q - 1:q], (r_heads, q))
    else:
        atot_t = jnp.dot(adt, jnp.where(same, 1.0, 0.0).astype(F32), precision=HIGHEST,
                         preferred_element_type=F32)
    stack = jnp.concatenate(
        [acum_t, dt, dt * jnp.exp(atot_t - acum_t), jnp.exp(acum_t),
         jnp.zeros((q - 4 * r_heads, q), F32)], axis=0)
    cols_form = stack.T
    sel = sel_ref[...]
    hi = cols_form.astype(BF16)
    rest = cols_form - hi.astype(F32)
    mid = rest.astype(BF16)
    low = (rest - mid.astype(F32)).astype(BF16)
    expand = (jnp.dot(hi, sel, preferred_element_type=F32) + jnp.dot(mid, sel, preferred_element_type=F32)
              + jnp.dot(low, sel, preferred_element_type=F32))
    e_dt, e_st, e_ac = expand[:, :gw], expand[:, gw:2 * gw], expand[:, 2 * gw:]

    x = xs_ref[...]
    bb = b_ref[...].astype(BF16)
    cb_ = c_ref[...].astype(BF16)
    cb = lax.dot_general(cb_, bb, (((1,), (1,)), ((), ())), preferred_element_type=F32)
    x_dt = x * e_dt
    x_dt_b = x_dt.astype(BF16)
    causal = same & (row >= col)
    lane = lax.broadcasted_iota(jnp.int32, (q, LANES), 1)
    heads_per_tile = LANES // p

    y_tiles = []
    for tile in range(gw // LANES):
        xt = x_dt_b[:, tile * LANES:(tile + 1) * LANES]
        acc = None
        for k in range(heads_per_tile):
            r = tile * heads_per_tile + k
            segm = cols_form[:, r:r + 1] - acum_t[r:r + 1, :]
            decay = jnp.exp(jnp.where(causal, segm, NEG_INF))
            m_r = (cb * decay).astype(BF16)
            x_r = jnp.where((lane >= k * p) & (lane < (k + 1) * p), xt, jnp.zeros_like(xt))
            part = jnp.dot(m_r, x_r, preferred_element_type=F32)
            acc = part if acc is None else acc + part
        y_tiles.append(acc)
    y = jnp.concatenate(y_tiles, axis=1)

    x_st = x * e_st
    eac_t = jnp.exp(atot_t)

    if chained:
        h = h_scr[...]
        y_off = lax.dot_general(cb_, h.astype(BF16), (((1,), (1,)), ((), ())),
                                preferred_element_type=F32)
        y = y + y_off * e_ac
        s_new = lax.dot_general(x_st.astype(BF16), bb, (((0,), (0,)), ((), ())),
                                preferred_element_type=F32)
        scale = jnp.broadcast_to(eac_t[:, q - 1:q], (r_heads, LANES))
        for r in range(r_heads):
            rows = slice(r * p, (r + 1) * p)
            h_scr[rows, :] = h[rows, :] * jnp.broadcast_to(scale[r:r + 1, :], (p, LANES)) + s_new[rows, :]
        for r in range(r_heads):
            hout_ref[0, r] = h_scr[r * p:(r + 1) * p, :]
    else:
        x_st_t = x_st.T
        rows_q = lax.broadcasted_iota(jnp.int32, (q, gw), 0)
        lanes_q = lax.broadcasted_iota(jnp.int32, (gw, q), 1)
        y_off = jnp.zeros((q, gw), F32)
        for s in range(nseq):
            h_s = h0_ref[s].reshape(r_heads * p, dims.ssd_state)
            y_s = lax.dot_general(cb_, h_s.astype(BF16), (((1,), (1,)), ((), ())),
                                  preferred_element_type=F32)
            y_off = jnp.where(rows_q // seg == s, y_s, y_off)
            xs_s = jnp.where(lanes_q // seg == s, x_st_t, 0.0).astype(BF16)
            s_new = jnp.dot(xs_s, bb, preferred_element_type=F32)
            scale = jnp.broadcast_to(eac_t[:, s * seg:s * seg + 1], (r_heads, LANES))
            for r in range(r_heads):
                rows = slice(r * p, (r + 1) * p)
                hout_ref[s, r] = (h_s[rows, :] * jnp.broadcast_to(scale[r:r + 1, :], (p, LANES))
                                  + s_new[rows, :])
        y = y + y_off * e_ac

    y = y + d_ref[...] * x
    z = z_ref[...]
    y = y * (z * jax.nn.sigmoid(z))
    inv = lax.rsqrt(jnp.mean(y * y, axis=-1, keepdims=True) + EPS)
    y_ref[...] = ((y * inv) * nw_ref[...]).astype(y_ref.dtype)


def _ssd_selector(dims):
    r_heads, p, gw = dims.heads_per_group, dims.ssd_head_dim, dims.group_width
    sel = np.zeros((dims.ssd_chunk, 3 * gw), np.float32)
    for part in range(3):
        for r in range(r_heads):
            sel[(part + 1) * r_heads + r, part * gw + r * p:part * gw + (r + 1) * p] = 1.0
    return jnp.asarray(sel, BF16)


def ssd_block(dims, xbc, proj, dt_t, bias_g, alog_g, d_exp, norm_w, *, row0, nrows, chained, h0=None,
              conv=None):
    q = dims.ssd_chunk
    g = dims.ssd_groups
    gw = dims.group_width
    n = dims.ssd_state
    r_heads, p = dims.heads_per_group, dims.ssd_head_dim
    assert LANES % p == 0 and q == LANES and 4 * r_heads <= q
    sel = _ssd_selector(dims)
    rb0 = row0 // q
    zoff = dims.off_z // gw
    boff = dims.ssd_inner // n
    coff = boff + g

    gpb = 1
    if chained:
        for cand_gpb in (4, 2):
            if all(v % cand_gpb == 0 for v in (g, zoff, boff)):
                gpb = cand_gpb
                break
    gblocks = g // gpb
    if chained:
        nb, nc = dims.batch, dims.seq // q
        grid = (nb, gblocks, nc)
        rowblk = lambda b, gi, c: b * nc + c
        sems = ("parallel", "parallel", "arbitrary")
        nseq = 1
    else:
        nseq = q // dims.dec_seq
        nb = dims.dec_batch // nseq
        grid = (nb, gblocks)
        rowblk = lambda b, gi: b
        sems = ("parallel", "parallel")

    def spec(shape, fn):
        return pl.BlockSpec(shape, fn)

    if chained:
        ix = lambda f: (lambda b, gi, c: f(rowblk(b, gi, c), gi))
    else:
        ix = lambda f: (lambda b, gi: f(rowblk(b, gi), gi))

    in_specs = [
        spec((q, gpb * gw), ix(lambda rb, gi: (rb, gi))),
        spec((q, gpb * n), ix(lambda rb, gi: (rb, boff // gpb + gi))),
        spec((q, gpb * n), ix(lambda rb, gi: (rb, coff // gpb + gi))),
        spec((q, gpb * gw), ix(lambda rb, gi: (rb0 + rb, zoff // gpb + gi))),
        spec((gpb, r_heads, q), ix(lambda rb, gi: (gi, 0, rb0 + rb))),
        spec((gpb, r_heads, 1), ix(lambda rb, gi: (gi, 0, 0))),
        spec((gpb, r_heads, 1), ix(lambda rb, gi: (gi, 0, 0))),
        spec((1, gpb * gw), ix(lambda rb, gi: (0, gi))),
        spec((1, gpb * gw), ix(lambda rb, gi: (0, gi))),
        spec((q, 3 * gw), ix(lambda rb, gi: (0, 0))),
    ]
    args = [xbc, xbc, xbc, proj, dt_t, bias_g, alog_g, d_exp, norm_w, sel]
    y_spec = spec((q, gpb * gw), ix(lambda rb, gi: (rb, gi)))
    if chained:
        conv_w, conv_b = conv
        xw, bw = gpb * gw, gpb * n
        assert dims.off_xbc % xw == 0 and (dims.off_xbc + dims.ssd_inner) % bw == 0 and (g * n) % bw == 0
        x0, b0 = dims.off_xbc // xw, (dims.off_xbc + dims.ssd_inner) // bw
        c0 = b0 + (g * n) // bw
        in_specs[0] = spec((q, xw), ix(lambda rb, gi: (rb0 + rb, x0 + gi)))
        in_specs[1] = spec((q, bw), ix(lambda rb, gi: (rb0 + rb, b0 + gi)))
        in_specs[2] = spec((q, bw), ix(lambda rb, gi: (rb0 + rb, c0 + gi)))
        args[0:3] = [proj, proj, proj]
        width = conv_w.shape[0]
        wb0 = dims.ssd_inner // bw
        wc0 = wb0 + (g * n) // bw
        in_specs += [spec((width, xw), ix(lambda rb, gi: (0, gi))),
                     spec((width, bw), ix(lambda rb, gi: (0, wb0 + gi))),
                     spec((width, bw), ix(lambda rb, gi: (0, wc0 + gi))),
                     spec((1, xw), ix(lambda rb, gi: (0, gi))),
                     spec((1, bw), ix(lambda rb, gi: (0, wb0 + gi))),
                     spec((1, bw), ix(lambda rb, gi: (0, wc0 + gi)))]
        conv_b2 = conv_b.reshape(1, dims.ssd_xbc)
        args += [conv_w, conv_w, conv_w, conv_b2, conv_b2, conv_b2]
        h_spec = pl.BlockSpec((1, gpb * r_heads, p, n), lambda b, gi, c: (b, gi, 0, 0))
        h_shape = jax.ShapeDtypeStruct((nb, dims.ssd_heads, p, n), F32)
        scratch = [pltpu.VMEM((gpb * r_heads * p, n), F32),
                   pltpu.VMEM((q, xw), F32), pltpu.VMEM((q, bw), F32), pltpu.VMEM((q, bw), F32),
                   pltpu.VMEM((q + SUBLANES, xw), F32), pltpu.VMEM((q + SUBLANES, bw), F32),
                   pltpu.VMEM((q + SUBLANES, bw), F32)]
    else:
        in_specs.append(pl.BlockSpec((nseq, gpb * r_heads, p, n), lambda b, gi: (b, gi, 0, 0)))
        args.append(h0)
        h_spec = pl.BlockSpec((nseq, gpb * r_heads, p, n), lambda b, gi: (b, gi, 0, 0))
        h_shape = jax.ShapeDtypeStruct((dims.dec_batch, dims.ssd_heads, p, n), F32)
        scratch = []
    return pl.pallas_call(
        functools.partial(_ssd_kernel, nseq=nseq, dims=dims, chained=chained, gpb=gpb),
        grid=grid,
        in_specs=in_specs,
        out_specs=[y_spec, h_spec],
        out_shape=[jax.ShapeDtypeStruct((nrows, dims.ssd_inner), BF16), h_shape],
        scratch_shapes=scratch,
        compiler_params=_params(*sems),
        name="ssd_prompt" if chained else "ssd_sample",
    )(*args)


def _attend(q, k, v, scale):
    s = lax.dot_general(q.astype(BF16), k.astype(BF16), (((1,), (1,)), ((), ())),
                        preferred_element_type=F32) * scale
    s = s - jnp.max(s, axis=-1, keepdims=True)
    e = jnp.exp(s)
    pr = (e / jnp.sum(e, axis=-1, keepdims=True)).astype(BF16)
    return jnp.dot(pr, v.astype(BF16), preferred_element_type=F32)


def _xattn_kernel(q_ref, k_ref, v_ref, o_ref, *, heads, head_dim):
    for h in range(heads):
        cs = slice(h * head_dim, (h + 1) * head_dim)
        o_ref[:, cs] = _attend(q_ref[:, cs], k_ref[0, :, cs], v_ref[0, :, cs], head_dim ** -0.5).astype(o_ref.dtype)


def _xattn_cached_kernel(q_ref, k_ref, v_ref, o_ref, *, heads, head_dim):
    tq = q_ref.shape[0]
    m = k_ref.shape[1]
    k = k_ref[0].reshape(m * heads, head_dim).astype(BF16)
    v = v_ref[0].reshape(m * heads, head_dim).astype(BF16)
    q = jnp.concatenate([q_ref[:, h * head_dim:(h + 1) * head_dim] for h in range(heads)], axis=0)
    s = lax.dot_general(q.astype(BF16), k, (((1,), (1,)), ((), ())), preferred_element_type=F32)
    s = s * (head_dim ** -0.5)
    row_head = lax.broadcasted_iota(jnp.int32, s.shape, 0) // tq
    col_head = lax.broadcasted_iota(jnp.int32, s.shape, 1) % heads
    s = jnp.where(row_head == col_head, s, NEG_INF)
    s = s - jnp.max(s, axis=-1, keepdims=True)
    e = jnp.exp(s)
    pr = (e / jnp.sum(e, axis=-1, keepdims=True)).astype(BF16)
    o = jnp.dot(pr, v, preferred_element_type=F32)
    for h in range(heads):
        o_ref[:, h * head_dim:(h + 1) * head_dim] = o[h * tq:(h + 1) * tq, :].astype(o_ref.dtype)


def cross_attention(dims, proj, k, v, *, row0, nseq, length, tq=512):
    w = dims.xatt_width
    tq = _pick(length, tq)
    nq = length // tq
    rb0 = row0 // tq
    assert row0 % tq == 0 and dims.off_q % w == 0
    qoff = dims.off_q // w
    hd, nh = dims.xatt_head_dim, dims.xatt_heads
    if k.ndim == 4:
        assert nq == 1 and tq % SUBLANES == 0
        kv_spec = pl.BlockSpec((1, dims.n_mem, nh, hd), lambda b: (b, 0, 0, 0))
        return pl.pallas_call(
            functools.partial(_xattn_cached_kernel, heads=nh, head_dim=hd),
            grid=(nseq,),
            in_specs=[pl.BlockSpec((tq, w), lambda b: (rb0 + b, qoff)), kv_spec, kv_spec],
            out_specs=pl.BlockSpec((tq, w), lambda b: (b, 0)),
            out_shape=jax.ShapeDtypeStruct((nseq * length, w), BF16),
            compiler_params=_params("parallel"),
            name="cross_attention_cached",
        )(proj, k, v)
    kv_spec = pl.BlockSpec((1, dims.n_mem, w), lambda b, i: (b, 0, 0))
    return pl.pallas_call(
        functools.partial(_xattn_kernel, heads=nh, head_dim=hd),
        grid=(nseq, nq),
        in_specs=[pl.BlockSpec((tq, w), lambda b, i: (rb0 + b * nq + i, qoff)), kv_spec, kv_spec],
        out_specs=pl.BlockSpec((tq, w), lambda b, i: (b * nq + i, 0)),
        out_shape=jax.ShapeDtypeStruct((nseq * length, w), BF16),
        compiler_params=_params("parallel", "parallel"),
        name="cross_attention",
    )(proj, k, v)


def _merge_kernel(vap_ref, ybp_ref, ocp_ref, vas_ref, ybs_ref, ocs_ref, wa_ref, wb_ref, wc_ref,
                  ga_ref, gb_ref, gc_ref, o_ref, *, npb):
    i = pl.program_id(1)

    def body(va_ref, yb_ref, oc_ref):
        ha = jnp.dot(va_ref[...], wa_ref[...], preferred_element_type=F32)
        hb = jnp.dot(yb_ref[...], wb_ref[...], preferred_element_type=F32)
        hc = jnp.dot(oc_ref[...], wc_ref[...], preferred_element_type=F32)
        mix = (jax.nn.sigmoid(ga_ref[...]) * ha + jax.nn.sigmoid(gb_ref[...]) * hb
               + jax.nn.sigmoid(gc_ref[...]) * hc)
        o_ref[...] = mix.astype(o_ref.dtype)

    @pl.when(i < npb)
    def _():
        body(vap_ref, ybp_ref, ocp_ref)

    @pl.when(i >= npb)
    def _():
        body(vas_ref, ybs_ref, ocs_ref)


def merge_branches(dims, prompt, sample, wa, wb, wc, proj, tm=512, tn=512):
    d = dims.d_model
    mp, ms = prompt[0].shape[0], sample[0].shape[0]
    tm, tn = _pick(ms, _pick(mp, tm)), _pick(d, tn)
    npb = mp // tm
    g0 = dims.off_gates // tn
    gd = d // tn
    assert dims.off_gates % tn == 0

    def lhs_p(a):
        return pl.BlockSpec((tm, a.shape[1]), lambda j, i: (jnp.minimum(i, npb - 1), 0))

    def lhs_s(a):
        return pl.BlockSpec((tm, a.shape[1]), lambda j, i: (jnp.maximum(i - npb, 0), 0))

    def rhs(width):
        return pl.BlockSpec((width, tn), lambda j, i: (0, j), pipeline_mode=pl.Buffered(1))

    def gate(k):
        return pl.BlockSpec((tm, tn), lambda j, i, k=k: (i, g0 + k * gd + j))

    return pl.pallas_call(
        functools.partial(_merge_kernel, npb=npb),
        grid=(d // tn, (mp + ms) // tm),
        in_specs=[lhs_p(a) for a in prompt] + [lhs_s(a) for a in sample]
        + [rhs(wa.shape[0]), rhs(wb.shape[0]), rhs(wc.shape[0]), gate(0), gate(1), gate(2)],
        out_specs=pl.BlockSpec((tm, tn), lambda j, i: (i, j)),
        out_shape=jax.ShapeDtypeStruct((mp + ms, d), BF16),
        compiler_params=_params("arbitrary", "arbitrary"),
        name="merge_branches",
    )(*prompt, *sample, wa, wb, wc, proj, proj, proj)


def _peer_select_kernel(q_ref, keys_ref, c1_ref, e1_ref, r2_ref, e2_ref, scores, work, tops, cand, ranks, *,
                        dims):
    heads, nk, topk = dims.peer_heads, dims.peer_keys, dims.peer_topk
    half = dims.peer_qdim // 2
    tb = q_ref.shape[0]
    rank = lax.broadcasted_iota(jnp.int32, (topk, tb), 0)

    for k in range(2 * heads):
        qh = q_ref[:, k * half:(k + 1) * half].astype(BF16)
        s = lax.dot_general(keys_ref[k].astype(BF16), qh, (((1,), (1,)), ((), ())),
                            preferred_element_type=F32)
        scores[k] = s
        work[k] = s
        tops[k] = jnp.full((topk, tb), NEG_INF, F32)

    for h in range(heads):
        ranks[h] = jnp.full((nk, tb), float(topk), F32)

    def extract(r, carry):
        rf = jnp.asarray(r, F32)
        for k in range(2 * heads):
            s = work[k]
            m = jnp.max(s, axis=0, keepdims=True)
            hit = s == m
            tops[k] = jnp.where(rank == r, m, tops[k])
            work[k] = jnp.where(hit, NEG_INF, s)
            if k % 2:
                ranks[k // 2] = jnp.where(hit, rf, ranks[k // 2])
        return carry
    lax.fori_loop(0, topk, extract, 0)

    for h in range(heads):
        a, b = tops[2 * h], tops[2 * h + 1]
        cand[h, 0:topk, :] = a[0:1, :] + b
        for pi in range(1, topk):
            lo = topk + (pi - 1) * SUBLANES
            cand[h, lo:lo + SUBLANES, :] = a[pi:pi + 1, :] + b[0:SUBLANES, :]

    def threshold(r, taus):
        new = []
        for h in range(heads):
            c = cand[h]
            m = jnp.max(c, axis=0, keepdims=True)
            cand[h] = jnp.where(c == m, NEG_INF, c)
            new.append(m)
        return tuple(new)
    taus = lax.fori_loop(0, topk, threshold, tuple(jnp.zeros((1, tb), F32) for _ in range(heads)))

    for h in range(heads):
        a, b = tops[2 * h], tops[2 * h + 1]
        tau = taus[h]
        z = None
        for pi in range(topk):
            rows = topk if pi == 0 else SUBLANES
            c = a[pi:pi + 1, :] + b[0:rows, :]
            part = jnp.sum(jnp.where(c >= tau, jnp.exp(c - (a[0:1, :] + b[0:1, :])), 0.0), axis=0, keepdims=True)
            z = part if z is None else z + part
        s1, s2 = scores[2 * h], scores[2 * h + 1]
        count = jnp.zeros_like(s1)
        for qi in range(topk):
            reach = a + b[qi:qi + 1, :] >= tau
            alpha = jnp.min(jnp.where(reach, a, float("inf")), axis=0, keepdims=True)
            count = count + jnp.where(s1 >= alpha, 1.0, 0.0)
        c1_ref[h, :, 0, :] = count
        r2_ref[h] = ranks[h].astype(r2_ref.dtype)
        e1_ref[h, :, 0, :] = jnp.exp(s1 - a[0:1, :]) * (0.5 / z)
        e2_ref[h] = jnp.exp(s2 - b[0:1, :]).astype(e2_ref.dtype)


def peer_select(dims, q, subkeys, tb=128):
    t = q.shape[0]
    heads, nk = dims.peer_heads, dims.peer_keys
    half = dims.peer_qdim // 2
    assert dims.peer_topk >= SUBLANES and heads == SUBLANES
    tab = jax.ShapeDtypeStruct((heads, nk, 1, t), F32)
    tab16 = jax.ShapeDtypeStruct((heads, nk, t), BF16)
    row_spec = pl.BlockSpec((heads, nk, 1, tb), lambda i: (0, 0, 0, i))
    tab_spec = pl.BlockSpec((heads, nk, tb), lambda i: (0, 0, i))
    return pl.pallas_call(
        functools.partial(_peer_select_kernel, dims=dims),
        grid=(t // tb,),
        in_specs=[pl.BlockSpec((tb, q.shape[1]), lambda i: (i, 0)),
                  pl.BlockSpec((2 * heads, nk, half), lambda i: (0, 0, 0))],
        out_specs=[row_spec, row_spec, tab_spec, tab_spec],
        out_shape=[tab, tab, tab16, tab16],
        scratch_shapes=[pltpu.VMEM((2 * heads, nk, tb), F32),
                        pltpu.VMEM((2 * heads, nk, tb), F32),
                        pltpu.VMEM((2 * heads, dims.peer_topk, tb), F32),
                        pltpu.VMEM((heads, dims.peer_topk + (dims.peer_topk - 1) * SUBLANES, tb), F32),
                        pltpu.VMEM((heads, nk, tb), F32)],
        compiler_params=_params("parallel"),
        name="peer_select",
    )(q, subkeys.reshape(2 * heads, nk, half))


def _peer_mix_kernel(x_ref, u_ref, v_ref, c1_ref, e1_ref, r2_ref, e2_ref, o_ref, gw_scr, w_scr, *, dims, sub):
    heads, nk = dims.peer_heads, dims.peer_keys
    eb = pl.program_id(1)
    te = u_ref.shape[0]

    @pl.when(eb == 0)
    def _():
        o_ref[...] = jnp.zeros_like(o_ref)

    tb = x_ref.shape[0]
    tr, tl = 2 * SUBLANES, LANES
    zero = jnp.zeros((tr, tl), BF16)
    for ii in range(te // nk):
        for c0 in range(0, tb, tl):
            cs = slice(c0, c0 + tl)
            acc = [None] * (nk // tr)
            for h in range(heads):
                count = jnp.broadcast_to(c1_ref[h, ii, :, cs], (tr, tl)).astype(BF16)
                e1 = jnp.broadcast_to(e1_ref[h, ii, :, cs], (tr, tl)).astype(BF16)
                for s in range(nk // tr):
                    rs = slice(s * tr, (s + 1) * tr)
                    g1 = jnp.minimum(jnp.maximum(count - r2_ref[h, rs, cs], zero), e1)
                    wh = g1 * e2_ref[h, rs, cs]
                    acc[s] = wh if acc[s] is None else acc[s] + wh
            for s in range(nk // tr):
                w_scr[ii * nk + s * tr:ii * nk + (s + 1) * tr, cs] = acc[s]
    x = x_ref[...]
    for sb in range(te // sub):
        rows = slice(sb * sub, (sb + 1) * sub)
        hh = lax.dot_general(u_ref[rows, :], x, (((1,), (1,)), ((), ())), preferred_element_type=F32)
        gelu2 = hh * (1.0 + lax.erf(hh * (2.0 ** -0.5)))
        gw_scr[rows, :] = gelu2.astype(BF16) * w_scr[rows, :]
        o_ref[...] += lax.dot_general(gw_scr[rows, :], v_ref[rows, :], (((0,), (0,)), ((), ())),
                                      preferred_element_type=F32)


def peer_mix(dims, xn, u, v, c1, e1, r2, e2, tb=512, te=512, sub=512):
    t, d = xn.shape
    heads, nk = dims.peer_heads, dims.peer_keys
    tb, te = _pick(t, tb), _pick(dims.n_experts, te)
    sub = min(sub, te)
    assert te % sub == 0 and sub % nk == 0
    ni = te // nk
    row_spec = pl.BlockSpec((heads, ni, 1, tb), lambda i, e: (0, e, 0, i))
    col_spec = pl.BlockSpec((heads, nk, tb), lambda i, e: (0, 0, i))
    return pl.pallas_call(
        functools.partial(_peer_mix_kernel, dims=dims, sub=sub),
        grid=(t // tb, dims.n_experts // te),
        in_specs=[pl.BlockSpec((tb, d), lambda i, e: (i, 0)),
                  pl.BlockSpec((te, d), lambda i, e: (e, 0)),
                  pl.BlockSpec((te, d), lambda i, e: (e, 0)),
                  row_spec, row_spec, col_spec, col_spec],
        out_specs=pl.BlockSpec((tb, d), lambda i, e: (i, 0)),
        out_shape=jax.ShapeDtypeStruct((t, d), F32),
        scratch_shapes=[pltpu.VMEM((te, tb), BF16), pltpu.VMEM((te, tb), BF16)],
        compiler_params=_params("parallel", "arbitrary"),
        name="peer_mix",
    )(xn, u, v, c1, e1, r2, e2)


def _pad_prev(state, width):
    return jnp.pad(state, ((0, 0), (SUBLANES - (width - 1), 0), (0, 0)))


def forward(dims, x_prompt, x_sample, mem_prompt, cache_mem_k, cache_mem_v, state_conv_a,
            state_ssd_conv, state_ssd, norm_mix, norm_mem, norm_ffn, norm_final, w_in,
            a_conv_w, a_out, ssd_conv_w, ssd_conv_b, ssd_dt_bias, ssd_a_log, ssd_d, ssd_norm,
            ssd_out, w_mem_k, w_mem_v, xatt_out, w_o, peer_wq, peer_subkeys, peer_u, peer_v):
    d = dims.d_model
    tp, ts, t = dims.t_prompt, dims.t_sample, dims.tokens
    g, rh, p, n = dims.ssd_groups, dims.heads_per_group, dims.ssd_head_dim, dims.ssd_state
    bf = lambda a: a.astype(BF16)

    xp2, xs2 = x_prompt.reshape(tp, d), x_sample.reshape(ts, d)

    mn = rmsnorm(mem_prompt.reshape(dims.batch * dims.n_mem, d), norm_mem[0], BF16)
    xw = dims.xatt_width
    k_p = matmul(mn, w_mem_k, F32, tn=512, name="mem_k").reshape(dims.batch, dims.n_mem, xw)
    v_p = matmul(mn, w_mem_v, F32, tn=512, name="mem_v").reshape(dims.batch, dims.n_mem, xw)
    mem_k_p = k_p.reshape(1, dims.batch, dims.n_mem, dims.xatt_heads, dims.xatt_head_dim)
    mem_v_p = v_p.reshape(1, dims.batch, dims.n_mem, dims.xatt_heads, dims.xatt_head_dim)

    xn = rmsnorm_parts(xp2, xs2, norm_mix[0], BF16)
    wt = jnp.swapaxes(w_in, 1, 2)[0]
    dt0 = 3 * dims.a_width + dims.ssd_inner + dims.ssd_xbc
    proj = matmul_wt(xn, wt, [(0, dt0), (dt0 + dims.ssd_heads, dims.proj_width - dt0)], tm=512, tn=1024,
                     name="in_proj")
    dt_t = dt_proj_t(xn, wt, dt0, dims.ssd_heads).reshape(g, rh, t)

    a_cols = (dims.off_ain, dims.off_abg, dims.off_acg)
    zeros_a = jnp.zeros((dims.batch, SUBLANES, dims.a_width), F32)
    va_p, st_a_p = gated_conv(proj, zeros_a, a_conv_w[0], row0=0, nseq=dims.batch, length=dims.seq,
                              cols=dims.a_width, col_offs=a_cols, out_dtype=BF16)
    prev_a = _pad_prev(state_conv_a[0], dims.a_conv).reshape(dims.dec_batch * SUBLANES, dims.a_width)
    va_s, u_s = short_seq_conv(proj, prev_a, a_conv_w[0], row0=tp, nseq=dims.dec_batch, seq=dims.dec_seq,
                               cols=dims.a_width, col_offs=a_cols, gated=True, out_dtype=BF16)
    na = dims.a_conv - 1
    conv_a_p = st_a_p[None, :, SUBLANES - na:, :]
    conv_a_s = u_s.reshape(dims.dec_batch, dims.dec_seq, dims.a_width)[None, :, dims.dec_seq - na:, :]

    prev_b = _pad_prev(state_ssd_conv[0], dims.ssd_conv).reshape(dims.dec_batch * SUBLANES, dims.ssd_xbc)
    xbc_s = short_seq_conv(proj, prev_b, ssd_conv_w[0], row0=tp, nseq=dims.dec_batch, seq=dims.dec_seq,
                           cols=dims.ssd_xbc, col_offs=(dims.off_xbc,), gated=False, bias=ssd_conv_b[0])
    nb = dims.ssd_conv - 1
    x0, x1c = dims.off_xbc, dims.off_xbc + dims.ssd_xbc
    ssd_conv_p = jnp.stack([lax.slice(proj, ((b + 1) * dims.seq - nb, x0), ((b + 1) * dims.seq, x1c))
                            for b in range(dims.batch)])[None]
    ssd_conv_s = lax.slice(proj.reshape(t // dims.dec_seq, dims.dec_seq, dims.proj_width),
                           (tp // dims.dec_seq, dims.dec_seq - nb, x0),
                           (t // dims.dec_seq, dims.dec_seq, x1c))[None]

    bias_g = ssd_dt_bias[0].reshape(g, rh, 1)
    alog_g = ssd_a_log[0].reshape(g, rh, 1)
    d_exp = jnp.repeat(ssd_d[0], p).reshape(1, dims.ssd_inner)
    norm_w = ssd_norm[0].reshape(1, dims.ssd_inner)
    yb_p, h_p = ssd_block(dims, None, proj, dt_t, bias_g, alog_g, d_exp, norm_w, row0=0, nrows=tp, chained=True,
                          conv=(ssd_conv_w[0], ssd_conv_b[0]))
    yb_s, h_s = ssd_block(dims, xbc_s, proj, dt_t, bias_g, alog_g, d_exp, norm_w, row0=tp, nrows=ts,
                          chained=False, h0=state_ssd[0])

    oc_p = cross_attention(dims, proj, k_p, v_p, row0=0, nseq=dims.batch, length=dims.seq)
    oc_s = cross_attention(dims, proj, cache_mem_k[0], cache_mem_v[0], row0=tp, nseq=dims.dec_batch,
                           length=dims.dec_seq)

    mix = merge_branches(dims, (va_p, yb_p, oc_p), (va_s, yb_s, oc_s), bf(a_out[0]), bf(ssd_out[0]),
                         bf(xatt_out[0]), proj)
    x1 = matmul(mix, bf(w_o[0]), F32, residual_parts=(xp2, xs2), name="out_proj")

    xn2 = rmsnorm(x1, norm_ffn[0], BF16)
    qp = matmul(xn2, bf(peer_wq[0]), F32, name="peer_query")
    c1, e1, r2, e2 = peer_select(dims, qp, peer_subkeys[0])
    ffn = peer_mix(dims, xn2, cast_layer(peer_u), cast_layer(peer_v), c1, e1, r2, e2)
    y_prompt = rmsnorm(x1, norm_final, F32, residual=ffn, row0=0, nrows=tp).reshape(dims.batch, dims.seq, d)
    y_sample = rmsnorm(x1, norm_final, F32, residual=ffn, row0=tp, nrows=ts).reshape(
        dims.dec_batch, dims.dec_seq, d)
    return (y_prompt, y_sample, mem_k_p, mem_v_p, conv_a_p, ssd_conv_p, h_p[None],
            conv_a_s, ssd_conv_s, h_s[None])


def kernel(x_prompt, x_sample, mem_prompt, cache_mem_k, cache_mem_v, state_conv_a, state_ssd_conv, state_ssd, norm_mix, norm_mem, norm_ffn, norm_final, w_in, a_conv_w, a_out, ssd_conv_w, ssd_conv_b, ssd_dt_bias, ssd_a_log, ssd_d, ssd_norm, ssd_out, w_mem_k, w_mem_v, xatt_out, w_o, peer_wq, peer_subkeys, peer_u, peer_v):
    return forward(FULL, x_prompt, x_sample, mem_prompt, cache_mem_k, cache_mem_v, state_conv_a,
                   state_ssd_conv, state_ssd, norm_mix, norm_mem, norm_ffn, norm_final, w_in,
                   a_conv_w, a_out, ssd_conv_w, ssd_conv_b, ssd_dt_bias, ssd_a_log, ssd_d, ssd_norm,
                   ssd_out, w_mem_k, w_mem_v, xatt_out, w_o, peer_wq, peer_subkeys, peer_u, peer_v)
```

```python
import dataclasses
import functools

import jax
import jax.numpy as jnp
import numpy as np
from jax import lax
from jax.experimental import pallas as pl
from jax.experimental.pallas import tpu as pltpu

F32 = jnp.float32
BF16 = jnp.bfloat16
EPS = 1e-6
HIGHEST = lax.Precision.HIGHEST
NEG_INF = float("-inf")

LANES = 128
SUBLANES = 8
VMEM_LIMIT_BYTES = 56 * 1024 * 1024


@dataclasses.dataclass(frozen=True)
class Dims:
    d_model: int = 4096
    batch: int = 4
    seq: int = 2048
    dec_batch: int = 128
    dec_seq: int = 8
    a_width: int = 2048
    a_conv: int = 3
    ssd_inner: int = 4096
    ssd_head_dim: int = 64
    ssd_groups: int = 8
    ssd_state: int = 128
    ssd_conv: int = 4
    ssd_chunk: int = 128
    n_mem: int = 256
    xatt_heads: int = 4
    xatt_head_dim: int = 512
    peer_heads: int = 8
    peer_keys: int = 128
    peer_topk: int = 16
    peer_qdim: int = 256

    @property
    def ssd_heads(self):
        return self.ssd_inner // self.ssd_head_dim

    @property
    def heads_per_group(self):
        return self.ssd_heads // self.ssd_groups

    @property
    def group_width(self):
        return self.ssd_inner // self.ssd_groups

    @property
    def ssd_xbc(self):
        return self.ssd_inner + 2 * self.ssd_groups * self.ssd_state

    @property
    def xatt_width(self):
        return self.xatt_heads * self.xatt_head_dim

    @property
    def n_experts(self):
        return self.peer_keys * self.peer_keys

    @property
    def t_prompt(self):
        return self.batch * self.seq

    @property
    def t_sample(self):
        return self.dec_batch * self.dec_seq

    @property
    def tokens(self):
        return self.t_prompt + self.t_sample

    @property
    def off_ain(self):
        return 0

    @property
    def off_abg(self):
        return self.a_width

    @property
    def off_acg(self):
        return 2 * self.a_width

    @property
    def off_z(self):
        return 3 * self.a_width

    @property
    def off_xbc(self):
        return self.off_z + self.ssd_inner

    @property
    def off_q(self):
        return self.off_xbc + self.ssd_xbc

    @property
    def off_gates(self):
        return self.off_q + self.xatt_width

    @property
    def proj_width(self):
        return self.off_gates + 3 * self.d_model


FULL = Dims()


def _params(*sem):
    return pltpu.CompilerParams(dimension_semantics=sem, vmem_limit_bytes=VMEM_LIMIT_BYTES)


def _pick(n, pref):
    t = min(n, pref)
    while n % t:
        t //= 2
    return t


def _rmsnorm_rows(x, g_ref, o_ref):
    inv = lax.rsqrt(jnp.mean(x * x, axis=-1, keepdims=True) + EPS)
    o_ref[...] = ((x * inv) * g_ref[...]).astype(o_ref.dtype)


def _rmsnorm_kernel(x_ref, g_ref, o_ref):
    _rmsnorm_rows(x_ref[...], g_ref, o_ref)


def _add_rmsnorm_kernel(x_ref, r_ref, g_ref, o_ref):
    _rmsnorm_rows(x_ref[...] + r_ref[...], g_ref, o_ref)


def _rmsnorm_parts_kernel(xp_ref, xs_ref, g_ref, o_ref, *, npb):
    i = pl.program_id(0)

    @pl.when(i < npb)
    def _():
        _rmsnorm_rows(xp_ref[...], g_ref, o_ref)

    @pl.when(i >= npb)
    def _():
        _rmsnorm_rows(xs_ref[...], g_ref, o_ref)


def rmsnorm(x, g, out_dtype, residual=None, row0=0, nrows=None, tm=256):
    m, d = x.shape
    nrows = m - row0 if nrows is None else nrows
    tm = _pick(nrows, tm)
    rb0 = row0 // tm
    assert row0 % tm == 0
    row = pl.BlockSpec((tm, d), lambda i: (rb0 + i, 0))
    gspec = pl.BlockSpec((1, d), lambda i: (0, 0))
    args = (x,) if residual is None else (x, residual)
    return pl.pallas_call(
        _rmsnorm_kernel if residual is None else _add_rmsnorm_kernel,
        grid=(nrows // tm,),
        in_specs=[row] * len(args) + [gspec],
        out_specs=pl.BlockSpec((tm, d), lambda i: (i, 0)),
        out_shape=jax.ShapeDtypeStruct((nrows, d), out_dtype),
        compiler_params=_params("parallel"),
        name="rmsnorm" if residual is None else "add_rmsnorm",
    )(*args, g.reshape(1, d))


def rmsnorm_parts(xp, xs, g, out_dtype, tm=256):
    d = xp.shape[1]
    tm = _pick(xs.shape[0], _pick(xp.shape[0], tm))
    npb, nsb = xp.shape[0] // tm, xs.shape[0] // tm
    return pl.pallas_call(
        functools.partial(_rmsnorm_parts_kernel, npb=npb),
        grid=(npb + nsb,),
        in_specs=[pl.BlockSpec((tm, d), lambda i: (jnp.minimum(i, npb - 1), 0)),
                  pl.BlockSpec((tm, d), lambda i: (jnp.maximum(i - npb, 0), 0)),
                  pl.BlockSpec((1, d), lambda i: (0, 0))],
        out_specs=pl.BlockSpec((tm, d), lambda i: (i, 0)),
        out_shape=jax.ShapeDtypeStruct((xp.shape[0] + xs.shape[0], d), out_dtype),
        compiler_params=_params("arbitrary"),
        name="rmsnorm_parts",
    )(xp, xs, g.reshape(1, d))


def _cast_kernel(x_ref, o_ref):
    o_ref[...] = x_ref[0].astype(o_ref.dtype)


def cast_layer(w, dtype=BF16, tr=512):
    _, r, c = w.shape
    tr = _pick(r, tr)
    return pl.pallas_call(
        _cast_kernel,
        grid=(r // tr,),
        in_specs=[pl.BlockSpec((1, tr, c), lambda i: (0, i, 0))],
        out_specs=pl.BlockSpec((tr, c), lambda i: (i, 0)),
        out_shape=jax.ShapeDtypeStruct((r, c), dtype),
        compiler_params=_params("parallel"),
        name="cast_layer",
    )(w)


def _mm_kernel(*refs, npb, layer_weight):
    a_ref, b_ref = refs[:2]
    refs = refs[2:]
    if npb is not None:
        rp_ref, rs_ref = refs[:2]
        refs = refs[2:]
    o_ref = refs[0]
    i = pl.program_id(1)
    if layer_weight:
        w_scr = refs[1]

        @pl.when(i == 0)
        def _():
            w_scr[...] = b_ref[0].astype(BF16)

        w = w_scr[...]
    else:
        w = b_ref[...]
    acc = jnp.dot(a_ref[...], w, preferred_element_type=F32)
    if npb is None:
        o_ref[...] = acc.astype(o_ref.dtype)
    else:
        @pl.when(i < npb)
        def _():
            o_ref[...] = (rp_ref[...] + acc).astype(o_ref.dtype)

        @pl.when(i >= npb)
        def _():
            o_ref[...] = (rs_ref[...] + acc).astype(o_ref.dtype)


def matmul(a, b, out_dtype, residual_parts=None, tm=512, tn=1024, name="matmul"):
    m, k = a.shape
    layer_weight = b.ndim == 3
    n = b.shape[-1]
    tm, tn = _pick(m, tm), _pick(n, tn)
    npb = None
    extra_specs, extra_args = [], []
    if residual_parts is not None:
        rp, rs = residual_parts
        tm = _pick(rs.shape[0], _pick(rp.shape[0], tm))
        npb = rp.shape[0] // tm
        extra_specs = [pl.BlockSpec((tm, tn), lambda j, i: (jnp.minimum(i, npb - 1), j)),
                       pl.BlockSpec((tm, tn), lambda j, i: (jnp.maximum(i - npb, 0), j))]
        extra_args = [rp, rs]
    if layer_weight:
        b_spec = pl.BlockSpec((1, k, tn), lambda j, i: (0, 0, j))
        scratch = [pltpu.VMEM((k, tn), BF16)]
    else:
        b_spec = pl.BlockSpec((k, tn), lambda j, i: (0, j))
        scratch = []
    sequential = layer_weight or npb is not None
    return pl.pallas_call(
        functools.partial(_mm_kernel, npb=npb, layer_weight=layer_weight),
        grid=(n // tn, m // tm),
        in_specs=[pl.BlockSpec((tm, k), lambda j, i: (i, 0)), b_spec] + extra_specs,
        out_specs=pl.BlockSpec((tm, tn), lambda j, i: (i, j)),
        out_shape=jax.ShapeDtypeStruct((m, n), out_dtype),
        scratch_shapes=scratch,
        compiler_params=_params(*(("arbitrary", "arbitrary") if sequential else ("parallel", "parallel"))),
        name=name,
    )(a, b, *extra_args)


def _mm_res_norm_kernel(a_ref, w_ref, rp_ref, rs_ref, g_ref, x_ref, xn_ref, row_scr, *, npb, nj, tn):
    i, j = pl.program_id(0), pl.program_id(1)
    acc = jnp.dot(a_ref[...], w_ref[...], preferred_element_type=F32)

    def finish(res_ref):
        x = res_ref[...] + acc
        x_ref[...] = x
        row_scr[j] = x

    @pl.when(i < npb)
    def _():
        finish(rp_ref)

    @pl.when(i >= npb)
    def _():
        finish(rs_ref)

    @pl.when(j == nj - 1)
    def _():
        ssq = None
        for jj in range(nj):
            part = jnp.sum(row_scr[jj] * row_scr[jj], axis=-1, keepdims=True)
            ssq = part if ssq is None else ssq + part
        inv = lax.rsqrt(ssq * (1.0 / (nj * tn)) + EPS)
        for jj in range(nj):
            cs = slice(jj * tn, (jj + 1) * tn)
            xn_ref[:, cs] = ((row_scr[jj] * inv) * g_ref[:, cs]).astype(xn_ref.dtype)


def matmul_residual_norm(a, b, residual_parts, g, norm_dtype, tm=512, tn=1024, name="matmul_res_norm"):
    m, k = a.shape
    n = b.shape[1]
    rp, rs = residual_parts
    tm, tn = _pick(rs.shape[0], _pick(rp.shape[0], _pick(m, tm))), _pick(n, tn)
    npb, nj = rp.shape[0] // tm, n // tn
    return pl.pallas_call(
        functools.partial(_mm_res_norm_kernel, npb=npb, nj=nj, tn=tn),
        grid=(m // tm, nj),
        in_specs=[pl.BlockSpec((tm, k), lambda i, j: (i, 0)),
                  pl.BlockSpec((k, tn), lambda i, j: (0, j)),
                  pl.BlockSpec((tm, tn), lambda i, j: (jnp.minimum(i, npb - 1), j)),
                  pl.BlockSpec((tm, tn), lambda i, j: (jnp.maximum(i - npb, 0), j)),
                  pl.BlockSpec((1, n), lambda i, j: (0, 0))],
        out_specs=[pl.BlockSpec((tm, tn), lambda i, j: (i, j)), pl.BlockSpec((tm, n), lambda i, j: (i, 0))],
        out_shape=[jax.ShapeDtypeStruct((m, n), F32), jax.ShapeDtypeStruct((m, n), norm_dtype)],
        scratch_shapes=[pltpu.VMEM((nj, tm, tn), F32)],
        compiler_params=pltpu.CompilerParams(dimension_semantics=("arbitrary", "arbitrary"),
                                             vmem_limit_bytes=VMEM_LIMIT_BYTES + 2 * 1024 * 1024),
        name=name,
    )(a, b, rp, rs, g.reshape(1, n))


def _mm_wt_kernel(a_ref, wt_ref, o_ref, w_scr):
    @pl.when(pl.program_id(1) == 0)
    def _():
        w_scr[...] = wt_ref[...].astype(BF16)

    o_ref[...] = lax.dot_general(a_ref[...], w_scr[...], (((1,), (1,)), ((), ())),
                                 preferred_element_type=F32).astype(o_ref.dtype)


def matmul_wt(a, wt, segments, out_dtype=F32, tm=1024, tn=512, name="matmul_wt"):
    m, k = a.shape
    tm = _pick(m, tm)
    for start, length in segments:
        tn = _pick(length, tn)
        assert start % SUBLANES == 0
    assert all(length % tn == 0 for _, length in segments)
    starts = np.concatenate([np.arange(s, s + l, tn) for s, l in segments]).astype(np.int32)
    nblk = len(starts)
    bounds = np.cumsum([l // tn for _, l in segments])[:-1]
    shifts = [segments[i + 1][0] - (segments[i][0] + segments[i][1]) for i in range(len(segments) - 1)]

    def row_start(j):
        r = segments[0][0] + j * tn
        for b, sh in zip(bounds, shifts):
            r = r + jnp.where(j >= b, sh, 0)
        return pl.multiple_of(r, SUBLANES)

    return pl.pallas_call(
        _mm_wt_kernel,
        grid=(nblk, m // tm),
        in_specs=[pl.BlockSpec((tm, k), lambda j, i: (i, 0)),
                  pl.BlockSpec((pl.Element(tn), pl.Element(k)), lambda j, i: (row_start(j), 0))],
        out_specs=pl.BlockSpec((tm, tn), lambda j, i: (i, j)),
        out_shape=jax.ShapeDtypeStruct((m, nblk * tn), out_dtype),
        scratch_shapes=[pltpu.VMEM((tn, k), BF16)],
        compiler_params=_params("arbitrary", "arbitrary"),
        name=name,
    )(a, wt)


def _dt_proj_kernel(a_ref, wt_ref, o_ref):
    o_ref[...] = lax.dot_general(wt_ref[...].astype(BF16), a_ref[...], (((1,), (1,)), ((), ())),
                                 preferred_element_type=F32)


def dt_proj_t(a, wt, start, nrows, tm=1024):
    m, k = a.shape
    tm = _pick(m, tm)
    assert start % SUBLANES == 0
    return pl.pallas_call(
        _dt_proj_kernel,
        grid=(m // tm,),
        in_specs=[pl.BlockSpec((tm, k), lambda i: (i, 0)),
                  pl.BlockSpec((pl.Element(nrows), pl.Element(k)), lambda i: (start, 0))],
        out_specs=pl.BlockSpec((nrows, tm), lambda i: (0, i)),
        out_shape=jax.ShapeDtypeStruct((nrows, m), F32),
        compiler_params=_params("parallel"),
        name="dt_proj",
    )(a, wt)


def _gated_conv_kernel(in_ref, bg_ref, cg_ref, prev_ref, w_ref, o_ref, st_ref, scr, *, width, tl):
    lt = pl.program_id(2)

    @pl.when(lt == 0)
    def _():
        scr[0:SUBLANES, :] = prev_ref[0]

    u = cg_ref[...] * in_ref[...]
    scr[SUBLANES:SUBLANES + tl, :] = u
    acc = w_ref[width - 1:width, :] * u
    for k in range(width - 1):
        lo = SUBLANES - (width - 1 - k)
        acc = acc + w_ref[k:k + 1, :] * scr[lo:lo + tl, :]
    tail = scr[tl:tl + SUBLANES, :]
    scr[0:SUBLANES, :] = tail
    o_ref[...] = (bg_ref[...] * acc).astype(o_ref.dtype)
    st_ref[0] = tail


def gated_conv(proj, prev8, w, *, row0, nseq, length, cols, col_offs, out_dtype, tl=1024, tc=512):
    width = w.shape[0]
    tl, tc = _pick(length, tl), _pick(cols, tc)
    while any(o % tc for o in col_offs):
        tc //= 2
    nl = length // tl
    rb0 = row0 // tl
    assert row0 % tl == 0

    def blk(off):
        return pl.BlockSpec((tl, tc), lambda s, c, l, off=off: (rb0 + s * nl + l, off // tc + c))

    carry_spec = pl.BlockSpec((1, SUBLANES, tc), lambda s, c, l: (s, 0, c))
    return pl.pallas_call(
        functools.partial(_gated_conv_kernel, width=width, tl=tl),
        grid=(nseq, cols // tc, nl),
        in_specs=[blk(col_offs[0]), blk(col_offs[1]), blk(col_offs[2]), carry_spec,
                  pl.BlockSpec((width, tc), lambda s, c, l: (0, c))],
        out_specs=[pl.BlockSpec((tl, tc), lambda s, c, l: (s * nl + l, c)), carry_spec],
        out_shape=[jax.ShapeDtypeStruct((nseq * length, cols), out_dtype),
                   jax.ShapeDtypeStruct((nseq, SUBLANES, cols), F32)],
        scratch_shapes=[pltpu.VMEM((tl + SUBLANES, tc), F32)],
        compiler_params=_params("parallel", "parallel", "arbitrary"),
        name="gated_conv",
    )(proj, proj, proj, prev8, w)


def _short_conv_kernel(*refs, width, seq, gated):
    if gated:
        in_ref, bg_ref, cg_ref, prev_ref, w_ref, o_ref, u_ref, scr_u, scr_p = refs
    else:
        in_ref, prev_ref, w_ref, bias_ref, o_ref, scr_u, scr_p = refs
    rows = in_ref.shape[0]
    u = cg_ref[...] * in_ref[...] if gated else in_ref[...]
    zeros = jnp.zeros((SUBLANES, u.shape[1]), F32)
    scr_u[0:SUBLANES, :] = zeros
    scr_u[SUBLANES:SUBLANES + rows, :] = u
    scr_p[0:rows, :] = prev_ref[...]
    scr_p[rows:rows + SUBLANES, :] = zeros
    pos = lax.broadcasted_iota(jnp.int32, u.shape, 0) % seq
    acc = w_ref[width - 1:width, :] * u
    for k in range(width - 1):
        shift = width - 1 - k
        lo = SUBLANES - shift
        operand = jnp.where(pos >= shift, scr_u[lo:lo + rows, :], scr_p[lo:lo + rows, :])
        acc = acc + w_ref[k:k + 1, :] * operand
    if gated:
        o_ref[...] = (bg_ref[...] * acc).astype(o_ref.dtype)
        u_ref[...] = u
    else:
        y = acc + bias_ref[...]
        o_ref[...] = (y * jax.nn.sigmoid(y)).astype(o_ref.dtype)


def short_seq_conv(proj, prev8, w, *, row0, nseq, seq, cols, col_offs, gated, bias=None, out_dtype=F32,
                   rows=128, tc=2048):
    width = w.shape[0]
    assert seq == SUBLANES and width - 1 <= seq
    total = nseq * seq
    rows, tc = _pick(total, rows), _pick(cols, tc)
    while any(o % tc for o in col_offs):
        tc //= 2
    rb0 = row0 // rows
    assert row0 % rows == 0

    def blk(off):
        return pl.BlockSpec((rows, tc), lambda r, c, off=off: (rb0 + r, off // tc + c))

    own = pl.BlockSpec((rows, tc), lambda r, c: (r, c))
    w_spec = pl.BlockSpec((width, tc), lambda r, c: (0, c))
    out_shape = jax.ShapeDtypeStruct((total, cols), out_dtype)
    if gated:
        in_specs = [blk(col_offs[0]), blk(col_offs[1]), blk(col_offs[2]), own, w_spec]
        args = (proj, proj, proj, prev8, w)
        out_specs = [own, own]
        out_shape = [out_shape, jax.ShapeDtypeStruct((total, cols), F32)]
    else:
        in_specs = [blk(col_offs[0]), own, w_spec, pl.BlockSpec((1, tc), lambda r, c: (0, c))]
        args = (proj, prev8, w, bias.reshape(1, cols))
        out_specs = own
    return pl.pallas_call(
        functools.partial(_short_conv_kernel, width=width, seq=seq, gated=gated),
        grid=(total // rows, cols // tc),
        in_specs=in_specs,
        out_specs=out_specs,
        out_shape=out_shape,
        scratch_shapes=[pltpu.VMEM((rows + SUBLANES, tc), F32), pltpu.VMEM((rows + SUBLANES, tc), F32)],
        compiler_params=_params("parallel", "parallel"),
        name="gated_conv_short" if gated else "ssd_conv_short",
    )(*args)


def _softplus(x):
    return jnp.maximum(x, 0.0) + jnp.log1p(jnp.exp(-jnp.abs(x)))


def _ssd_kernel(*refs, nseq, dims, chained, gpb):
    q = dims.ssd_chunk
    r_heads = dims.heads_per_group
    p = dims.ssd_head_dim
    gw = dims.group_width
    n = dims.ssd_state
    seg = q // nseq
    if chained:
        (xs_raw, b_raw, c_raw, z_ref, dt_ref, bias_ref, alog_ref, d_ref, nw_ref, sel_ref,
         wx_ref, wb_ref, wc_ref, cbx_ref, cbb_ref, cbc_ref,
         y_ref, hout_ref, h_scr, xs_ref, b_ref, c_ref, tail_x, tail_b, tail_c) = refs
        first = pl.program_id(2) == 0

        @pl.when(first)
        def _():
            h_scr[...] = jnp.zeros_like(h_scr)
            for tail in (tail_x, tail_b, tail_c):
                tail[0:SUBLANES, :] = jnp.zeros((SUBLANES, tail.shape[1]), F32)

        width = wx_ref.shape[0]
        for raw, w_ref, cb_ref, tail, out in ((xs_raw, wx_ref, cbx_ref, tail_x, xs_ref),
                                              (b_raw, wb_ref, cbb_ref, tail_b, b_ref),
                                              (c_raw, wc_ref, cbc_ref, tail_c, c_ref)):
            u = raw[...]
            tail[SUBLANES:SUBLANES + q, :] = u
            acc = w_ref[width - 1:width, :] * u
            for k in range(width - 1):
                lo = SUBLANES - (width - 1 - k)
                acc = acc + w_ref[k:k + 1, :] * tail[lo:lo + q, :]
            tail[0:SUBLANES, :] = tail[q:q + SUBLANES, :]
            yv = acc + cb_ref[...]
            out[...] = yv * jax.nn.sigmoid(yv)
    else:
        (xs_ref, b_ref, c_ref, z_ref, dt_ref, bias_ref, alog_ref, d_ref, nw_ref, sel_ref, h0_ref,
         y_ref, hout_ref) = refs

    for gi in range(gpb):
        gcols = slice(gi * gw, (gi + 1) * gw)
        ncols = slice(gi * n, (gi + 1) * n)
        _ssd_group(
            xs_ref.at[:, gcols], b_ref.at[:, ncols], c_ref.at[:, ncols], z_ref.at[:, gcols],
            dt_ref.at[gi], bias_ref.at[gi], alog_ref.at[gi], d_ref.at[:, gcols], nw_ref.at[:, gcols], sel_ref,
            y_ref.at[:, gcols],
            hout_ref.at[:, gi * r_heads:(gi + 1) * r_heads],
            h_scr.at[gi * r_heads * p:(gi + 1) * r_heads * p, :] if chained else None,
            None if chained else h0_ref.at[:, gi * r_heads:(gi + 1) * r_heads],
            nseq=nseq, dims=dims, chained=chained)


def _ssd_group(xs_ref, b_ref, c_ref, z_ref, dt_ref, bias_ref, alog_ref, d_ref, nw_ref, sel_ref, y_ref,
               hout_ref, h_scr, h0_ref, *, nseq, dims, chained):
    q = dims.ssd_chunk
    r_heads = dims.heads_per_group
    p = dims.ssd_head_dim
    gw = dims.group_width
    seg = q // nseq
    row = lax.broadcasted_iota(jnp.int32, (q, q), 0)
    col = lax.broadcasted_iota(jnp.int32, (q, q), 1)
    same = (row // seg) == (col // seg)
    dt = _softplus(dt_ref[...] + bias_ref[...])
    adt = dt * (-jnp.exp(alog_ref[...]))
    cum_mask = jnp.where(same & (row <= col), 1.0, 0.0).astype(F32)
    acum_t = jnp.dot(adt, cum_mask, precision=HIGHEST, preferred_element_type=F32)
    if nseq == 1:
        atot_t = jnp.broadcast_to(acum_t[:, q - 1:q], (r_heads, q))
    else:
        atot_t = jnp.dot(adt, jnp.where(same, 1.0, 0.0).astype(F32), precision=HIGHEST,
                         preferred_element_type=F32)
    stack = jnp.concatenate(
        [acum_t, dt, dt * jnp.exp(atot_t - acum_t), jnp.exp(acum_t),
         jnp.zeros((q - 4 * r_heads, q), F32)], axis=0)
    cols_form = stack.T
    sel = sel_ref[...]
    hi = cols_form.astype(BF16)
    rest = cols_form - hi.astype(F32)
    mid = rest.astype(BF16)
    low = (rest - mid.astype(F32)).astype(BF16)
    expand = (jnp.dot(hi, sel, preferred_element_type=F32) + jnp.dot(mid, sel, preferred_element_type=F32)
              + jnp.dot(low, sel, preferred_element_type=F32))
    e_dt, e_st, e_ac = expand[:, :gw], expand[:, gw:2 * gw], expand[:, 2 * gw:]

    x = xs_ref[...]
    bb = b_ref[...].astype(BF16)
    cb_ = c_ref[...].astype(BF16)
    cb = lax.dot_general(cb_, bb, (((1,), (1,)), ((), ())), preferred_element_type=F32)
    x_dt = x * e_dt
    x_dt_b = x_dt.astype(BF16)
    causal = same & (row >= col)
    lane = lax.broadcasted_iota(jnp.int32, (q, LANES), 1)
    heads_per_tile = LANES // p

    y_tiles = []
    for tile in range(gw // LANES):
        xt = x_dt_b[:, tile * LANES:(tile + 1) * LANES]
        acc = None
        for k in range(heads_per_tile):
            r = tile * heads_per_tile + k
            segm = cols_form[:, r:r + 1] - acum_t[r:r + 1, :]
            decay = jnp.exp(jnp.where(causal, segm, NEG_INF))
            m_r = (cb * decay).astype(BF16)
            x_r = jnp.where((lane >= k * p) & (lane < (k + 1) * p), xt, jnp.zeros_like(xt))
            part = jnp.dot(m_r, x_r, preferred_element_type=F32)
            acc = part if acc is None else acc + part
        y_tiles.append(acc)
    y = jnp.concatenate(y_tiles, axis=1)

    x_st = x * e_st
    eac_t = jnp.exp(atot_t)

    if chained:
        h = h_scr[...]
        y_off = lax.dot_general(cb_, h.astype(BF16), (((1,), (1,)), ((), ())),
                                preferred_element_type=F32)
        y = y + y_off * e_ac
        s_new = lax.dot_general(x_st.astype(BF16), bb, (((0,), (0,)), ((), ())),
                                preferred_element_type=F32)
        scale = jnp.broadcast_to(eac_t[:, q - 1:q], (r_heads, LANES))
        for r in range(r_heads):
            rows = slice(r * p, (r + 1) * p)
            h_scr[rows, :] = h[rows, :] * jnp.broadcast_to(scale[r:r + 1, :], (p, LANES)) + s_new[rows, :]
        for r in range(r_heads):
            hout_ref[0, r] = h_scr[r * p:(r + 1) * p, :]
    else:
        x_st_t = x_st.T
        rows_q = lax.broadcasted_iota(jnp.int32, (q, gw), 0)
        lanes_q = lax.broadcasted_iota(jnp.int32, (gw, q), 1)
        y_off = jnp.zeros((q, gw), F32)
        for s in range(nseq):
            h_s = h0_ref[s].reshape(r_heads * p, dims.ssd_state)
            y_s = lax.dot_general(cb_, h_s.astype(BF16), (((1,), (1,)), ((), ())),
                                  preferred_element_type=F32)
            y_off = jnp.where(rows_q // seg == s, y_s, y_off)
            xs_s = jnp.where(lanes_q // seg == s, x_st_t, 0.0).astype(BF16)
            s_new = jnp.dot(xs_s, bb, preferred_element_type=F32)
            scale = jnp.broadcast_to(eac_t[:, s * seg:s * seg + 1], (r_heads, LANES))
            for r in range(r_heads):
                rows = slice(r * p, (r + 1) * p)
                hout_ref[s, r] = (h_s[rows, :] * jnp.broadcast_to(scale[r:r + 1, :], (p, LANES))
                                  + s_new[rows, :])
        y = y + y_off * e_ac

    y = y + d_ref[...] * x
    z = z_ref[...]
    y = y * (z * jax.nn.sigmoid(z))
    inv = lax.rsqrt(jnp.mean(y * y, axis=-1, keepdims=True) + EPS)
    y_ref[...] = ((y * inv) * nw_ref[...]).astype(y_ref.dtype)


def _ssd_selector(dims):
    r_heads, p, gw = dims.heads_per_group, dims.ssd_head_dim, dims.group_width
    sel = np.zeros((dims.ssd_chunk, 3 * gw), np.float32)
    for part in range(3):
        for r in range(r_heads):
            sel[(part + 1) * r_heads + r, part * gw + r * p:part * gw + (r + 1) * p] = 1.0
    return jnp.asarray(sel, BF16)


def ssd_block(dims, xbc, proj, dt_t, bias_g, alog_g, d_exp, norm_w, *, row0, nrows, chained, h0=None,
              conv=None):
    q = dims.ssd_chunk
    g = dims.ssd_groups
    gw = dims.group_width
    n = dims.ssd_state
    r_heads, p = dims.heads_per_group, dims.ssd_head_dim
    assert LANES % p == 0 and q == LANES and 4 * r_heads <= q
    sel = _ssd_selector(dims)
    rb0 = row0 // q
    zoff = dims.off_z // gw
    boff = dims.ssd_inner // n
    coff = boff + g

    gpb = 1
    if chained:
        for cand_gpb in (4, 2):
            if all(v % cand_gpb == 0 for v in (g, zoff, boff)):
                gpb = cand_gpb
                break
    gblocks = g // gpb
    if chained:
        nb, nc = dims.batch, dims.seq // q
        grid = (nb, gblocks, nc)
        rowblk = lambda b, gi, c: b * nc + c
        sems = ("parallel", "parallel", "arbitrary")
        nseq = 1
    else:
        nseq = q // dims.dec_seq
        nb = dims.dec_batch // nseq
        grid = (nb, gblocks)
        rowblk = lambda b, gi: b
        sems = ("parallel", "parallel")

    def spec(shape, fn):
        return pl.BlockSpec(shape, fn)

    if chained:
        ix = lambda f: (lambda b, gi, c: f(rowblk(b, gi, c), gi))
    else:
        ix = lambda f: (lambda b, gi: f(rowblk(b, gi), gi))

    in_specs = [
        spec((q, gpb * gw), ix(lambda rb, gi: (rb, gi))),
        spec((q, gpb * n), ix(lambda rb, gi: (rb, boff // gpb + gi))),
        spec((q, gpb * n), ix(lambda rb, gi: (rb, coff // gpb + gi))),
        spec((q, gpb * gw), ix(lambda rb, gi: (rb0 + rb, zoff // gpb + gi))),
        spec((gpb, r_heads, q), ix(lambda rb, gi: (gi, 0, rb0 + rb))),
        spec((gpb, r_heads, 1), ix(lambda rb, gi: (gi, 0, 0))),
        spec((gpb, r_heads, 1), ix(lambda rb, gi: (gi, 0, 0))),
        spec((1, gpb * gw), ix(lambda rb, gi: (0, gi))),
        spec((1, gpb * gw), ix(lambda rb, gi: (0, gi))),
        spec((q, 3 * gw), ix(lambda rb, gi: (0, 0))),
    ]
    args = [xbc, xbc, xbc, proj, dt_t, bias_g, alog_g, d_exp, norm_w, sel]
    y_spec = spec((q, gpb * gw), ix(lambda rb, gi: (rb, gi)))
    if chained:
        conv_w, conv_b = conv
        xw, bw = gpb * gw, gpb * n
        assert dims.off_xbc % xw == 0 and (dims.off_xbc + dims.ssd_inner) % bw == 0 and (g * n) % bw == 0
        x0, b0 = dims.off_xbc // xw, (dims.off_xbc + dims.ssd_inner) // bw
        c0 = b0 + (g * n) // bw
        in_specs[0] = spec((q, xw), ix(lambda rb, gi: (rb0 + rb, x0 + gi)))
        in_specs[1] = spec((q, bw), ix(lambda rb, gi: (rb0 + rb, b0 + gi)))
        in_specs[2] = spec((q, bw), ix(lambda rb, gi: (rb0 + rb, c0 + gi)))
        args[0:3] = [proj, proj, proj]
        width = conv_w.shape[0]
        wb0 = dims.ssd_inner // bw
        wc0 = wb0 + (g * n) // bw
        in_specs += [spec((width, xw), ix(lambda rb, gi: (0, gi))),
                     spec((width, bw), ix(lambda rb, gi: (0, wb0 + gi))),
                     spec((width, bw), ix(lambda rb, gi: (0, wc0 + gi))),
                     spec((1, xw), ix(lambda rb, gi: (0, gi))),
                     spec((1, bw), ix(lambda rb, gi: (0, wb0 + gi))),
                     spec((1, bw), ix(lambda rb, gi: (0, wc0 + gi)))]
        conv_b2 = conv_b.reshape(1, dims.ssd_xbc)
        args += [conv_w, conv_w, conv_w, conv_b2, conv_b2, conv_b2]
        h_spec = pl.BlockSpec((1, gpb * r_heads, p, n), lambda b, gi, c: (b, gi, 0, 0))
        h_shape = jax.ShapeDtypeStruct((nb, dims.ssd_heads, p, n), F32)
        scratch = [pltpu.VMEM((gpb * r_heads * p, n), F32),
                   pltpu.VMEM((q, xw), F32), pltpu.VMEM((q, bw), F32), pltpu.VMEM((q, bw), F32),
                   pltpu.VMEM((q + SUBLANES, xw), F32), pltpu.VMEM((q + SUBLANES, bw), F32),
                   pltpu.VMEM((q + SUBLANES, bw), F32)]
    else:
        in_specs.append(pl.BlockSpec((nseq, gpb * r_heads, p, n), lambda b, gi: (b, gi, 0, 0)))
        args.append(h0)
        h_spec = pl.BlockSpec((nseq, gpb * r_heads, p, n), lambda b, gi: (b, gi, 0, 0))
        h_shape = jax.ShapeDtypeStruct((dims.dec_batch, dims.ssd_heads, p, n), F32)
        scratch = []
    return pl.pallas_call(
        functools.partial(_ssd_kernel, nseq=nseq, dims=dims, chained=chained, gpb=gpb),
        grid=grid,
        in_specs=in_specs,
        out_specs=[y_spec, h_spec],
        out_shape=[jax.ShapeDtypeStruct((nrows, dims.ssd_inner), BF16), h_shape],
        scratch_shapes=scratch,
        compiler_params=_params(*sems),
        name="ssd_prompt" if chained else "ssd_sample",
    )(*args)


def _attend(q, k, v, scale):
    s = lax.dot_general(q.astype(BF16), k.astype(BF16), (((1,), (1,)), ((), ())),
                        preferred_element_type=F32) * scale
    s = s - jnp.max(s, axis=-1, keepdims=True)
    e = jnp.exp(s)
    pr = (e / jnp.sum(e, axis=-1, keepdims=True)).astype(BF16)
    return jnp.dot(pr, v.astype(BF16), preferred_element_type=F32)


def _xattn_kernel(q_ref, k_ref, v_ref, o_ref, *, heads, head_dim):
    for h in range(heads):
        cs = slice(h * head_dim, (h + 1) * head_dim)
        o_ref[:, cs] = _attend(q_ref[:, cs], k_ref[0, :, cs], v_ref[0, :, cs], head_dim ** -0.5).astype(o_ref.dtype)


def _xattn_cached_kernel(q_ref, k_ref, v_ref, o_ref, *, heads, head_dim):
    tq = q_ref.shape[0]
    m = k_ref.shape[1]
    k = k_ref[0].reshape(m * heads, head_dim).astype(BF16)
    v = v_ref[0].reshape(m * heads, head_dim).astype(BF16)
    q = jnp.concatenate([q_ref[:, h * head_dim:(h + 1) * head_dim] for h in range(heads)], axis=0)
    s = lax.dot_general(q.astype(BF16), k, (((1,), (1,)), ((), ())), preferred_element_type=F32)
    s = s * (head_dim ** -0.5)
    row_head = lax.broadcasted_iota(jnp.int32, s.shape, 0) // tq
    col_head = lax.broadcasted_iota(jnp.int32, s.shape, 1) % heads
    s = jnp.where(row_head == col_head, s, NEG_INF)
    s = s - jnp.max(s, axis=-1, keepdims=True)
    e = jnp.exp(s)
    pr = (e / jnp.sum(e, axis=-1, keepdims=True)).astype(BF16)
    o = jnp.dot(pr, v, preferred_element_type=F32)
    for h in range(heads):
        o_ref[:, h * head_dim:(h + 1) * head_dim] = o[h * tq:(h + 1) * tq, :].astype(o_ref.dtype)


def cross_attention(dims, proj, k, v, *, row0, nseq, length, tq=512):
    w = dims.xatt_width
    tq = _pick(length, tq)
    nq = length // tq
    rb0 = row0 // tq
    assert row0 % tq == 0 and dims.off_q % w == 0
    qoff = dims.off_q // w
    hd, nh = dims.xatt_head_dim, dims.xatt_heads
    if k.ndim == 4:
        assert nq == 1 and tq % SUBLANES == 0
        kv_spec = pl.BlockSpec((1, dims.n_mem, nh, hd), lambda b: (b, 0, 0, 0))
        return pl.pallas_call(
            functools.partial(_xattn_cached_kernel, heads=nh, head_dim=hd),
            grid=(nseq,),
            in_specs=[pl.BlockSpec((tq, w), lambda b: (rb0 + b, qoff)), kv_spec, kv_spec],
            out_specs=pl.BlockSpec((tq, w), lambda b: (b, 0)),
            out_shape=jax.ShapeDtypeStruct((nseq * length, w), BF16),
            compiler_params=_params("parallel"),
            name="cross_attention_cached",
        )(proj, k, v)
    kv_spec = pl.BlockSpec((1, dims.n_mem, w), lambda b, i: (b, 0, 0))
    return pl.pallas_call(
        functools.partial(_xattn_kernel, heads=nh, head_dim=hd),
        grid=(nseq, nq),
        in_specs=[pl.BlockSpec((tq, w), lambda b, i: (rb0 + b * nq + i, qoff)), kv_spec, kv_spec],
        out_specs=pl.BlockSpec((tq, w), lambda b, i: (b * nq + i, 0)),
        out_shape=jax.ShapeDtypeStruct((nseq * length, w), BF16),
        compiler_params=_params("parallel", "parallel"),
        name="cross_attention",
    )(proj, k, v)


def _merge_kernel(vap_ref, ybp_ref, ocp_ref, vas_ref, ybs_ref, ocs_ref, wa_ref, wb_ref, wc_ref,
                  ga_ref, gb_ref, gc_ref, o_ref, *, npb):
    i = pl.program_id(1)

    def body(va_ref, yb_ref, oc_ref):
        ha = jnp.dot(va_ref[...], wa_ref[...], preferred_element_type=F32)
        hb = jnp.dot(yb_ref[...], wb_ref[...], preferred_element_type=F32)
        hc = jnp.dot(oc_ref[...], wc_ref[...], preferred_element_type=F32)
        mix = (jax.nn.sigmoid(ga_ref[...]) * ha + jax.nn.sigmoid(gb_ref[...]) * hb
               + jax.nn.sigmoid(gc_ref[...]) * hc)
        o_ref[...] = mix.astype(o_ref.dtype)

    @pl.when(i < npb)
    def _():
        body(vap_ref, ybp_ref, ocp_ref)

    @pl.when(i >= npb)
    def _():
        body(vas_ref, ybs_ref, ocs_ref)


def merge_branches(dims, prompt, sample, wa, wb, wc, proj, tm=512, tn=512):
    d = dims.d_model
    mp, ms = prompt[0].shape[0], sample[0].shape[0]
    tm, tn = _pick(ms, _pick(mp, tm)), _pick(d, tn)
    npb = mp // tm
    g0 = dims.off_gates // tn
    gd = d // tn
    assert dims.off_gates % tn == 0

    def lhs_p(a):
        return pl.BlockSpec((tm, a.shape[1]), lambda j, i: (jnp.minimum(i, npb - 1), 0))

    def lhs_s(a):
        return pl.BlockSpec((tm, a.shape[1]), lambda j, i: (jnp.maximum(i - npb, 0), 0))

    def rhs(width):
        return pl.BlockSpec((width, tn), lambda j, i: (0, j), pipeline_mode=pl.Buffered(1))

    def gate(k):
        return pl.BlockSpec((tm, tn), lambda j, i, k=k: (i, g0 + k * gd + j))

    return pl.pallas_call(
        functools.partial(_merge_kernel, npb=npb),
        grid=(d // tn, (mp + ms) // tm),
        in_specs=[lhs_p(a) for a in prompt] + [lhs_s(a) for a in sample]
        + [rhs(wa.shape[0]), rhs(wb.shape[0]), rhs(wc.shape[0]), gate(0), gate(1), gate(2)],
        out_specs=pl.BlockSpec((tm, tn), lambda j, i: (i, j)),
        out_shape=jax.ShapeDtypeStruct((mp + ms, d), BF16),
        compiler_params=_params("arbitrary", "arbitrary"),
        name="merge_branches",
    )(*prompt, *sample, wa, wb, wc, proj, proj, proj)


def _peer_select_kernel(q_ref, keys_ref, c1_ref, e1_ref, r2_ref, e2_ref, scores, work, tops, cand, ranks, *,
                        dims):
    heads, nk, topk = dims.peer_heads, dims.peer_keys, dims.peer_topk
    half = dims.peer_qdim // 2
    tb = q_ref.shape[0]
    rank = lax.broadcasted_iota(jnp.int32, (topk, tb), 0)

    for k in range(2 * heads):
        qh = q_ref[:, k * half:(k + 1) * half].astype(BF16)
        s = lax.dot_general(keys_ref[k].astype(BF16), qh, (((1,), (1,)), ((), ())),
                            preferred_element_type=F32)
        scores[k] = s
        work[k] = s
        tops[k] = jnp.full((topk, tb), NEG_INF, F32)

    for h in range(heads):
        ranks[h] = jnp.full((nk, tb), float(topk), F32)

    def extract(r, carry):
        rf = jnp.asarray(r, F32)
        for k in range(2 * heads):
            s = work[k]
            m = jnp.max(s, axis=0, keepdims=True)
            hit = s == m
            tops[k] = jnp.where(rank == r, m, tops[k])
            work[k] = jnp.where(hit, NEG_INF, s)
            if k % 2:
                ranks[k // 2] = jnp.where(hit, rf, ranks[k // 2])
        return carry
    lax.fori_loop(0, topk, extract, 0)

    for h in range(heads):
        a, b = tops[2 * h], tops[2 * h + 1]
        cand[h, 0:topk, :] = a[0:1, :] + b
        for pi in range(1, topk):
            lo = topk + (pi - 1) * SUBLANES
            cand[h, lo:lo + SUBLANES, :] = a[pi:pi + 1, :] + b[0:SUBLANES, :]

    def threshold(r, taus):
        new = []
        for h in range(heads):
            c = cand[h]
            m = jnp.max(c, axis=0, keepdims=True)
            cand[h] = jnp.where(c == m, NEG_INF, c)
            new.append(m)
        return tuple(new)
    taus = lax.fori_loop(0, topk, threshold, tuple(jnp.zeros((1, tb), F32) for _ in range(heads)))

    for h in range(heads):
        a, b = tops[2 * h], tops[2 * h + 1]
        tau = taus[h]
        z = None
        for pi in range(topk):
            rows = topk if pi == 0 else SUBLANES
            c = a[pi:pi + 1, :] + b[0:rows, :]
            part = jnp.sum(jnp.where(c >= tau, jnp.exp(c - (a[0:1, :] + b[0:1, :])), 0.0), axis=0, keepdims=True)
            z = part if z is None else z + part
        s1, s2 = scores[2 * h], scores[2 * h + 1]
        count = jnp.zeros_like(s1)
        for qi in range(topk):
            reach = a + b[qi:qi + 1, :] >= tau
            alpha = jnp.min(jnp.where(reach, a, float("inf")), axis=0, keepdims=True)
            count = count + jnp.where(s1 >= alpha, 1.0, 0.0)
        c1_ref[h, :, 0, :] = count
        r2_ref[h] = ranks[h].astype(r2_ref.dtype)
        e1_ref[h, :, 0, :] = jnp.exp(s1 - a[0:1, :]) * (0.5 / z)
        e2_ref[h] = jnp.exp(s2 - b[0:1, :]).astype(e2_ref.dtype)


def peer_select(dims, q, subkeys, tb=128):
    t = q.shape[0]
    heads, nk = dims.peer_heads, dims.peer_keys
    half = dims.peer_qdim // 2
    assert dims.peer_topk >= SUBLANES and heads == SUBLANES
    tab = jax.ShapeDtypeStruct((heads, nk, 1, t), F32)
    tab16 = jax.ShapeDtypeStruct((heads, nk, t), BF16)
    row_spec = pl.BlockSpec((heads, nk, 1, tb), lambda i: (0, 0, 0, i))
    tab_spec = pl.BlockSpec((heads, nk, tb), lambda i: (0, 0, i))
    return pl.pallas_call(
        functools.partial(_peer_select_kernel, dims=dims),
        grid=(t // tb,),
        in_specs=[pl.BlockSpec((tb, q.shape[1]), lambda i: (i, 0)),
                  pl.BlockSpec((2 * heads, nk, half), lambda i: (0, 0, 0))],
        out_specs=[row_spec, row_spec, tab_spec, tab_spec],
        out_shape=[tab, tab, tab16, tab16],
        scratch_shapes=[pltpu.VMEM((2 * heads, nk, tb), F32),
                        pltpu.VMEM((2 * heads, nk, tb), F32),
                        pltpu.VMEM((2 * heads, dims.peer_topk, tb), F32),
                        pltpu.VMEM((heads, dims.peer_topk + (dims.peer_topk - 1) * SUBLANES, tb), F32),
                        pltpu.VMEM((heads, nk, tb), F32)],
        compiler_params=_params("parallel"),
        name="peer_select",
    )(q, subkeys.reshape(2 * heads, nk, half))


def _peer_mix_kernel(x_ref, u_ref, v_ref, c1_ref, e1_ref, r2_ref, e2_ref, o_ref, gw_scr, w_scr, *, dims, sub):
    heads, nk = dims.peer_heads, dims.peer_keys
    eb = pl.program_id(1)
    te = u_ref.shape[0]

    @pl.when(eb == 0)
    def _():
        o_ref[...] = jnp.zeros_like(o_ref)

    tb = x_ref.shape[0]
    tr, tl = 2 * SUBLANES, LANES
    zero = jnp.zeros((tr, tl), BF16)
    for ii in range(te // nk):
        for c0 in range(0, tb, tl):
            cs = slice(c0, c0 + tl)
            acc = [None] * (nk // tr)
            for h in range(heads):
                count = jnp.broadcast_to(c1_ref[h, ii, :, cs], (tr, tl)).astype(BF16)
                e1 = jnp.broadcast_to(e1_ref[h, ii, :, cs], (tr, tl)).astype(BF16)
                for s in range(nk // tr):
                    rs = slice(s * tr, (s + 1) * tr)
                    g1 = jnp.minimum(jnp.maximum(count - r2_ref[h, rs, cs], zero), e1)
                    wh = g1 * e2_ref[h, rs, cs]
                    acc[s] = wh if acc[s] is None else acc[s] + wh
            for s in range(nk // tr):
                w_scr[ii * nk + s * tr:ii * nk + (s + 1) * tr, cs] = acc[s]
    x = x_ref[...]
    for sb in range(te // sub):
        rows = slice(sb * sub, (sb + 1) * sub)
        hh = lax.dot_general(u_ref[rows, :], x, (((1,), (1,)), ((), ())), preferred_element_type=F32)
        gelu2 = hh * (1.0 + lax.erf(hh * (2.0 ** -0.5)))
        gw_scr[rows, :] = gelu2.astype(BF16) * w_scr[rows, :]
        o_ref[...] += lax.dot_general(gw_scr[rows, :], v_ref[rows, :], (((0,), (0,)), ((), ())),
                                      preferred_element_type=F32)


def peer_mix(dims, xn, u, v, c1, e1, r2, e2, tb=512, te=512, sub=512):
    t, d = xn.shape
    heads, nk = dims.peer_heads, dims.peer_keys
    tb, te = _pick(t, tb), _pick(dims.n_experts, te)
    sub = min(sub, te)
    assert te % sub == 0 and sub % nk == 0
    ni = te // nk
    row_spec = pl.BlockSpec((heads, ni, 1, tb), lambda i, e: (0, e, 0, i))
    col_spec = pl.BlockSpec((heads, nk, tb), lambda i, e: (0, 0, i))
    return pl.pallas_call(
        functools.partial(_peer_mix_kernel, dims=dims, sub=sub),
        grid=(t // tb, dims.n_experts // te),
        in_specs=[pl.BlockSpec((tb, d), lambda i, e: (i, 0)),
                  pl.BlockSpec((te, d), lambda i, e: (e, 0)),
                  pl.BlockSpec((te, d), lambda i, e: (e, 0)),
                  row_spec, row_spec, col_spec, col_spec],
        out_specs=pl.BlockSpec((tb, d), lambda i, e: (i, 0)),
        out_shape=jax.ShapeDtypeStruct((t, d), F32),
        scratch_shapes=[pltpu.VMEM((te, tb), BF16), pltpu.VMEM((te, tb), BF16)],
        compiler_params=_params("parallel", "arbitrary"),
        name="peer_mix",
    )(xn, u, v, c1, e1, r2, e2)


def _pad_prev(state, width):
    return jnp.pad(state, ((0, 0), (SUBLANES - (width - 1), 0), (0, 0)))


def forward(dims, x_prompt, x_sample, mem_prompt, cache_mem_k, cache_mem_v, state_conv_a,
            state_ssd_conv, state_ssd, norm_mix, norm_mem, norm_ffn, norm_final, w_in,
            a_conv_w, a_out, ssd_conv_w, ssd_conv_b, ssd_dt_bias, ssd_a_log, ssd_d, ssd_norm,
            ssd_out, w_mem_k, w_mem_v, xatt_out, w_o, peer_wq, peer_subkeys, peer_u, peer_v):
    d = dims.d_model
    tp, ts, t = dims.t_prompt, dims.t_sample, dims.tokens
    g, rh, p, n = dims.ssd_groups, dims.heads_per_group, dims.ssd_head_dim, dims.ssd_state
    bf = lambda a: a.astype(BF16)

    xp2, xs2 = x_prompt.reshape(tp, d), x_sample.reshape(ts, d)

    mn = rmsnorm(mem_prompt.reshape(dims.batch * dims.n_mem, d), norm_mem[0], BF16)
    xw = dims.xatt_width
    k_p = matmul(mn, w_mem_k, F32, tn=512, name="mem_k").reshape(dims.batch, dims.n_mem, xw)
    v_p = matmul(mn, w_mem_v, F32, tn=512, name="mem_v").reshape(dims.batch, dims.n_mem, xw)
    mem_k_p = k_p.reshape(1, dims.batch, dims.n_mem, dims.xatt_heads, dims.xatt_head_dim)
    mem_v_p = v_p.reshape(1, dims.batch, dims.n_mem, dims.xatt_heads, dims.xatt_head_dim)

    xn = rmsnorm_parts(xp2, xs2, norm_mix[0], BF16)
    wt = jnp.swapaxes(w_in, 1, 2)[0]
    dt0 = 3 * dims.a_width + dims.ssd_inner + dims.ssd_xbc
    proj = matmul_wt(xn, wt, [(0, dt0), (dt0 + dims.ssd_heads, dims.proj_width - dt0)], tm=512, tn=1024,
                     name="in_proj")
    dt_t = dt_proj_t(xn, wt, dt0, dims.ssd_heads).reshape(g, rh, t)

    a_cols = (dims.off_ain, dims.off_abg, dims.off_acg)
    zeros_a = jnp.zeros((dims.batch, SUBLANES, dims.a_width), F32)
    va_p, st_a_p = gated_conv(proj, zeros_a, a_conv_w[0], row0=0, nseq=dims.batch, length=dims.seq,
                              cols=dims.a_width, col_offs=a_cols, out_dtype=BF16)
    prev_a = _pad_prev(state_conv_a[0], dims.a_conv).reshape(dims.dec_batch * SUBLANES, dims.a_width)
    va_s, u_s = short_seq_conv(proj, prev_a, a_conv_w[0], row0=tp, nseq=dims.dec_batch, seq=dims.dec_seq,
                               cols=dims.a_width, col_offs=a_cols, gated=True, out_dtype=BF16)
    na = dims.a_conv - 1
    conv_a_p = st_a_p[None, :, SUBLANES - na:, :]
    conv_a_s = u_s.reshape(dims.dec_batch, dims.dec_seq, dims.a_width)[None, :, dims.dec_seq - na:, :]

    prev_b = _pad_prev(state_ssd_conv[0], dims.ssd_conv).reshape(dims.dec_batch * SUBLANES, dims.ssd_xbc)
    xbc_s = short_seq_conv(proj, prev_b, ssd_conv_w[0], row0=tp, nseq=dims.dec_batch, seq=dims.dec_seq,
                           cols=dims.ssd_xbc, col_offs=(dims.off_xbc,), gated=False, bias=ssd_conv_b[0])
    nb = dims.ssd_conv - 1
    x0, x1c = dims.off_xbc, dims.off_xbc + dims.ssd_xbc
    ssd_conv_p = jnp.stack([lax.slice(proj, ((b + 1) * dims.seq - nb, x0), ((b + 1) * dims.seq, x1c))
                            for b in range(dims.batch)])[None]
    ssd_conv_s = lax.slice(proj.reshape(t // dims.dec_seq, dims.dec_seq, dims.proj_width),
                           (tp // dims.dec_seq, dims.dec_seq - nb, x0),
                           (t // dims.dec_seq, dims.dec_seq, x1c))[None]

    bias_g = ssd_dt_bias[0].reshape(g, rh, 1)
    alog_g = ssd_a_log[0].reshape(g, rh, 1)
    d_exp = jnp.repeat(ssd_d[0], p).reshape(1, dims.ssd_inner)
    norm_w = ssd_norm[0].reshape(1, dims.ssd_inner)
    yb_p, h_p = ssd_block(dims, None, proj, dt_t, bias_g, alog_g, d_exp, norm_w, row0=0, nrows=tp, chained=True,
                          conv=(ssd_conv_w[0], ssd_conv_b[0]))
    yb_s, h_s = ssd_block(dims, xbc_s, proj, dt_t, bias_g, alog_g, d_exp, norm_w, row0=tp, nrows=ts,
                          chained=False, h0=state_ssd[0])

    oc_p = cross_attention(dims, proj, k_p, v_p, row0=0, nseq=dims.batch, length=dims.seq)
    oc_s = cross_attention(dims, proj, cache_mem_k[0], cache_mem_v[0], row0=tp, nseq=dims.dec_batch,
                           length=dims.dec_seq)

    mix = merge_branches(dims, (va_p, yb_p, oc_p), (va_s, yb_s, oc_s), bf(a_out[0]), bf(ssd_out[0]),
                         bf(xatt_out[0]), proj)
    x1, xn2 = matmul_residual_norm(mix, bf(w_o[0]), (xp2, xs2), norm_ffn[0], BF16, name="out_proj")

    qp = matmul(xn2, bf(peer_wq[0]), F32, name="peer_query")
    c1, e1, r2, e2 = peer_select(dims, qp, peer_subkeys[0])
    ffn = peer_mix(dims, xn2, cast_layer(peer_u), cast_layer(peer_v), c1, e1, r2, e2)
    y_prompt = rmsnorm(x1, norm_final, F32, residual=ffn, row0=0, nrows=tp).reshape(dims.batch, dims.seq, d)
    y_sample = rmsnorm(x1, norm_final, F32, residual=ffn, row0=tp, nrows=ts).reshape(
        dims.dec_batch, dims.dec_seq, d)
    return (y_prompt, y_sample, mem_k_p, mem_v_p, conv_a_p, ssd_conv_p, h_p[None],
            conv_a_s, ssd_conv_s, h_s[None])


def kernel(x_prompt, x_sample, mem_prompt, cache_mem_k, cache_mem_v, state_conv_a, state_ssd_conv, state_ssd, norm_mix, norm_mem, norm_ffn, norm_final, w_in, a_conv_w, a_out, ssd_conv_w, ssd_conv_b, ssd_dt_bias, ssd_a_log, ssd_d, ssd_norm, ssd_out, w_mem_k, w_mem_v, xatt_out, w_o, peer_wq, peer_subkeys, peer_u, peer_v):
    return forward(FULL, x_prompt, x_sample, mem_prompt, cache_mem_k, cache_mem_v, state_conv_a,
                   state_ssd_conv, state_ssd, norm_mix, norm_mem, norm_ffn, norm_final, w_in,
                   a_conv_w, a_out, ssd_conv_w, ssd_conv_b, ssd_dt_bias, ssd_a_log, ssd_d, ssd_norm,
                   ssd_out, w_mem_k, w_mem_v, xatt_out, w_o, peer_wq, peer_subkeys, peer_u, peer_v)
```

```python
import dataclasses
import functools

import jax
import jax.numpy as jnp
import numpy as np
from jax import lax
from jax.experimental import pallas as pl
from jax.experimental.pallas import tpu as pltpu

F32 = jnp.float32
BF16 = jnp.bfloat16
EPS = 1e-6
HIGHEST = lax.Precision.HIGHEST
NEG_INF = float("-inf")

LANES = 128
SUBLANES = 8
VMEM_LIMIT_BYTES = 56 * 1024 * 1024


@dataclasses.dataclass(frozen=True)
class Dims:
    d_model: int = 4096
    batch: int = 4
    seq: int = 2048
    dec_batch: int = 128
    dec_seq: int = 8
    a_width: int = 2048
    a_conv: int = 3
    ssd_inner: int = 4096
    ssd_head_dim: int = 64
    ssd_groups: int = 8
    ssd_state: int = 128
    ssd_conv: int = 4
    ssd_chunk: int = 128
    n_mem: int = 256
    xatt_heads: int = 4
    xatt_head_dim: int = 512
    peer_heads: int = 8
    peer_keys: int = 128
    peer_topk: int = 16
    peer_qdim: int = 256

    @property
    def ssd_heads(self):
        return self.ssd_inner // self.ssd_head_dim

    @property
    def heads_per_group(self):
        return self.ssd_heads // self.ssd_groups

    @property
    def group_width(self):
        return self.ssd_inner // self.ssd_groups

    @property
    def ssd_xbc(self):
        return self.ssd_inner + 2 * self.ssd_groups * self.ssd_state

    @property
    def xatt_width(self):
        return self.xatt_heads * self.xatt_head_dim

    @property
    def n_experts(self):
        return self.peer_keys * self.peer_keys

    @property
    def t_prompt(self):
        return self.batch * self.seq

    @property
    def t_sample(self):
        return self.dec_batch * self.dec_seq

    @property
    def tokens(self):
        return self.t_prompt + self.t_sample

    @property
    def off_ain(self):
        return 0

    @property
    def off_abg(self):
        return self.a_width

    @property
    def off_acg(self):
        return 2 * self.a_width

    @property
    def off_z(self):
        return 3 * self.a_width

    @property
    def off_xbc(self):
        return self.off_z + self.ssd_inner

    @property
    def off_q(self):
        return self.off_xbc + self.ssd_xbc

    @property
    def off_gates(self):
        return self.off_q + self.xatt_width

    @property
    def proj_width(self):
        return self.off_gates + 3 * self.d_model


FULL = Dims()


def _params(*sem):
    return pltpu.CompilerParams(dimension_semantics=sem, vmem_limit_bytes=VMEM_LIMIT_BYTES)


def _pick(n, pref):
    t = min(n, pref)
    while n % t:
        t //= 2
    return t


def _rmsnorm_rows(x, g_ref, o_ref):
    inv = lax.rsqrt(jnp.mean(x * x, axis=-1, keepdims=True) + EPS)
    o_ref[...] = ((x * inv) * g_ref[...]).astype(o_ref.dtype)


def _rmsnorm_kernel(x_ref, g_ref, o_ref):
    _rmsnorm_rows(x_ref[...], g_ref, o_ref)


def _add_rmsnorm_kernel(x_ref, r_ref, g_ref, o_ref):
    _rmsnorm_rows(x_ref[...] + r_ref[...], g_ref, o_ref)


def _rmsnorm_parts_kernel(xp_ref, xs_ref, g_ref, o_ref, *, npb):
    i = pl.program_id(0)

    @pl.when(i < npb)
    def _():
        _rmsnorm_rows(xp_ref[...], g_ref, o_ref)

    @pl.when(i >= npb)
    def _():
        _rmsnorm_rows(xs_ref[...], g_ref, o_ref)


def rmsnorm(x, g, out_dtype, residual=None, row0=0, nrows=None, tm=256):
    m, d = x.shape
    nrows = m - row0 if nrows is None else nrows
    tm = _pick(nrows, tm)
    rb0 = row0 // tm
    assert row0 % tm == 0
    row = pl.BlockSpec((tm, d), lambda i: (rb0 + i, 0))
    gspec = pl.BlockSpec((1, d), lambda i: (0, 0))
    args = (x,) if residual is None else (x, residual)
    return pl.pallas_call(
        _rmsnorm_kernel if residual is None else _add_rmsnorm_kernel,
        grid=(nrows // tm,),
        in_specs=[row] * len(args) + [gspec],
        out_specs=pl.BlockSpec((tm, d), lambda i: (i, 0)),
        out_shape=jax.ShapeDtypeStruct((nrows, d), out_dtype),
        compiler_params=_params("parallel"),
        name="rmsnorm" if residual is None else "add_rmsnorm",
    )(*args, g.reshape(1, d))


def rmsnorm_parts(xp, xs, g, out_dtype, tm=256):
    d = xp.shape[1]
    tm = _pick(xs.shape[0], _pick(xp.shape[0], tm))
    npb, nsb = xp.shape[0] // tm, xs.shape[0] // tm
    return pl.pallas_call(
        functools.partial(_rmsnorm_parts_kernel, npb=npb),
        grid=(npb + nsb,),
        in_specs=[pl.BlockSpec((tm, d), lambda i: (jnp.minimum(i, npb - 1), 0)),
                  pl.BlockSpec((tm, d), lambda i: (jnp.maximum(i - npb, 0), 0)),
                  pl.BlockSpec((1, d), lambda i: (0, 0))],
        out_specs=pl.BlockSpec((tm, d), lambda i: (i, 0)),
        out_shape=jax.ShapeDtypeStruct((xp.shape[0] + xs.shape[0], d), out_dtype),
        compiler_params=_params("arbitrary"),
        name="rmsnorm_parts",
    )(xp, xs, g.reshape(1, d))


def _cast_kernel(x_ref, o_ref):
    o_ref[...] = x_ref[0].astype(o_ref.dtype)


def cast_layer(w, dtype=BF16, tr=512):
    _, r, c = w.shape
    tr = _pick(r, tr)
    return pl.pallas_call(
        _cast_kernel,
        grid=(r // tr,),
        in_specs=[pl.BlockSpec((1, tr, c), lambda i: (0, i, 0))],
        out_specs=pl.BlockSpec((tr, c), lambda i: (i, 0)),
        out_shape=jax.ShapeDtypeStruct((r, c), dtype),
        compiler_params=_params("parallel"),
        name="cast_layer",
    )(w)


def _mm_kernel(*refs, npb, layer_weight):
    a_ref, b_ref = refs[:2]
    refs = refs[2:]
    if npb is not None:
        rp_ref, rs_ref = refs[:2]
        refs = refs[2:]
    o_ref = refs[0]
    i = pl.program_id(1)
    if layer_weight:
        w_scr = refs[1]

        @pl.when(i == 0)
        def _():
            w_scr[...] = b_ref[0].astype(BF16)

        w = w_scr[...]
    else:
        w = b_ref[...]
    acc = jnp.dot(a_ref[...], w, preferred_element_type=F32)
    if npb is None:
        o_ref[...] = acc.astype(o_ref.dtype)
    else:
        @pl.when(i < npb)
        def _():
            o_ref[...] = (rp_ref[...] + acc).astype(o_ref.dtype)

        @pl.when(i >= npb)
        def _():
            o_ref[...] = (rs_ref[...] + acc).astype(o_ref.dtype)


def matmul(a, b, out_dtype, residual_parts=None, tm=512, tn=1024, name="matmul"):
    m, k = a.shape
    layer_weight = b.ndim == 3
    n = b.shape[-1]
    tm, tn = _pick(m, tm), _pick(n, tn)
    npb = None
    extra_specs, extra_args = [], []
    if residual_parts is not None:
        rp, rs = residual_parts
        tm = _pick(rs.shape[0], _pick(rp.shape[0], tm))
        npb = rp.shape[0] // tm
        extra_specs = [pl.BlockSpec((tm, tn), lambda j, i: (jnp.minimum(i, npb - 1), j)),
                       pl.BlockSpec((tm, tn), lambda j, i: (jnp.maximum(i - npb, 0), j))]
        extra_args = [rp, rs]
    if layer_weight:
        b_spec = pl.BlockSpec((1, k, tn), lambda j, i: (0, 0, j))
        scratch = [pltpu.VMEM((k, tn), BF16)]
    else:
        b_spec = pl.BlockSpec((k, tn), lambda j, i: (0, j))
        scratch = []
    sequential = layer_weight or npb is not None
    return pl.pallas_call(
        functools.partial(_mm_kernel, npb=npb, layer_weight=layer_weight),
        grid=(n // tn, m // tm),
        in_specs=[pl.BlockSpec((tm, k), lambda j, i: (i, 0)), b_spec] + extra_specs,
        out_specs=pl.BlockSpec((tm, tn), lambda j, i: (i, j)),
        out_shape=jax.ShapeDtypeStruct((m, n), out_dtype),
        scratch_shapes=scratch,
        compiler_params=_params(*(("arbitrary", "arbitrary") if sequential else ("parallel", "parallel"))),
        name=name,
    )(a, b, *extra_args)


def _mm_res_norm_kernel(a_ref, w_ref, rp_ref, rs_ref, g_ref, x_ref, xn_ref, row_scr, *, npb, nj, tn):
    i, j = pl.program_id(0), pl.program_id(1)
    acc = jnp.dot(a_ref[...], w_ref[...], preferred_element_type=F32)

    def finish(res_ref):
        x = res_ref[...] + acc
        x_ref[...] = x
        row_scr[j] = x

    @pl.when(i < npb)
    def _():
        finish(rp_ref)

    @pl.when(i >= npb)
    def _():
        finish(rs_ref)

    @pl.when(j == nj - 1)
    def _():
        ssq = None
        for jj in range(nj):
            part = jnp.sum(row_scr[jj] * row_scr[jj], axis=-1, keepdims=True)
            ssq = part if ssq is None else ssq + part
        inv = lax.rsqrt(ssq * (1.0 / (nj * tn)) + EPS)
        for jj in range(nj):
            cs = slice(jj * tn, (jj + 1) * tn)
            xn_ref[:, cs] = ((row_scr[jj] * inv) * g_ref[:, cs]).astype(xn_ref.dtype)


def matmul_residual_norm(a, b, residual_parts, g, norm_dtype, tm=512, tn=1024, name="matmul_res_norm"):
    m, k = a.shape
    n = b.shape[1]
    rp, rs = residual_parts
    tm, tn = _pick(rs.shape[0], _pick(rp.shape[0], _pick(m, tm))), _pick(n, tn)
    npb, nj = rp.shape[0] // tm, n // tn
    return pl.pallas_call(
        functools.partial(_mm_res_norm_kernel, npb=npb, nj=nj, tn=tn),
        grid=(m // tm, nj),
        in_specs=[pl.BlockSpec((tm, k), lambda i, j: (i, 0)),
                  pl.BlockSpec((k, tn), lambda i, j: (0, j)),
                  pl.BlockSpec((tm, tn), lambda i, j: (jnp.minimum(i, npb - 1), j)),
                  pl.BlockSpec((tm, tn), lambda i, j: (jnp.maximum(i - npb, 0), j)),
                  pl.BlockSpec((1, n), lambda i, j: (0, 0))],
        out_specs=[pl.BlockSpec((tm, tn), lambda i, j: (i, j)), pl.BlockSpec((tm, n), lambda i, j: (i, 0))],
        out_shape=[jax.ShapeDtypeStruct((m, n), F32), jax.ShapeDtypeStruct((m, n), norm_dtype)],
        scratch_shapes=[pltpu.VMEM((nj, tm, tn), F32)],
        compiler_params=pltpu.CompilerParams(dimension_semantics=("arbitrary", "arbitrary"),
                                             vmem_limit_bytes=VMEM_LIMIT_BYTES + 2 * 1024 * 1024),
        name=name,
    )(a, b, rp, rs, g.reshape(1, n))


def _mm_wt_kernel(a_ref, wt_ref, o_ref, w_scr):
    @pl.when(pl.program_id(1) == 0)
    def _():
        w_scr[...] = wt_ref[...].astype(BF16)

    o_ref[...] = lax.dot_general(a_ref[...], w_scr[...], (((1,), (1,)), ((), ())),
                                 preferred_element_type=F32).astype(o_ref.dtype)


def matmul_wt(a, wt, segments, out_dtype=F32, tm=1024, tn=512, name="matmul_wt"):
    m, k = a.shape
    tm = _pick(m, tm)
    for start, length in segments:
        tn = _pick(length, tn)
        assert start % SUBLANES == 0
    assert all(length % tn == 0 for _, length in segments)
    starts = np.concatenate([np.arange(s, s + l, tn) for s, l in segments]).astype(np.int32)
    nblk = len(starts)
    bounds = np.cumsum([l // tn for _, l in segments])[:-1]
    shifts = [segments[i + 1][0] - (segments[i][0] + segments[i][1]) for i in range(len(segments) - 1)]

    def row_start(j):
        r = segments[0][0] + j * tn
        for b, sh in zip(bounds, shifts):
            r = r + jnp.where(j >= b, sh, 0)
        return pl.multiple_of(r, SUBLANES)

    return pl.pallas_call(
        _mm_wt_kernel,
        grid=(nblk, m // tm),
        in_specs=[pl.BlockSpec((tm, k), lambda j, i: (i, 0)),
                  pl.BlockSpec((pl.Element(tn), pl.Element(k)), lambda j, i: (row_start(j), 0))],
        out_specs=pl.BlockSpec((tm, tn), lambda j, i: (i, j)),
        out_shape=jax.ShapeDtypeStruct((m, nblk * tn), out_dtype),
        scratch_shapes=[pltpu.VMEM((tn, k), BF16)],
        compiler_params=_params("arbitrary", "arbitrary"),
        name=name,
    )(a, wt)


def _mm_wt_pipelined_kernel(a_hbm, wt_hbm, o_hbm, w_scr, *, segments, tm, tn):
    m, k = a_hbm.shape

    def body(idx, a_ref, wt_ref, o_ref):
        row_block = (idx if isinstance(idx, tuple) else idx.index)[1]

        @pl.when(row_block == 0)
        def _():
            w_scr[...] = wt_ref[...].astype(BF16)

        o_ref[...] = lax.dot_general(a_ref[...], w_scr[...], (((1,), (1,)), ((), ())),
                                     preferred_element_type=F32).astype(o_ref.dtype)

    col = 0
    for start, length in segments:
        pltpu.emit_pipeline(
            body,
            grid=(length // tn, m // tm),
            in_specs=[pl.BlockSpec((tm, k), lambda j, i: (i, 0)),
                      pl.BlockSpec((tn, k), lambda j, i: (j, 0),
                                   pipeline_mode=pl.Buffered(2, use_lookahead=True))],
            out_specs=[pl.BlockSpec((tm, tn), lambda j, i: (i, j))],
            _explicit_indices=True,
        )(a_hbm, wt_hbm.at[pl.ds(start, length)], o_hbm.at[:, pl.ds(col, length)])
        col += length


def matmul_wt_pipelined(a, wt, segments, out_dtype=F32, tm=512, tn=1024, name="matmul_wt"):
    m, k = a.shape
    tm = _pick(m, tm)
    for start, length in segments:
        tn = _pick(length, tn)
        assert start % SUBLANES == 0
    assert all(length % tn == 0 for _, length in segments)
    n = sum(length for _, length in segments)
    return pl.pallas_call(
        functools.partial(_mm_wt_pipelined_kernel, segments=tuple(segments), tm=tm, tn=tn),
        in_specs=[pl.BlockSpec(memory_space=pl.ANY), pl.BlockSpec(memory_space=pl.ANY)],
        out_specs=pl.BlockSpec(memory_space=pl.ANY),
        out_shape=jax.ShapeDtypeStruct((m, n), out_dtype),
        scratch_shapes=[pltpu.VMEM((tn, k), BF16)],
        compiler_params=pltpu.CompilerParams(vmem_limit_bytes=VMEM_LIMIT_BYTES),
        name=name,
    )(a, wt)


def _dt_proj_kernel(a_ref, wt_ref, o_ref):
    o_ref[...] = lax.dot_general(wt_ref[...].astype(BF16), a_ref[...], (((1,), (1,)), ((), ())),
                                 preferred_element_type=F32)


def dt_proj_t(a, wt, start, nrows, tm=1024):
    m, k = a.shape
    tm = _pick(m, tm)
    assert start % SUBLANES == 0
    return pl.pallas_call(
        _dt_proj_kernel,
        grid=(m // tm,),
        in_specs=[pl.BlockSpec((tm, k), lambda i: (i, 0)),
                  pl.BlockSpec((pl.Element(nrows), pl.Element(k)), lambda i: (start, 0))],
        out_specs=pl.BlockSpec((nrows, tm), lambda i: (0, i)),
        out_shape=jax.ShapeDtypeStruct((nrows, m), F32),
        compiler_params=_params("parallel"),
        name="dt_proj",
    )(a, wt)


def _gated_conv_kernel(in_ref, bg_ref, cg_ref, prev_ref, w_ref, o_ref, st_ref, scr, *, width, tl):
    lt = pl.program_id(2)

    @pl.when(lt == 0)
    def _():
        scr[0:SUBLANES, :] = prev_ref[0]

    u = cg_ref[...] * in_ref[...]
    scr[SUBLANES:SUBLANES + tl, :] = u
    acc = w_ref[width - 1:width, :] * u
    for k in range(width - 1):
        lo = SUBLANES - (width - 1 - k)
        acc = acc + w_ref[k:k + 1, :] * scr[lo:lo + tl, :]
    tail = scr[tl:tl + SUBLANES, :]
    scr[0:SUBLANES, :] = tail
    o_ref[...] = (bg_ref[...] * acc).astype(o_ref.dtype)
    st_ref[0] = tail


def gated_conv(proj, prev8, w, *, row0, nseq, length, cols, col_offs, out_dtype, tl=1024, tc=512):
    width = w.shape[0]
    tl, tc = _pick(length, tl), _pick(cols, tc)
    while any(o % tc for o in col_offs):
        tc //= 2
    nl = length // tl
    rb0 = row0 // tl
    assert row0 % tl == 0

    def blk(off):
        return pl.BlockSpec((tl, tc), lambda s, c, l, off=off: (rb0 + s * nl + l, off // tc + c))

    carry_spec = pl.BlockSpec((1, SUBLANES, tc), lambda s, c, l: (s, 0, c))
    return pl.pallas_call(
        functools.partial(_gated_conv_kernel, width=width, tl=tl),
        grid=(nseq, cols // tc, nl),
        in_specs=[blk(col_offs[0]), blk(col_offs[1]), blk(col_offs[2]), carry_spec,
                  pl.BlockSpec((width, tc), lambda s, c, l: (0, c))],
        out_specs=[pl.BlockSpec((tl, tc), lambda s, c, l: (s * nl + l, c)), carry_spec],
        out_shape=[jax.ShapeDtypeStruct((nseq * length, cols), out_dtype),
                   jax.ShapeDtypeStruct((nseq, SUBLANES, cols), F32)],
        scratch_shapes=[pltpu.VMEM((tl + SUBLANES, tc), F32)],
        compiler_params=_params("parallel", "parallel", "arbitrary"),
        name="gated_conv",
    )(proj, proj, proj, prev8, w)


def _short_conv_kernel(*refs, width, seq, gated):
    if gated:
        in_ref, bg_ref, cg_ref, prev_ref, w_ref, o_ref, u_ref, scr_u, scr_p = refs
    else:
        in_ref, prev_ref, w_ref, bias_ref, o_ref, scr_u, scr_p = refs
    rows = in_ref.shape[0]
    u = cg_ref[...] * in_ref[...] if gated else in_ref[...]
    zeros = jnp.zeros((SUBLANES, u.shape[1]), F32)
    scr_u[0:SUBLANES, :] = zeros
    scr_u[SUBLANES:SUBLANES + rows, :] = u
    scr_p[0:rows, :] = prev_ref[...]
    scr_p[rows:rows + SUBLANES, :] = zeros
    pos = lax.broadcasted_iota(jnp.int32, u.shape, 0) % seq
    acc = w_ref[width - 1:width, :] * u
    for k in range(width - 1):
        shift = width - 1 - k
        lo = SUBLANES - shift
        operand = jnp.where(pos >= shift, scr_u[lo:lo + rows, :], scr_p[lo:lo + rows, :])
        acc = acc + w_ref[k:k + 1, :] * operand
    if gated:
        o_ref[...] = (bg_ref[...] * acc).astype(o_ref.dtype)
        u_ref[...] = u
    else:
        y = acc + bias_ref[...]
        o_ref[...] = (y * jax.nn.sigmoid(y)).astype(o_ref.dtype)


def short_seq_conv(proj, prev8, w, *, row0, nseq, seq, cols, col_offs, gated, bias=None, out_dtype=F32,
                   rows=128, tc=2048):
    width = w.shape[0]
    assert seq == SUBLANES and width - 1 <= seq
    total = nseq * seq
    rows, tc = _pick(total, rows), _pick(cols, tc)
    while any(o % tc for o in col_offs):
        tc //= 2
    rb0 = row0 // rows
    assert row0 % rows == 0

    def blk(off):
        return pl.BlockSpec((rows, tc), lambda r, c, off=off: (rb0 + r, off // tc + c))

    own = pl.BlockSpec((rows, tc), lambda r, c: (r, c))
    w_spec = pl.BlockSpec((width, tc), lambda r, c: (0, c))
    out_shape = jax.ShapeDtypeStruct((total, cols), out_dtype)
    if gated:
        in_specs = [blk(col_offs[0]), blk(col_offs[1]), blk(col_offs[2]), own, w_spec]
        args = (proj, proj, proj, prev8, w)
        out_specs = [own, own]
        out_shape = [out_shape, jax.ShapeDtypeStruct((total, cols), F32)]
    else:
        in_specs = [blk(col_offs[0]), own, w_spec, pl.BlockSpec((1, tc), lambda r, c: (0, c))]
        args = (proj, prev8, w, bias.reshape(1, cols))
        out_specs = own
    return pl.pallas_call(
        functools.partial(_short_conv_kernel, width=width, seq=seq, gated=gated),
        grid=(total // rows, cols // tc),
        in_specs=in_specs,
        out_specs=out_specs,
        out_shape=out_shape,
        scratch_shapes=[pltpu.VMEM((rows + SUBLANES, tc), F32), pltpu.VMEM((rows + SUBLANES, tc), F32)],
        compiler_params=_params("parallel", "parallel"),
        name="gated_conv_short" if gated else "ssd_conv_short",
    )(*args)


def _softplus(x):
    return jnp.maximum(x, 0.0) + jnp.log1p(jnp.exp(-jnp.abs(x)))


def _ssd_kernel(*refs, nseq, dims, chained, gpb):
    q = dims.ssd_chunk
    r_heads = dims.heads_per_group
    p = dims.ssd_head_dim
    gw = dims.group_width
    n = dims.ssd_state
    seg = q // nseq
    if chained:
        (xs_raw, b_raw, c_raw, z_ref, dt_ref, bias_ref, alog_ref, d_ref, nw_ref, sel_ref,
         wx_ref, wb_ref, wc_ref, cbx_ref, cbb_ref, cbc_ref,
         y_ref, hout_ref, h_scr, xs_ref, b_ref, c_ref, tail_x, tail_b, tail_c) = refs
        first = pl.program_id(2) == 0

        @pl.when(first)
        def _():
            h_scr[...] = jnp.zeros_like(h_scr)
            for tail in (tail_x, tail_b, tail_c):
                tail[0:SUBLANES, :] = jnp.zeros((SUBLANES, tail.shape[1]), F32)

        width = wx_ref.shape[0]
        for raw, w_ref, cb_ref, tail, out in ((xs_raw, wx_ref, cbx_ref, tail_x, xs_ref),
                                              (b_raw, wb_ref, cbb_ref, tail_b, b_ref),
                                              (c_raw, wc_ref, cbc_ref, tail_c, c_ref)):
            u = raw[...]
            tail[SUBLANES:SUBLANES + q, :] = u
            acc = w_ref[width - 1:width, :] * u
            for k in range(width - 1):
                lo = SUBLANES - (width - 1 - k)
                acc = acc + w_ref[k:k + 1, :] * tail[lo:lo + q, :]
            tail[0:SUBLANES, :] = tail[q:q + SUBLANES, :]
            yv = acc + cb_ref[...]
            out[...] = yv * jax.nn.sigmoid(yv)
    else:
        (xs_ref, b_ref, c_ref, z_ref, dt_ref, bias_ref, alog_ref, d_ref, nw_ref, sel_ref, h0_ref,
         y_ref, hout_ref) = refs

    for gi in range(gpb):
        gcols = slice(gi * gw, (gi + 1) * gw)
        ncols = slice(gi * n, (gi + 1) * n)
        _ssd_group(
            xs_ref.at[:, gcols], b_ref.at[:, ncols], c_ref.at[:, ncols], z_ref.at[:, gcols],
            dt_ref.at[gi], bias_ref.at[gi], alog_ref.at[gi], d_ref.at[:, gcols], nw_ref.at[:, gcols], sel_ref,
            y_ref.at[:, gcols],
            hout_ref.at[:, gi * r_heads:(gi + 1) * r_heads],
            h_scr.at[gi * r_heads * p:(gi + 1) * r_heads * p, :] if chained else None,
            None if chained else h0_ref.at[:, gi * r_heads:(gi + 1) * r_heads],
            nseq=nseq, dims=dims, chained=chained)


def _ssd_group(xs_ref, b_ref, c_ref, z_ref, dt_ref, bias_ref, alog_ref, d_ref, nw_ref, sel_ref, y_ref,
               hout_ref, h_scr, h0_ref, *, nseq, dims, chained):
    q = dims.ssd_chunk
    r_heads = dims.heads_per_group
    p = dims.ssd_head_dim
    gw = dims.group_width
    seg = q // nseq
    row = lax.broadcasted_iota(jnp.int32, (q, q), 0)
    col = lax.broadcasted_iota(jnp.int32, (q, q), 1)
    same = (row // seg) == (col // seg)
    dt = _softplus(dt_ref[...] + bias_ref[...])
    adt = dt * (-jnp.exp(alog_ref[...]))
    cum_mask = jnp.where(same & (row <= col), 1.0, 0.0).astype(F32)
    acum_t = jnp.dot(adt, cum_mask, precision=HIGHEST, preferred_element_type=F32)
    if nseq == 1:
        atot_t = jnp.broadcast_to(acum_t[:, q - 1:q], (r_heads, q))
    else:
        atot_t = jnp.dot(adt, jnp.where(same, 1.0, 0.0).astype(F32), precision=HIGHEST,
                         preferred_element_type=F32)
    stack = jnp.concatenate(
        [acum_t, dt, dt * jnp.exp(atot_t - acum_t), jnp.exp(acum_t),
         jnp.zeros((q - 4 * r_heads, q), F32)], axis=0)
    cols_form = stack.T
    sel = sel_ref[...]
    hi = cols_form.astype(BF16)
    rest = cols_form - hi.astype(F32)
    mid = rest.astype(BF16)
    low = (rest - mid.astype(F32)).astype(BF16)
    expand = (jnp.dot(hi, sel, preferred_element_type=F32) + jnp.dot(mid, sel, preferred_element_type=F32)
              + jnp.dot(low, sel, preferred_element_type=F32))
    e_dt, e_st, e_ac = expand[:, :gw], expand[:, gw:2 * gw], expand[:, 2 * gw:]

    x = xs_ref[...]
    bb = b_ref[...].astype(BF16)
    cb_ = c_ref[...].astype(BF16)
    cb = lax.dot_general(cb_, bb, (((1,), (1,)), ((), ())), preferred_element_type=F32)
    x_dt = x * e_dt
    x_dt_b = x_dt.astype(BF16)
    causal = same & (row >= col)
    lane = lax.broadcasted_iota(jnp.int32, (q, LANES), 1)
    heads_per_tile = LANES // p

    y_tiles = []
    for tile in range(gw // LANES):
        xt = x_dt_b[:, tile * LANES:(tile + 1) * LANES]
        acc = None
        for k in range(heads_per_tile):
            r = tile * heads_per_tile + k
            segm = cols_form[:, r:r + 1] - acum_t[r:r + 1, :]
            decay = jnp.exp(jnp.where(causal, segm, NEG_INF))
            m_r = (cb * decay).astype(BF16)
            x_r = jnp.where((lane >= k * p) & (lane < (k + 1) * p), xt, jnp.zeros_like(xt))
            part = jnp.dot(m_r, x_r, preferred_element_type=F32)
            acc = part if acc is None else acc + part
        y_tiles.append(acc)
    y = jnp.concatenate(y_tiles, axis=1)

    x_st = x * e_st
    eac_t = jnp.exp(atot_t)

    if chained:
        h = h_scr[...]
        y_off = lax.dot_general(cb_, h.astype(BF16), (((1,), (1,)), ((), ())),
                                preferred_element_type=F32)
        y = y + y_off * e_ac
        s_new = lax.dot_general(x_st.astype(BF16), bb, (((0,), (0,)), ((), ())),
                                preferred_element_type=F32)
        scale = jnp.broadcast_to(eac_t[:, q - 1:q], (r_heads, LANES))
        for r in range(r_heads):
            rows = slice(r * p, (r + 1) * p)
            h_scr[rows, :] = h[rows, :] * jnp.broadcast_to(scale[r:r + 1, :], (p, LANES)) + s_new[rows, :]
        for r in range(r_heads):
            hout_ref[0, r] = h_scr[r * p:(r + 1) * p, :]
    else:
        x_st_t = x_st.T
        rows_q = lax.broadcasted_iota(jnp.int32, (q, gw), 0)
        lanes_q = lax.broadcasted_iota(jnp.int32, (gw, q), 1)
        y_off = jnp.zeros((q, gw), F32)
        for s in range(nseq):
            h_s = h0_ref[s].reshape(r_heads * p, dims.ssd_state)
            y_s = lax.dot_general(cb_, h_s.astype(BF16), (((1,), (1,)), ((), ())),
                                  preferred_element_type=F32)
            y_off = jnp.where(rows_q // seg == s, y_s, y_off)
            xs_s = jnp.where(lanes_q // seg == s, x_st_t, 0.0).astype(BF16)
            s_new = jnp.dot(xs_s, bb, preferred_element_type=F32)
            scale = jnp.broadcast_to(eac_t[:, s * seg:s * seg + 1], (r_heads, LANES))
            for r in range(r_heads):
                rows = slice(r * p, (r + 1) * p)
                hout_ref[s, r] = (h_s[rows, :] * jnp.broadcast_to(scale[r:r + 1, :], (p, LANES))
                                  + s_new[rows, :])
        y = y + y_off * e_ac

    y = y + d_ref[...] * x
    z = z_ref[...]
    y = y * (z * jax.nn.sigmoid(z))
    inv = lax.rsqrt(jnp.mean(y * y, axis=-1, keepdims=True) + EPS)
    y_ref[...] = ((y * inv) * nw_ref[...]).astype(y_ref.dtype)


def _ssd_selector(dims):
    r_heads, p, gw = dims.heads_per_group, dims.ssd_head_dim, dims.group_width
    sel = np.zeros((dims.ssd_chunk, 3 * gw), np.float32)
    for part in range(3):
        for r in range(r_heads):
            sel[(part + 1) * r_heads + r, part * gw + r * p:part * gw + (r + 1) * p] = 1.0
    return jnp.asarray(sel, BF16)


def ssd_block(dims, xbc, proj, dt_t, bias_g, alog_g, d_exp, norm_w, *, row0, nrows, chained, h0=None,
              conv=None):
    q = dims.ssd_chunk
    g = dims.ssd_groups
    gw = dims.group_width
    n = dims.ssd_state
    r_heads, p = dims.heads_per_group, dims.ssd_head_dim
    assert LANES % p == 0 and q == LANES and 4 * r_heads <= q
    sel = _ssd_selector(dims)
    rb0 = row0 // q
    zoff = dims.off_z // gw
    boff = dims.ssd_inner // n
    coff = boff + g

    gpb = 1
    if chained:
        for cand_gpb in (4, 2):
            if all(v % cand_gpb == 0 for v in (g, zoff, boff)):
                gpb = cand_gpb
                break
    gblocks = g // gpb
    if chained:
        nb, nc = dims.batch, dims.seq // q
        grid = (nb, gblocks, nc)
        rowblk = lambda b, gi, c: b * nc + c
        sems = ("parallel", "parallel", "arbitrary")
        nseq = 1
    else:
        nseq = q // dims.dec_seq
        nb = dims.dec_batch // nseq
        grid = (nb, gblocks)
        rowblk = lambda b, gi: b
        sems = ("parallel", "parallel")

    def spec(shape, fn):
        return pl.BlockSpec(shape, fn)

    if chained:
        ix = lambda f: (lambda b, gi, c: f(rowblk(b, gi, c), gi))
    else:
        ix = lambda f: (lambda b, gi: f(rowblk(b, gi), gi))

    in_specs = [
        spec((q, gpb * gw), ix(lambda rb, gi: (rb, gi))),
        spec((q, gpb * n), ix(lambda rb, gi: (rb, boff // gpb + gi))),
        spec((q, gpb * n), ix(lambda rb, gi: (rb, coff // gpb + gi))),
        spec((q, gpb * gw), ix(lambda rb, gi: (rb0 + rb, zoff // gpb + gi))),
        spec((gpb, r_heads, q), ix(lambda rb, gi: (gi, 0, rb0 + rb))),
        spec((gpb, r_heads, 1), ix(lambda rb, gi: (gi, 0, 0))),
        spec((gpb, r_heads, 1), ix(lambda rb, gi: (gi, 0, 0))),
        spec((1, gpb * gw), ix(lambda rb, gi: (0, gi))),
        spec((1, gpb * gw), ix(lambda rb, gi: (0, gi))),
        spec((q, 3 * gw), ix(lambda rb, gi: (0, 0))),
    ]
    args = [xbc, xbc, xbc, proj, dt_t, bias_g, alog_g, d_exp, norm_w, sel]
    y_spec = spec((q, gpb * gw), ix(lambda rb, gi: (rb, gi)))
    if chained:
        conv_w, conv_b = conv
        xw, bw = gpb * gw, gpb * n
        assert dims.off_xbc % xw == 0 and (dims.off_xbc + dims.ssd_inner) % bw == 0 and (g * n) % bw == 0
        x0, b0 = dims.off_xbc // xw, (dims.off_xbc + dims.ssd_inner) // bw
        c0 = b0 + (g * n) // bw
        in_specs[0] = spec((q, xw), ix(lambda rb, gi: (rb0 + rb, x0 + gi)))
        in_specs[1] = spec((q, bw), ix(lambda rb, gi: (rb0 + rb, b0 + gi)))
        in_specs[2] = spec((q, bw), ix(lambda rb, gi: (rb0 + rb, c0 + gi)))
        args[0:3] = [proj, proj, proj]
        width = conv_w.shape[0]
        wb0 = dims.ssd_inner // bw
        wc0 = wb0 + (g * n) // bw
        in_specs += [spec((width, xw), ix(lambda rb, gi: (0, gi))),
                     spec((width, bw), ix(lambda rb, gi: (0, wb0 + gi))),
                     spec((width, bw), ix(lambda rb, gi: (0, wc0 + gi))),
                     spec((1, xw), ix(lambda rb, gi: (0, gi))),
                     spec((1, bw), ix(lambda rb, gi: (0, wb0 + gi))),
                     spec((1, bw), ix(lambda rb, gi: (0, wc0 + gi)))]
        conv_b2 = conv_b.reshape(1, dims.ssd_xbc)
        args += [conv_w, conv_w, conv_w, conv_b2, conv_b2, conv_b2]
        h_spec = pl.BlockSpec((1, gpb * r_heads, p, n), lambda b, gi, c: (b, gi, 0, 0))
        h_shape = jax.ShapeDtypeStruct((nb, dims.ssd_heads, p, n), F32)
        scratch = [pltpu.VMEM((gpb * r_heads * p, n), F32),
                   pltpu.VMEM((q, xw), F32), pltpu.VMEM((q, bw), F32), pltpu.VMEM((q, bw), F32),
                   pltpu.VMEM((q + SUBLANES, xw), F32), pltpu.VMEM((q + SUBLANES, bw), F32),
                   pltpu.VMEM((q + SUBLANES, bw), F32)]
    else:
        in_specs.append(pl.BlockSpec((nseq, gpb * r_heads, p, n), lambda b, gi: (b, gi, 0, 0)))
        args.append(h0)
        h_spec = pl.BlockSpec((nseq, gpb * r_heads, p, n), lambda b, gi: (b, gi, 0, 0))
        h_shape = jax.ShapeDtypeStruct((dims.dec_batch, dims.ssd_heads, p, n), F32)
        scratch = []
    return pl.pallas_call(
        functools.partial(_ssd_kernel, nseq=nseq, dims=dims, chained=chained, gpb=gpb),
        grid=grid,
        in_specs=in_specs,
        out_specs=[y_spec, h_spec],
        out_shape=[jax.ShapeDtypeStruct((nrows, dims.ssd_inner), BF16), h_shape],
        scratch_shapes=scratch,
        compiler_params=_params(*sems),
        name="ssd_prompt" if chained else "ssd_sample",
    )(*args)


def _attend(q, k, v, scale):
    s = lax.dot_general(q.astype(BF16), k.astype(BF16), (((1,), (1,)), ((), ())),
                        preferred_element_type=F32) * scale
    s = s - jnp.max(s, axis=-1, keepdims=True)
    e = jnp.exp(s)
    pr = (e / jnp.sum(e, axis=-1, keepdims=True)).astype(BF16)
    return jnp.dot(pr, v.astype(BF16), preferred_element_type=F32)


def _xattn_kernel(q_ref, k_ref, v_ref, o_ref, *, heads, head_dim):
    for h in range(heads):
        cs = slice(h * head_dim, (h + 1) * head_dim)
        o_ref[:, cs] = _attend(q_ref[:, cs], k_ref[0, :, cs], v_ref[0, :, cs], head_dim ** -0.5).astype(o_ref.dtype)


def _xattn_cached_kernel(q_ref, k_ref, v_ref, o_ref, *, heads, head_dim):
    tq = q_ref.shape[0]
    m = k_ref.shape[1]
    k = k_ref[0].reshape(m * heads, head_dim).astype(BF16)
    v = v_ref[0].reshape(m * heads, head_dim).astype(BF16)
    q = jnp.concatenate([q_ref[:, h * head_dim:(h + 1) * head_dim] for h in range(heads)], axis=0)
    s = lax.dot_general(q.astype(BF16), k, (((1,), (1,)), ((), ())), preferred_element_type=F32)
    s = s * (head_dim ** -0.5)
    row_head = lax.broadcasted_iota(jnp.int32, s.shape, 0) // tq
    col_head = lax.broadcasted_iota(jnp.int32, s.shape, 1) % heads
    s = jnp.where(row_head == col_head, s, NEG_INF)
    s = s - jnp.max(s, axis=-1, keepdims=True)
    e = jnp.exp(s)
    pr = (e / jnp.sum(e, axis=-1, keepdims=True)).astype(BF16)
    o = jnp.dot(pr, v, preferred_element_type=F32)
    for h in range(heads):
        o_ref[:, h * head_dim:(h + 1) * head_dim] = o[h * tq:(h + 1) * tq, :].astype(o_ref.dtype)


def cross_attention(dims, proj, k, v, *, row0, nseq, length, tq=512):
    w = dims.xatt_width
    tq = _pick(length, tq)
    nq = length // tq
    rb0 = row0 // tq
    assert row0 % tq == 0 and dims.off_q % w == 0
    qoff = dims.off_q // w
    hd, nh = dims.xatt_head_dim, dims.xatt_heads
    if k.ndim == 4:
        assert nq == 1 and tq % SUBLANES == 0
        kv_spec = pl.BlockSpec((1, dims.n_mem, nh, hd), lambda b: (b, 0, 0, 0))
        return pl.pallas_call(
            functools.partial(_xattn_cached_kernel, heads=nh, head_dim=hd),
            grid=(nseq,),
            in_specs=[pl.BlockSpec((tq, w), lambda b: (rb0 + b, qoff)), kv_spec, kv_spec],
            out_specs=pl.BlockSpec((tq, w), lambda b: (b, 0)),
            out_shape=jax.ShapeDtypeStruct((nseq * length, w), BF16),
            compiler_params=_params("parallel"),
            name="cross_attention_cached",
        )(proj, k, v)
    kv_spec = pl.BlockSpec((1, dims.n_mem, w), lambda b, i: (b, 0, 0))
    return pl.pallas_call(
        functools.partial(_xattn_kernel, heads=nh, head_dim=hd),
        grid=(nseq, nq),
        in_specs=[pl.BlockSpec((tq, w), lambda b, i: (rb0 + b * nq + i, qoff)), kv_spec, kv_spec],
        out_specs=pl.BlockSpec((tq, w), lambda b, i: (b * nq + i, 0)),
        out_shape=jax.ShapeDtypeStruct((nseq * length, w), BF16),
        compiler_params=_params("parallel", "parallel"),
        name="cross_attention",
    )(proj, k, v)


def _merge_kernel(vap_ref, ybp_ref, ocp_ref, vas_ref, ybs_ref, ocs_ref, wa_ref, wb_ref, wc_ref,
                  ga_ref, gb_ref, gc_ref, o_ref, *, npb):
    i = pl.program_id(1)

    def body(va_ref, yb_ref, oc_ref):
        ha = jnp.dot(va_ref[...], wa_ref[...], preferred_element_type=F32)
        hb = jnp.dot(yb_ref[...], wb_ref[...], preferred_element_type=F32)
        hc = jnp.dot(oc_ref[...], wc_ref[...], preferred_element_type=F32)
        mix = (jax.nn.sigmoid(ga_ref[...]) * ha + jax.nn.sigmoid(gb_ref[...]) * hb
               + jax.nn.sigmoid(gc_ref[...]) * hc)
        o_ref[...] = mix.astype(o_ref.dtype)

    @pl.when(i < npb)
    def _():
        body(vap_ref, ybp_ref, ocp_ref)

    @pl.when(i >= npb)
    def _():
        body(vas_ref, ybs_ref, ocs_ref)


def merge_branches(dims, prompt, sample, wa, wb, wc, proj, tm=512, tn=512):
    d = dims.d_model
    mp, ms = prompt[0].shape[0], sample[0].shape[0]
    tm, tn = _pick(ms, _pick(mp, tm)), _pick(d, tn)
    npb = mp // tm
    g0 = dims.off_gates // tn
    gd = d // tn
    assert dims.off_gates % tn == 0

    def lhs_p(a):
        return pl.BlockSpec((tm, a.shape[1]), lambda j, i: (jnp.minimum(i, npb - 1), 0))

    def lhs_s(a):
        return pl.BlockSpec((tm, a.shape[1]), lambda j, i: (jnp.maximum(i - npb, 0), 0))

    def rhs(width):
        return pl.BlockSpec((width, tn), lambda j, i: (0, j), pipeline_mode=pl.Buffered(1))

    def gate(k):
        return pl.BlockSpec((tm, tn), lambda j, i, k=k: (i, g0 + k * gd + j))

    return pl.pallas_call(
        functools.partial(_merge_kernel, npb=npb),
        grid=(d // tn, (mp + ms) // tm),
        in_specs=[lhs_p(a) for a in prompt] + [lhs_s(a) for a in sample]
        + [rhs(wa.shape[0]), rhs(wb.shape[0]), rhs(wc.shape[0]), gate(0), gate(1), gate(2)],
        out_specs=pl.BlockSpec((tm, tn), lambda j, i: (i, j)),
        out_shape=jax.ShapeDtypeStruct((mp + ms, d), BF16),
        compiler_params=_params("arbitrary", "arbitrary"),
        name="merge_branches",
    )(*prompt, *sample, wa, wb, wc, proj, proj, proj)


def _peer_select_kernel(q_ref, keys_ref, c1_ref, e1_ref, r2_ref, e2_ref, scores, work, tops, cand, ranks, *,
                        dims):
    heads, nk, topk = dims.peer_heads, dims.peer_keys, dims.peer_topk
    half = dims.peer_qdim // 2
    tb = q_ref.shape[0]
    rank = lax.broadcasted_iota(jnp.int32, (topk, tb), 0)

    for k in range(2 * heads):
        qh = q_ref[:, k * half:(k + 1) * half].astype(BF16)
        s = lax.dot_general(keys_ref[k].astype(BF16), qh, (((1,), (1,)), ((), ())),
                            preferred_element_type=F32)
        scores[k] = s
        work[k] = s
        tops[k] = jnp.full((topk, tb), NEG_INF, F32)

    for h in range(heads):
        ranks[h] = jnp.full((nk, tb), float(topk), F32)

    def extract(r, carry):
        rf = jnp.asarray(r, F32)
        for k in range(2 * heads):
            s = work[k]
            m = jnp.max(s, axis=0, keepdims=True)
            hit = s == m
            tops[k] = jnp.where(rank == r, m, tops[k])
            work[k] = jnp.where(hit, NEG_INF, s)
            if k % 2:
                ranks[k // 2] = jnp.where(hit, rf, ranks[k // 2])
        return carry
    lax.fori_loop(0, topk, extract, 0)

    for h in range(heads):
        a, b = tops[2 * h], tops[2 * h + 1]
        cand[h, 0:topk, :] = a[0:1, :] + b
        for pi in range(1, topk):
            lo = topk + (pi - 1) * SUBLANES
            cand[h, lo:lo + SUBLANES, :] = a[pi:pi + 1, :] + b[0:SUBLANES, :]

    def threshold(r, taus):
        new = []
        for h in range(heads):
            c = cand[h]
            m = jnp.max(c, axis=0, keepdims=True)
            cand[h] = jnp.where(c == m, NEG_INF, c)
            new.append(m)
        return tuple(new)
    taus = lax.fori_loop(0, topk, threshold, tuple(jnp.zeros((1, tb), F32) for _ in range(heads)))

    for h in range(heads):
        a, b = tops[2 * h], tops[2 * h + 1]
        tau = taus[h]
        z = None
        for pi in range(topk):
            rows = topk if pi == 0 else SUBLANES
            c = a[pi:pi + 1, :] + b[0:rows, :]
            part = jnp.sum(jnp.where(c >= tau, jnp.exp(c - (a[0:1, :] + b[0:1, :])), 0.0), axis=0, keepdims=True)
            z = part if z is None else z + part
        s1, s2 = scores[2 * h], scores[2 * h + 1]
        count = jnp.zeros_like(s1)
        for qi in range(topk):
            reach = a + b[qi:qi + 1, :] >= tau
            alpha = jnp.min(jnp.where(reach, a, float("inf")), axis=0, keepdims=True)
            count = count + jnp.where(s1 >= alpha, 1.0, 0.0)
        c1_ref[h, :, 0, :] = count
        r2_ref[h] = ranks[h].astype(r2_ref.dtype)
        e1_ref[h, :, 0, :] = jnp.exp(s1 - a[0:1, :]) * (0.5 / z)
        e2_ref[h] = jnp.exp(s2 - b[0:1, :]).astype(e2_ref.dtype)


def peer_select(dims, q, subkeys, tb=128):
    t = q.shape[0]
    heads, nk = dims.peer_heads, dims.peer_keys
    half = dims.peer_qdim // 2
    assert dims.peer_topk >= SUBLANES and heads == SUBLANES
    tab = jax.ShapeDtypeStruct((heads, nk, 1, t), F32)
    tab16 = jax.ShapeDtypeStruct((heads, nk, t), BF16)
    row_spec = pl.BlockSpec((heads, nk, 1, tb), lambda i: (0, 0, 0, i))
    tab_spec = pl.BlockSpec((heads, nk, tb), lambda i: (0, 0, i))
    return pl.pallas_call(
        functools.partial(_peer_select_kernel, dims=dims),
        grid=(t // tb,),
        in_specs=[pl.BlockSpec((tb, q.shape[1]), lambda i: (i, 0)),
                  pl.BlockSpec((2 * heads, nk, half), lambda i: (0, 0, 0))],
        out_specs=[row_spec, row_spec, tab_spec, tab_spec],
        out_shape=[tab, tab, tab16, tab16],
        scratch_shapes=[pltpu.VMEM((2 * heads, nk, tb), F32),
                        pltpu.VMEM((2 * heads, nk, tb), F32),
                        pltpu.VMEM((2 * heads, dims.peer_topk, tb), F32),
                        pltpu.VMEM((heads, dims.peer_topk + (dims.peer_topk - 1) * SUBLANES, tb), F32),
                        pltpu.VMEM((heads, nk, tb), F32)],
        compiler_params=_params("parallel"),
        name="peer_select",
    )(q, subkeys.reshape(2 * heads, nk, half))


def _peer_mix_kernel(x_ref, u_ref, v_ref, c1_ref, e1_ref, r2_ref, e2_ref, o_ref, gw_scr, w_scr, *, dims, sub):
    heads, nk = dims.peer_heads, dims.peer_keys
    eb = pl.program_id(1)
    te = u_ref.shape[0]

    @pl.when(eb == 0)
    def _():
        o_ref[...] = jnp.zeros_like(o_ref)

    tb = x_ref.shape[0]
    tr, tl = 2 * SUBLANES, LANES
    zero = jnp.zeros((tr, tl), BF16)
    for ii in range(te // nk):
        for c0 in range(0, tb, tl):
            cs = slice(c0, c0 + tl)
            acc = [None] * (nk // tr)
            for h in range(heads):
                count = jnp.broadcast_to(c1_ref[h, ii, :, cs], (tr, tl)).astype(BF16)
                e1 = jnp.broadcast_to(e1_ref[h, ii, :, cs], (tr, tl)).astype(BF16)
                for s in range(nk // tr):
                    rs = slice(s * tr, (s + 1) * tr)
                    g1 = jnp.minimum(jnp.maximum(count - r2_ref[h, rs, cs], zero), e1)
                    wh = g1 * e2_ref[h, rs, cs]
                    acc[s] = wh if acc[s] is None else acc[s] + wh
            for s in range(nk // tr):
                w_scr[ii * nk + s * tr:ii * nk + (s + 1) * tr, cs] = acc[s]
    x = x_ref[...]
    for sb in range(te // sub):
        rows = slice(sb * sub, (sb + 1) * sub)
        hh = lax.dot_general(u_ref[rows, :], x, (((1,), (1,)), ((), ())), preferred_element_type=F32)
        gelu2 = hh * (1.0 + lax.erf(hh * (2.0 ** -0.5)))
        gw_scr[rows, :] = gelu2.astype(BF16) * w_scr[rows, :]
        o_ref[...] += lax.dot_general(gw_scr[rows, :], v_ref[rows, :], (((0,), (0,)), ((), ())),
                                      preferred_element_type=F32)


def peer_mix(dims, xn, u, v, c1, e1, r2, e2, tb=512, te=512, sub=512):
    t, d = xn.shape
    heads, nk = dims.peer_heads, dims.peer_keys
    tb, te = _pick(t, tb), _pick(dims.n_experts, te)
    sub = min(sub, te)
    assert te % sub == 0 and sub % nk == 0
    ni = te // nk
    row_spec = pl.BlockSpec((heads, ni, 1, tb), lambda i, e: (0, e, 0, i))
    col_spec = pl.BlockSpec((heads, nk, tb), lambda i, e: (0, 0, i))
    return pl.pallas_call(
        functools.partial(_peer_mix_kernel, dims=dims, sub=sub),
        grid=(t // tb, dims.n_experts // te),
        in_specs=[pl.BlockSpec((tb, d), lambda i, e: (i, 0)),
                  pl.BlockSpec((te, d), lambda i, e: (e, 0)),
                  pl.BlockSpec((te, d), lambda i, e: (e, 0)),
                  row_spec, row_spec, col_spec, col_spec],
        out_specs=pl.BlockSpec((tb, d), lambda i, e: (i, 0)),
        out_shape=jax.ShapeDtypeStruct((t, d), F32),
        scratch_shapes=[pltpu.VMEM((te, tb), BF16), pltpu.VMEM((te, tb), BF16)],
        compiler_params=_params("parallel", "arbitrary"),
        name="peer_mix",
    )(xn, u, v, c1, e1, r2, e2)


def _pad_prev(state, width):
    return jnp.pad(state, ((0, 0), (SUBLANES - (width - 1), 0), (0, 0)))


def forward(dims, x_prompt, x_sample, mem_prompt, cache_mem_k, cache_mem_v, state_conv_a,
            state_ssd_conv, state_ssd, norm_mix, norm_mem, norm_ffn, norm_final, w_in,
            a_conv_w, a_out, ssd_conv_w, ssd_conv_b, ssd_dt_bias, ssd_a_log, ssd_d, ssd_norm,
            ssd_out, w_mem_k, w_mem_v, xatt_out, w_o, peer_wq, peer_subkeys, peer_u, peer_v):
    d = dims.d_model
    tp, ts, t = dims.t_prompt, dims.t_sample, dims.tokens
    g, rh, p, n = dims.ssd_groups, dims.heads_per_group, dims.ssd_head_dim, dims.ssd_state
    bf = lambda a: a.astype(BF16)

    xp2, xs2 = x_prompt.reshape(tp, d), x_sample.reshape(ts, d)

    mn = rmsnorm(mem_prompt.reshape(dims.batch * dims.n_mem, d), norm_mem[0], BF16)
    xw = dims.xatt_width
    k_p = matmul(mn, w_mem_k, F32, tn=512, name="mem_k").reshape(dims.batch, dims.n_mem, xw)
    v_p = matmul(mn, w_mem_v, F32, tn=512, name="mem_v").reshape(dims.batch, dims.n_mem, xw)
    mem_k_p = k_p.reshape(1, dims.batch, dims.n_mem, dims.xatt_heads, dims.xatt_head_dim)
    mem_v_p = v_p.reshape(1, dims.batch, dims.n_mem, dims.xatt_heads, dims.xatt_head_dim)

    xn = rmsnorm_parts(xp2, xs2, norm_mix[0], BF16)
    wt = jnp.swapaxes(w_in, 1, 2)[0]
    dt0 = 3 * dims.a_width + dims.ssd_inner + dims.ssd_xbc
    proj = matmul_wt_pipelined(xn, wt, [(0, dt0), (dt0 + dims.ssd_heads, dims.proj_width - dt0)], tm=512,
                               tn=1024, name="in_proj")
    dt_t = dt_proj_t(xn, wt, dt0, dims.ssd_heads).reshape(g, rh, t)

    a_cols = (dims.off_ain, dims.off_abg, dims.off_acg)
    zeros_a = jnp.zeros((dims.batch, SUBLANES, dims.a_width), F32)
    va_p, st_a_p = gated_conv(proj, zeros_a, a_conv_w[0], row0=0, nseq=dims.batch, length=dims.seq,
                              cols=dims.a_width, col_offs=a_cols, out_dtype=BF16)
    prev_a = _pad_prev(state_conv_a[0], dims.a_conv).reshape(dims.dec_batch * SUBLANES, dims.a_width)
    va_s, u_s = short_seq_conv(proj, prev_a, a_conv_w[0], row0=tp, nseq=dims.dec_batch, seq=dims.dec_seq,
                               cols=dims.a_width, col_offs=a_cols, gated=True, out_dtype=BF16)
    na = dims.a_conv - 1
    conv_a_p = st_a_p[None, :, SUBLANES - na:, :]
    conv_a_s = u_s.reshape(dims.dec_batch, dims.dec_seq, dims.a_width)[None, :, dims.dec_seq - na:, :]

    prev_b = _pad_prev(state_ssd_conv[0], dims.ssd_conv).reshape(dims.dec_batch * SUBLANES, dims.ssd_xbc)
    xbc_s = short_seq_conv(proj, prev_b, ssd_conv_w[0], row0=tp, nseq=dims.dec_batch, seq=dims.dec_seq,
                           cols=dims.ssd_xbc, col_offs=(dims.off_xbc,), gated=False, bias=ssd_conv_b[0])
    nb = dims.ssd_conv - 1
    x0, x1c = dims.off_xbc, dims.off_xbc + dims.ssd_xbc
    ssd_conv_p = jnp.stack([lax.slice(proj, ((b + 1) * dims.seq - nb, x0), ((b + 1) * dims.seq, x1c))
                            for b in range(dims.batch)])[None]
    ssd_conv_s = lax.slice(proj.reshape(t // dims.dec_seq, dims.dec_seq, dims.proj_width),
                           (tp // dims.dec_seq, dims.dec_seq - nb, x0),
                           (t // dims.dec_seq, dims.dec_seq, x1c))[None]

    bias_g = ssd_dt_bias[0].reshape(g, rh, 1)
    alog_g = ssd_a_log[0].reshape(g, rh, 1)
    d_exp = jnp.repeat(ssd_d[0], p).reshape(1, dims.ssd_inner)
    norm_w = ssd_norm[0].reshape(1, dims.ssd_inner)
    yb_p, h_p = ssd_block(dims, None, proj, dt_t, bias_g, alog_g, d_exp, norm_w, row0=0, nrows=tp, chained=True,
                          conv=(ssd_conv_w[0], ssd_conv_b[0]))
    yb_s, h_s = ssd_block(dims, xbc_s, proj, dt_t, bias_g, alog_g, d_exp, norm_w, row0=tp, nrows=ts,
                          chained=False, h0=state_ssd[0])

    oc_p = cross_attention(dims, proj, k_p, v_p, row0=0, nseq=dims.batch, length=dims.seq)
    oc_s = cross_attention(dims, proj, cache_mem_k[0], cache_mem_v[0], row0=tp, nseq=dims.dec_batch,
                           length=dims.dec_seq)

    mix = merge_branches(dims, (va_p, yb_p, oc_p), (va_s, yb_s, oc_s), bf(a_out[0]), bf(ssd_out[0]),
                         bf(xatt_out[0]), proj)
    x1, xn2 = matmul_residual_norm(mix, bf(w_o[0]), (xp2, xs2), norm_ffn[0], BF16, name="out_proj")

    qp = matmul(xn2, bf(peer_wq[0]), F32, name="peer_query")
    c1, e1, r2, e2 = peer_select(dims, qp, peer_subkeys[0])
    ffn = peer_mix(dims, xn2, cast_layer(peer_u), cast_layer(peer_v), c1, e1, r2, e2)
    y_prompt = rmsnorm(x1, norm_final, F32, residual=ffn, row0=0, nrows=tp).reshape(dims.batch, dims.seq, d)
    y_sample = rmsnorm(x1, norm_final, F32, residual=ffn, row0=tp, nrows=ts).reshape(
        dims.dec_batch, dims.dec_seq, d)
    return (y_prompt, y_sample, mem_k_p, mem_v_p, conv_a_p, ssd_conv_p, h_p[None],
            conv_a_s, ssd_conv_s, h_s[None])


def kernel(x_prompt, x_sample, mem_prompt, cache_mem_k, cache_mem_v, state_conv_a, state_ssd_conv, state_ssd, norm_mix, norm_mem, norm_ffn, norm_final, w_in, a_conv_w, a_out, ssd_conv_w, ssd_conv_b, ssd_dt_bias, ssd_a_log, ssd_d, ssd_norm, ssd_out, w_mem_k, w_mem_v, xatt_out, w_o, peer_wq, peer_subkeys, peer_u, peer_v):
    return forward(FULL, x_prompt, x_sample, mem_prompt, cache_mem_k, cache_mem_v, state_conv_a,
                   state_ssd_conv, state_ssd, norm_mix, norm_mem, norm_ffn, norm_final, w_in,
                   a_conv_w, a_out, ssd_conv_w, ssd_conv_b, ssd_dt_bias, ssd_a_log, ssd_d, ssd_norm,
                   ssd_out, w_mem_k, w_mem_v, xatt_out, w_o, peer_wq, peer_subkeys, peer_u, peer_v)
```
